```python
import math
import jax, jax.numpy as jnp
from jax import lax
import numpy as np

D_MODEL = 1024
BATCH = 4
SEQ = 8192
DEPTH = 2

N_META = 16
BLK = 128
META_BLK = BLK

MLA_HEADS = 6
MLA_Q_RANK = 256
MLA_KV_RANK = 128
MLA_NOPE = 64
MLA_ROPE = 32
MLA_V = 64
ROPE_THETA = 10000.0

DIFF_HEADS = 4
DIFF_QK = 32
DIFF_V = 2 * DIFF_QK

SWA_HEADS = 6
SWA_KV_HEADS = 2
SWA_HD = 64
WINDOW = 128

REL_BUCKETS = 32
REL_MAX_DIST = 128
N_BIAS_HEADS = DIFF_HEADS + SWA_HEADS

D_FF = -(-(8 * D_MODEL) // (3 * 256)) * 256
NEG_INF = -1e30

IN_WIDTHS = (MLA_Q_RANK, MLA_KV_RANK, MLA_ROPE,
             DIFF_HEADS * 2 * DIFF_QK, DIFF_HEADS * 2 * DIFF_QK, DIFF_HEADS * DIFF_V,
             SWA_HEADS * SWA_HD, SWA_KV_HEADS * SWA_HD, SWA_KV_HEADS * SWA_HD)
D_IN = sum(IN_WIDTHS)
IN_OFFSETS = tuple(sum(IN_WIDTHS[:i + 1]) for i in range(len(IN_WIDTHS) - 1))
D_MIX = MLA_HEADS * MLA_V + DIFF_HEADS * DIFF_V + SWA_HEADS * SWA_HD

kernel_name = "hymba_mla_diff_swa_hybrid"


def rms_norm(x, g, eps=1e-6):
    xf = x.astype(jnp.float32)
    y = xf * lax.rsqrt(jnp.mean(xf * xf, axis=-1, keepdims=True) + eps)
    return (y * g.astype(jnp.float32)).astype(x.dtype)


def rotate(x, cos, sin):
    x1, x2 = jnp.split(x, 2, axis=-1)
    return jnp.concatenate([x1 * cos - x2 * sin, x2 * cos + x1 * sin], axis=-1)


def t5_bucket(q_pos, k_pos):
    n = jnp.maximum(q_pos - k_pos, 0)
    max_exact = REL_BUCKETS // 2
    nf = jnp.maximum(n, max_exact).astype(jnp.float32)
    large = max_exact + (jnp.log(nf / max_exact) / math.log(REL_MAX_DIST / max_exact)
                         * (REL_BUCKETS - max_exact)).astype(jnp.int32)
    large = jnp.minimum(large, REL_BUCKETS - 1)
    return jnp.where(n < max_exact, n, large)


def masked_softmax(logits, mask):
    z = jnp.where(mask, logits.astype(jnp.float32), NEG_INF)
    return jax.nn.softmax(z, axis=-1)


def mla_mixer(c_q, c_kv, k_rope, q_norm, w_qb, kv_norm, w_kvb, cos, sin, idx, valid):
    B, L, _ = c_q.shape
    q = (rms_norm(c_q, q_norm) @ w_qb).reshape(B, L, MLA_HEADS, MLA_NOPE + MLA_ROPE)
    q_nope = q[..., :MLA_NOPE]
    q_rot = rotate(q[..., MLA_NOPE:], cos[None, :, None], sin[None, :, None])
    kv = (rms_norm(c_kv, kv_norm) @ w_kvb).reshape(B, L, MLA_HEADS, MLA_NOPE + MLA_V)
    k_nope, v = kv[..., :MLA_NOPE], kv[..., MLA_NOPE:]
    k_rot = rotate(k_rope, cos[None], sin[None])
    scale = (MLA_NOPE + MLA_ROPE) ** -0.5

    def one_block(i):
        s = i * BLK
        qn = lax.dynamic_slice_in_dim(q_nope, s, BLK, axis=1)
        qr = lax.dynamic_slice_in_dim(q_rot, s, BLK, axis=1)
        logits = (jnp.einsum('bqhd,bkhd->bhqk', qn, k_nope)
                  + jnp.einsum('bqhd,bkd->bhqk', qr, k_rot)) * scale
        q_idx = s + jnp.arange(BLK)
        mask = (idx[None, :] <= q_idx[:, None]) & valid[None, :]
        p = masked_softmax(logits, mask).astype(v.dtype)
        return jnp.einsum('bhqk,bkhd->bqhd', p, v)

    out = lax.map(one_block, jnp.arange(L // BLK))
    return out.transpose(1, 0, 2, 3, 4).reshape(B, L, MLA_HEADS * MLA_V)


def diff_mixer(q, k, v, lam_p, sub_g, lam_init, bias_tab, pos, idx, valid):
    B, L, _ = q.shape
    q = q.reshape(B, L, DIFF_HEADS, 2, DIFF_QK)
    k = k.reshape(B, L, DIFF_HEADS, 2, DIFF_QK)
    v = v.reshape(B, L, DIFF_HEADS, DIFF_V)
    lp = lam_p.astype(jnp.float32)
    lam = jnp.exp(jnp.sum(lp[0] * lp[1])) - jnp.exp(jnp.sum(lp[2] * lp[3])) + lam_init
    scale = DIFF_QK ** -0.5

    def one_block(i):
        s = i * BLK
        qb = lax.dynamic_slice_in_dim(q, s, BLK, axis=1)
        q_idx = s + jnp.arange(BLK)
        q_pos = lax.dynamic_slice_in_dim(pos, s, BLK)
        bias = bias_tab[t5_bucket(q_pos[:, None], pos[None, :])]
        bias = bias.transpose(2, 0, 1).astype(jnp.float32)
        logits = jnp.einsum('bqhcd,bkhcd->bchqk', qb, k).astype(jnp.float32) * scale + bias[None, None]
        mask = (idx[None, :] <= q_idx[:, None]) & valid[None, :]
        p = masked_softmax(logits, mask)
        attn = (p[:, 0] - lam * p[:, 1]).astype(v.dtype)
        return jnp.einsum('bhqk,bkhd->bqhd', attn, v)

    out = lax.map(one_block, jnp.arange(L // BLK))
    out = out.transpose(1, 0, 2, 3, 4).reshape(B, L, DIFF_HEADS, DIFF_V)
    out = rms_norm(out, sub_g) * (1.0 - lam_init)
    return out.reshape(B, L, DIFF_HEADS * DIFF_V)


def swa_mixer(q, k, v, sinks, bias_tab, pos, real):
    B, L, _ = q.shape
    nb = L // BLK
    G, R = SWA_KV_HEADS, SWA_HEADS // SWA_KV_HEADS
    qb = q.reshape(B, nb, BLK, G, R, SWA_HD)
    kb = k.reshape(B, nb, BLK, G, SWA_HD)
    vb = v.reshape(B, nb, BLK, G, SWA_HD)

    def band(t, meta):
        prev = jnp.concatenate([t[:, :1], t[:, :-1]], axis=1)
        meta_b = jnp.broadcast_to(meta[:, None], (B, nb) + meta.shape[1:])
        return jnp.concatenate([meta_b, prev, t], axis=2)

    keys = band(kb, k.reshape(B, L, G, SWA_HD)[:, :N_META])
    vals = band(vb, v.reshape(B, L, G, SWA_HD)[:, :N_META])
    K = N_META + 2 * BLK
    blk = jnp.arange(nb)
    ar = jnp.arange(BLK)
    q_idx = blk[:, None] * BLK + ar[None, :]
    prev_idx = jnp.maximum(blk - 1, 0)[:, None] * BLK + ar[None, :]
    k_idx = jnp.concatenate([jnp.broadcast_to(jnp.arange(N_META), (nb, N_META)), prev_idx, q_idx], axis=1)
    in_band = jnp.arange(K) >= N_META
    q_pos, k_pos = pos[q_idx], pos[k_idx]
    causal = k_idx[:, None, :] <= q_idx[:, :, None]
    window_ok = (q_pos[:, :, None] - k_pos[:, None, :]) < WINDOW
    mask = causal & jnp.where(in_band[None, None, :], real[k_idx][:, None, :] & window_ok, True)
    bias = bias_tab[t5_bucket(q_pos[:, :, None], k_pos[:, None, :])]
    bias = bias.transpose(0, 3, 1, 2).reshape(nb, G, R, BLK, K).astype(jnp.float32)
    logits = jnp.einsum('bnqgrd,bnkgd->bngrqk', qb, keys).astype(jnp.float32) * (SWA_HD ** -0.5) + bias[None]
    logits = jnp.where(mask[None, :, None, None], logits, NEG_INF)
    sink = sinks.astype(jnp.float32).reshape(G, R)[None, None, :, :, None, None]
    m = jnp.maximum(jnp.max(logits, axis=-1, keepdims=True), sink)
    p = jnp.exp(logits - m)
    p = p / (jnp.sum(p, axis=-1, keepdims=True) + jnp.exp(sink - m))
    out = jnp.einsum('bngrqk,bnkgd->bnqgrd', p.astype(v.dtype), vals)
    return out.reshape(B, L, SWA_HEADS * SWA_HD)


def setup_inputs(seed: int = 0) -> dict:
    key = jax.random.key(seed)
    ks = jax.random.split(key, 20)
    n = jax.random.normal
    f32 = jnp.float32
    return {
        "x": n(ks[0], (BATCH, SEQ, D_MODEL), f32),
        "meta_tokens": n(ks[1], (N_META, D_MODEL), f32),
        "rel_bias": 0.5 * n(ks[2], (REL_BUCKETS, N_BIAS_HEADS), f32),
        "attn_norm": 1.0 + 0.02 * n(ks[3], (DEPTH, D_MODEL), f32),
        "w_in": n(ks[4], (DEPTH, D_MODEL, D_IN), f32) * D_MODEL ** -0.5,
        "mla_q_norm": 1.0 + 0.02 * n(ks[5], (DEPTH, MLA_Q_RANK), f32),
        "mla_w_qb": n(ks[6], (DEPTH, MLA_Q_RANK, MLA_HEADS * (MLA_NOPE + MLA_ROPE)), f32) * MLA_Q_RANK ** -0.5,
        "mla_kv_norm": 1.0 + 0.02 * n(ks[7], (DEPTH, MLA_KV_RANK), f32),
        "mla_w_kvb": n(ks[8], (DEPTH, MLA_KV_RANK, MLA_HEADS * (MLA_NOPE + MLA_V)), f32) * MLA_KV_RANK ** -0.5,
        "diff_lambda": 0.1 * n(ks[9], (DEPTH, 4, DIFF_QK), f32),
        "diff_subln": 1.0 + 0.02 * n(ks[10], (DEPTH, DIFF_V), f32),
        "swa_sinks": n(ks[11], (DEPTH, SWA_HEADS), f32),
        "w_out": n(ks[12], (DEPTH, D_MIX, D_MODEL), f32) * D_MIX ** -0.5,
        "ffn_norm": 1.0 + 0.02 * n(ks[13], (DEPTH, D_MODEL), f32),
        "w_gate": n(ks[14], (DEPTH, D_MODEL, D_FF), f32) * D_MODEL ** -0.5,
        "w_up": n(ks[15], (DEPTH, D_MODEL, D_FF), f32) * D_MODEL ** -0.5,
        "w_down": n(ks[16], (DEPTH, D_FF, D_MODEL), f32) * D_FF ** -0.5,
        "final_norm": 1.0 + 0.02 * n(ks[17], (D_MODEL,), f32),
    }


def reference(x, meta_tokens, rel_bias, attn_norm, w_in, mla_q_norm, mla_w_qb, mla_kv_norm, mla_w_kvb,
              diff_lambda, diff_subln, swa_sinks, w_out, ffn_norm, w_gate, w_up, w_down, final_norm):
    B = x.shape[0]
    meta = jnp.broadcast_to(meta_tokens.astype(x.dtype)[None], (B, N_META, D_MODEL))
    pad = jnp.zeros((B, META_BLK - N_META, D_MODEL), x.dtype)
    h = jnp.concatenate([meta, pad, x], axis=1)
    L = h.shape[1]
    idx = jnp.arange(L)
    real = idx >= META_BLK
    valid = real | (idx < N_META)
    pos = jnp.where(real, idx - META_BLK + N_META, jnp.minimum(idx, N_META - 1))
    inv_freq = ROPE_THETA ** (-jnp.arange(0, MLA_ROPE, 2, dtype=jnp.float32) / MLA_ROPE)
    ang = pos.astype(jnp.float32)[:, None] * inv_freq[None, :]
    cos, sin = jnp.cos(ang).astype(x.dtype), jnp.sin(ang).astype(x.dtype)
    bias_b, bias_c = rel_bias[:, :DIFF_HEADS], rel_bias[:, DIFF_HEADS:]

    for l in range(DEPTH):
        hn = rms_norm(h, attn_norm[l])
        proj = hn @ w_in[l]
        c_q, c_kv, k_rope, dq, dk, dv, sq, sk, sv = jnp.split(proj, IN_OFFSETS, axis=-1)
        y_a = mla_mixer(c_q, c_kv, k_rope, mla_q_norm[l], mla_w_qb[l], mla_kv_norm[l], mla_w_kvb[l],
                        cos, sin, idx, valid)
        lam_init = 0.8 - 0.6 * math.exp(-0.3 * l)
        y_b = diff_mixer(dq, dk, dv, diff_lambda[l], diff_subln[l], lam_init, bias_b, pos, idx, valid)
        y_c = swa_mixer(sq, sk, sv, swa_sinks[l], bias_c, pos, real)
        h = h + jnp.concatenate([y_a, y_b, y_c], axis=-1) @ w_out[l]
        hn = rms_norm(h, ffn_norm[l])
        h = h + (jax.nn.silu(hn @ w_gate[l]) * (hn @ w_up[l])) @ w_down[l]

    h = rms_norm(h, final_norm)
    return h[:, META_BLK:]
```

```python
import functools
import math

import numpy as np
import jax
import jax.numpy as jnp
from jax import lax
from jax.experimental import pallas as pl
from jax.experimental.pallas import tpu as pltpu

D_MODEL = 1024
DEPTH = 2
N_META = 16
BLK = 128

MLA_HEADS = 6
MLA_Q_RANK = 256
MLA_KV_RANK = 128
MLA_NOPE = 64
MLA_ROPE = 32
MLA_V = 64
ROPE_THETA = 10000.0

DIFF_HEADS = 4
DIFF_QK = 32
DIFF_V = 64

SWA_HEADS = 6
SWA_KV_HEADS = 2
SWA_REP = SWA_HEADS // SWA_KV_HEADS
SWA_HD = 64
WINDOW = 128

REL_BUCKETS = 32
REL_MAX_DIST = 128
D_FF = 2816
NEG_INF = -1e30
EPS = 1e-6

MLA_SCALE = (MLA_NOPE + MLA_ROPE) ** -0.5
DIFF_SCALE = DIFF_QK ** -0.5
SWA_SCALE = SWA_HD ** -0.5

LANES = 128
D_PROJ = 2048
VMEM_LIMIT = 56 * 1024 * 1024

MLA_TILE = 512
DIFF_TILE = 256
SWA_QTILE = 512
ROW_TILE = 512

f32 = jnp.float32
bf16 = jnp.bfloat16


def _bucket_thresholds():
    n = np.arange(0, 4 * REL_MAX_DIST)
    max_exact = REL_BUCKETS // 2
    nf = np.maximum(n, max_exact).astype(np.float32)
    large = max_exact + (np.log(nf / np.float32(max_exact)) / np.float32(math.log(REL_MAX_DIST / max_exact))
                         * np.float32(REL_BUCKETS - max_exact)).astype(np.int32)
    bucket = np.where(n < max_exact, n, np.minimum(large, REL_BUCKETS - 1))
    assert np.all(np.diff(bucket) >= 0) and bucket[-1] == REL_BUCKETS - 1
    return [int(np.argmax(bucket >= k)) for k in range(1, REL_BUCKETS)]


BUCKET_THR = _bucket_thresholds()
FAR_DIST = BUCKET_THR[-1]
assert FAR_DIST <= BLK - N_META + 1


def _dot(a, b):
    return jnp.dot(a, b, preferred_element_type=f32)


def _dot_nt(a, b):
    return lax.dot_general(a, b, (((1,), (1,)), ((), ())), preferred_element_type=f32)


def _rms(x, g):
    return (x * lax.rsqrt(jnp.mean(x * x, axis=-1, keepdims=True) + EPS)) * g


def _lane(shape):
    return lax.broadcasted_iota(jnp.int32, shape, len(shape) - 1)


def _bias_lookup(n, tab_ref, h):
    acc = jnp.full(n.shape, tab_ref[0, h], f32)
    for k, thr in enumerate(BUCKET_THR, start=1):
        acc = jnp.where(n >= thr, tab_ref[k, h], acc)
    return acc


def _bias_kernel(tab_ref, dd_diag, dd_prev, dd_meta, dd_self, ds_band, ds_meta, ds_self, *, td):
    rc = 32

    def rows_cols(r0, cols):
        a = r0 + lax.broadcasted_iota(jnp.int32, (rc, cols), 0)
        b = lax.broadcasted_iota(jnp.int32, (rc, cols), 1)
        return a, b

    def diff_body(i, carry):
        r0 = pl.multiple_of(i * rc, rc)
        a, b = rows_cols(r0, td)
        a2, j = rows_cols(r0, LANES)
        for h in range(DIFF_HEADS):
            c = tab_ref[REL_BUCKETS - 1, h]
            dd_diag[h, pl.ds(r0, rc), :] = jnp.where(
                b <= a, _bias_lookup(jnp.maximum(a - b, 0), tab_ref, h) - c, NEG_INF)
            dd_prev[h, pl.ds(r0, rc), :] = _bias_lookup(a + td - b, tab_ref, h) - c
            dd_meta[0, h, pl.ds(r0, rc), :] = jnp.where(
                j < N_META, _bias_lookup(N_META + a2 - j, tab_ref, h) - c, NEG_INF)
            dd_meta[1, h, pl.ds(r0, rc), :] = jnp.where(j < N_META, 0.0, NEG_INF)
        return carry

    lax.fori_loop(0, td // rc, diff_body, 0)

    def blk_body(i, carry):
        r0 = pl.multiple_of(i * rc, rc)
        a, b = rows_cols(r0, BLK)
        self_ok = jnp.minimum(a, N_META - 1) >= b
        for h in range(DIFF_HEADS):
            c = tab_ref[REL_BUCKETS - 1, h]
            dd_self[h, pl.ds(r0, rc), :] = jnp.where(
                self_ok, _bias_lookup(jnp.maximum(a - b, 0), tab_ref, h) - c, NEG_INF)
        for h in range(SWA_HEADS):
            hb = DIFF_HEADS + h
            near = _bias_lookup(jnp.maximum(a - b, 0), tab_ref, hb)
            cur = jnp.where(b <= a, near, NEG_INF)
            prev = jnp.where(b > a, _bias_lookup(a + BLK - b, tab_ref, hb), NEG_INF)
            ds_band[0, h, pl.ds(r0, rc), 0:BLK] = cur
            ds_band[0, h, pl.ds(r0, rc), BLK:2 * BLK] = jnp.full((rc, BLK), NEG_INF, f32)
            ds_band[1, h, pl.ds(r0, rc), 0:BLK] = prev
            ds_band[1, h, pl.ds(r0, rc), BLK:2 * BLK] = cur
            ds_meta[0, h, pl.ds(r0, rc), :] = jnp.where(
                b < N_META, _bias_lookup(N_META + a - b, tab_ref, hb), NEG_INF)
            ds_meta[1, h, pl.ds(r0, rc), :] = jnp.where(b < N_META, tab_ref[REL_BUCKETS - 1, hb], NEG_INF)
            ds_self[0, h, pl.ds(r0, rc), :] = jnp.where(self_ok, near, NEG_INF)
        return carry

    lax.fori_loop(0, BLK // rc, blk_body, 0)


def _bias_tiles(rel_bias):
    td = DIFF_TILE
    outs = (
        jax.ShapeDtypeStruct((DIFF_HEADS, td, td), f32),
        jax.ShapeDtypeStruct((DIFF_HEADS, td, td), f32),
        jax.ShapeDtypeStruct((2, DIFF_HEADS, td, LANES), f32),
        jax.ShapeDtypeStruct((DIFF_HEADS, BLK, BLK), f32),
        jax.ShapeDtypeStruct((2, SWA_HEADS, BLK, 2 * BLK), f32),
        jax.ShapeDtypeStruct((2, SWA_HEADS, BLK, LANES), f32),
        jax.ShapeDtypeStruct((1, SWA_HEADS, BLK, LANES), f32),
    )
    return pl.pallas_call(
        functools.partial(_bias_kernel, td=td),
        name="bias_tiles",
        out_shape=outs,
        in_specs=[pl.BlockSpec(memory_space=pltpu.SMEM)],
        compiler_params=pltpu.CompilerParams(vmem_limit_bytes=VMEM_LIMIT),
    )(rel_bias)


def _proj_kernel(h_ref, g_ref, win_ref, qn_ref, wqb_ref, kvn_ref, wkvb_ref, csq_ref, cka_ref, ckb_ref,
                 mq_ref, mk_ref, mv_ref, dq_ref, dk_ref, dv_ref, sq_ref, sk_ref, sv_ref):
    hn = _rms(h_ref[0], g_ref[...]).astype(bf16)
    proj = _dot(hn, win_ref[...])
    c_q = proj[:, 0:256]
    c_kv = proj[:, 256:384]
    rope_a = proj[:, 384:512]
    rope_b = proj[:, 512:640]
    dq_ref[0] = (proj[:, 640:896] * DIFF_SCALE).astype(bf16)
    dk_ref[0] = proj[:, 896:1152].astype(bf16)
    dv_ref[0] = proj[:, 1152:1408].astype(bf16)
    sq_ref[0] = (proj[:, 1408:1792] * SWA_SCALE).astype(bf16)
    sk_ref[0] = proj[:, 1792:1920].astype(bf16)
    sv_ref[0] = proj[:, 1920:2048].astype(bf16)

    q = _dot(_rms(c_q, qn_ref[...]).astype(bf16), wqb_ref[...])
    kv = _dot(_rms(c_kv, kvn_ref[...]).astype(bf16), wkvb_ref[...])
    csq = csq_ref[...]
    k_rot = rope_a * cka_ref[...] + rope_b * ckb_ref[...]
    for hd in range(MLA_HEADS):
        sl = slice(hd * LANES, (hd + 1) * LANES)
        mq_ref[0, :, sl] = (q[:, sl] * csq).astype(bf16)
        mk_ref[0, :, sl] = (kv[:, sl] + k_rot).astype(bf16)
    mv_ref[0] = kv[:, MLA_HEADS * LANES:].astype(bf16)


def _const_spec(shape):
    nd = len(shape)
    return pl.BlockSpec(shape, lambda *_: (0,) * nd)


def _project(h, lw, tabs, tm):
    bsz, t, _ = h.shape
    widths = (768, 768, 384, 256, 256, 256, 384, 128, 128)
    out_shape = tuple(jax.ShapeDtypeStruct((bsz, t, w), bf16) for w in widths)
    tab_spec = pl.BlockSpec((tm, LANES), lambda b, i: (i, 0))
    in_specs = [
        pl.BlockSpec((1, tm, D_MODEL), lambda b, i: (b, i, 0)),
        _const_spec((1, D_MODEL)),
        _const_spec((D_MODEL, D_PROJ)),
        _const_spec((1, MLA_Q_RANK)),
        _const_spec((MLA_Q_RANK, 768)),
        _const_spec((1, MLA_KV_RANK)),
        _const_spec((MLA_KV_RANK, 1152)),
        tab_spec, tab_spec, tab_spec,
    ]
    out_specs = tuple(pl.BlockSpec((1, tm, w), lambda b, i: (b, i, 0)) for w in widths)
    return pl.pallas_call(
        _proj_kernel,
        name="proj_real" if bsz > 1 else "proj_meta",
        out_shape=out_shape,
        grid=(bsz, t // tm),
        in_specs=in_specs,
        out_specs=out_specs,
        compiler_params=pltpu.CompilerParams(
            dimension_semantics=("parallel", "parallel"), vmem_limit_bytes=VMEM_LIMIT),
    )(h, lw["attn_norm"], lw["w_in"], lw["q_norm"], lw["w_qb"], lw["kv_norm"], lw["w_kvb"],
      tabs["csq"], tabs["cka"], tabs["ckb"])


def _flash_update(s, v, m_ref, l_ref, acc_ref, idx):
    m_prev = m_ref[idx]
    m_new = jnp.maximum(m_prev, jnp.max(s, axis=-1, keepdims=True))
    alpha = jnp.exp(m_prev - m_new)
    p = jnp.exp(s - m_new)
    l_ref[idx] = alpha * l_ref[idx] + jnp.sum(p, axis=-1, keepdims=True)
    acc_ref[idx] = alpha * acc_ref[idx] + _dot(p.astype(bf16), v)
    m_ref[idx] = m_new


def _init_stats(m_ref, l_ref, acc_ref):
    m_ref[...] = jnp.full(m_ref.shape, NEG_INF, f32)
    l_ref[...] = jnp.zeros(l_ref.shape, f32)
    acc_ref[...] = jnp.zeros(acc_ref.shape, f32)


def _mla_kernel(*refs, t, has_meta):
    if has_meta:
        q_ref, k_ref, v_ref, mk_ref, mv_ref, o_ref, m_ref, l_ref, acc_ref = refs
    else:
        q_ref, k_ref, v_ref, o_ref, m_ref, l_ref, acc_ref = refs
    qi = pl.program_id(2)
    _init_stats(m_ref, l_ref, acc_ref)

    def head(hh):
        return slice(hh * LANES, (hh + 1) * LANES)

    if has_meta:
        valid = _lane((t, LANES)) < N_META
        for hh in range(2):
            s = _dot_nt(q_ref[0, :, head(hh)], mk_ref[0, :, head(hh)])
            _flash_update(jnp.where(valid, s, NEG_INF), mv_ref[0], m_ref, l_ref, acc_ref, hh)

    def body(kj, carry):
        off = pl.multiple_of(kj * t, t)
        for hh in range(2):
            s = _dot_nt(q_ref[0, :, head(hh)], k_ref[0, pl.ds(off, t), head(hh)])
            _flash_update(s, v_ref[0, pl.ds(off, t), :], m_ref, l_ref, acc_ref, hh)
        return carry

    lax.fori_loop(0, qi, body, 0)

    off = pl.multiple_of(qi * t, t)
    row = lax.broadcasted_iota(jnp.int32, (t, t), 0)
    col = lax.broadcasted_iota(jnp.int32, (t, t), 1)
    if has_meta:
        mask = col <= row
    else:
        mask = col <= jnp.minimum(row, N_META - 1)
    for hh in range(2):
        s = _dot_nt(q_ref[0, :, head(hh)], k_ref[0, pl.ds(off, t), head(hh)])
        _flash_update(jnp.where(mask, s, NEG_INF), v_ref[0, pl.ds(off, t), :], m_ref, l_ref, acc_ref, hh)

    o0 = acc_ref[0] / l_ref[0]
    o1 = acc_ref[1] / l_ref[1]
    o_ref[0] = jnp.where(_lane((t, LANES)) < MLA_V, o0, o1).astype(bf16)


def _mla_attention(mq, mk, mv, meta_kv, t):
    bsz, tlen, _ = mq.shape
    has_meta = meta_kv is not None
    npair = MLA_HEADS // 2
    in_specs = [
        pl.BlockSpec((1, t, 2 * LANES), lambda b, p, i: (b, i, p)),
        pl.BlockSpec((1, tlen, 2 * LANES), lambda b, p, i: (b, 0, p)),
        pl.BlockSpec((1, tlen, LANES), lambda b, p, i: (b, 0, p)),
    ]
    args = [mq, mk, mv]
    if has_meta:
        in_specs += [
            pl.BlockSpec((1, BLK, 2 * LANES), lambda b, p, i: (0, 0, p)),
            pl.BlockSpec((1, BLK, LANES), lambda b, p, i: (0, 0, p)),
        ]
        args += list(meta_kv)
    return pl.pallas_call(
        functools.partial(_mla_kernel, t=t, has_meta=has_meta),
        name="mla_real" if has_meta else "mla_meta",
        out_shape=jax.ShapeDtypeStruct((bsz, tlen, npair * LANES), bf16),
        grid=(bsz, npair, tlen // t),
        in_specs=in_specs,
        out_specs=pl.BlockSpec((1, t, LANES), lambda b, p, i: (b, i, p)),
        scratch_shapes=[
            pltpu.VMEM((2, t, 1), f32),
            pltpu.VMEM((2, t, 1), f32),
            pltpu.VMEM((2, t, LANES), f32),
        ],
        compiler_params=pltpu.CompilerParams(
            dimension_semantics=("parallel", "parallel", "arbitrary"), vmem_limit_bytes=VMEM_LIMIT),
    )(*args)


def _diff_kernel(*refs, t, has_meta, lam_init):
    if has_meta:
        (q_ref, k_ref, v_ref, mk_ref, mv_ref, ddiag_ref, dprev_ref, dmeta_ref, lam_ref, subg_ref,
         o_ref, qs_ref, m_ref, l_ref, acc_ref) = refs
    else:
        (q_ref, k_ref, v_ref, ddiag_ref, lam_ref, subg_ref, o_ref, qs_ref, m_ref, l_ref, acc_ref) = refs
    qi = pl.program_id(2)
    _init_stats(m_ref, l_ref, acc_ref)

    q = q_ref[0]
    group = _lane((t, LANES)) // DIFF_QK
    for j in range(4):
        qs_ref[j * t:(j + 1) * t, :] = jnp.where(group == j, q, jnp.zeros_like(q))

    def logits(k, bias_ref, sel):
        s = _dot_nt(qs_ref[...], k)
        if bias_ref is None:
            return s
        b0 = bias_ref[sel + (0,)]
        b1 = bias_ref[sel + (1,)]
        return s + jnp.concatenate([b0, b0, b1, b1], axis=0)

    if has_meta:
        s = logits(mk_ref[0], dmeta_ref, (0,))
        _flash_update(s, mv_ref[0], m_ref, l_ref, acc_ref, 0)

        def body(kj, carry):
            off = pl.multiple_of(kj * t, t)
            s = logits(k_ref[0, pl.ds(off, t), :], None, None)
            _flash_update(s, v_ref[0, pl.ds(off, t), :], m_ref, l_ref, acc_ref, 0)
            return carry

        lax.fori_loop(0, jnp.maximum(qi - 1, 0), body, 0)

        @pl.when(qi >= 1)
        def _():
            off = pl.multiple_of((qi - 1) * t, t)
            s = logits(k_ref[0, pl.ds(off, t), :], dprev_ref, ())
            _flash_update(s, v_ref[0, pl.ds(off, t), :], m_ref, l_ref, acc_ref, 0)

    off = pl.multiple_of(qi * t, t)
    s = logits(k_ref[0, pl.ds(off, t), :], ddiag_ref, ())
    _flash_update(s, v_ref[0, pl.ds(off, t), :], m_ref, l_ref, acc_ref, 0)

    lp = lam_ref[...]
    lam = (jnp.exp(jnp.sum(lp[0:1] * lp[1:2], axis=-1, keepdims=True))
           - jnp.exp(jnp.sum(lp[2:3] * lp[3:4], axis=-1, keepdims=True)) + lam_init)
    lane = _lane((t, LANES))
    outs = []
    for hh in range(2):
        r0, r1 = 2 * hh * t, (2 * hh + 1) * t
        o = (acc_ref[0, r0:r0 + t] / l_ref[0, r0:r0 + t]
             - lam * (acc_ref[0, r1:r1 + t] / l_ref[0, r1:r1 + t]))
        mine = (lane // DIFF_V) == hh
        ms = jnp.sum(jnp.where(mine, o * o, 0.0), axis=-1, keepdims=True) * (1.0 / DIFF_V)
        outs.append(((o * lax.rsqrt(ms + EPS)) * subg_ref[...]) * (1.0 - lam_init))
    o_ref[0] = jnp.where(lane < DIFF_V, outs[0], outs[1]).astype(bf16)


def _diff_attention(dq, dk, dv, meta_kv, tiles, lam_p, sub_g, lam_init, t):
    bsz, tlen, _ = dq.shape
    has_meta = meta_kv is not None
    npair = DIFF_HEADS // 2
    in_specs = [
        pl.BlockSpec((1, t, LANES), lambda b, p, i: (b, i, p)),
        pl.BlockSpec((1, tlen, LANES), lambda b, p, i: (b, 0, p)),
        pl.BlockSpec((1, tlen, LANES), lambda b, p, i: (b, 0, p)),
    ]
    args = [dq, dk, dv]
    if has_meta:
        in_specs += [
            pl.BlockSpec((1, BLK, LANES), lambda b, p, i: (0, 0, p)),
            pl.BlockSpec((1, BLK, LANES), lambda b, p, i: (0, 0, p)),
            pl.BlockSpec((2, t, t), lambda b, p, i: (p, 0, 0)),
            pl.BlockSpec((2, t, t), lambda b, p, i: (p, 0, 0)),
            pl.BlockSpec((1, 2, t, LANES), lambda b, p, i: (jnp.minimum(i, 1), p, 0, 0)),
        ]
        args += list(meta_kv) + [tiles["dd_diag"], tiles["dd_prev"], tiles["dd_meta"]]
    else:
        in_specs += [pl.BlockSpec((2, t, t), lambda b, p, i: (p, 0, 0))]
        args += [tiles["dd_self"]]
    in_specs += [_const_spec((4, DIFF_QK)), _const_spec((1, LANES))]
    args += [lam_p, sub_g]
    return pl.pallas_call(
        functools.partial(_diff_kernel, t=t, has_meta=has_meta, lam_init=lam_init),
        name="diff_real" if has_meta else "diff_meta",
        out_shape=jax.ShapeDtypeStruct((bsz, tlen, npair * LANES), bf16),
        grid=(bsz, npair, tlen // t),
        in_specs=in_specs,
        out_specs=pl.BlockSpec((1, t, LANES), lambda b, p, i: (b, i, p)),
        scratch_shapes=[
            pltpu.VMEM((4 * t, LANES), bf16),
            pltpu.VMEM((1, 4 * t, 1), f32),
            pltpu.VMEM((1, 4 * t, 1), f32),
            pltpu.VMEM((1, 4 * t, LANES), f32),
        ],
        compiler_params=pltpu.CompilerParams(
            dimension_semantics=("parallel", "parallel", "arbitrary"), vmem_limit_bytes=VMEM_LIMIT),
    )(*args)


def _swa_kernel(*refs, tq, has_band):
    if has_band:
        q_ref, k_ref, v_ref, mk_ref, mv_ref, dband_ref, dmeta_ref, sink_ref, o_ref = refs
    else:
        q_ref, mk_ref, mv_ref, dmeta_ref, sink_ref, o_ref = refs
    qi = pl.program_id(1)
    lane = _lane((BLK, LANES))
    row2 = lax.broadcasted_iota(jnp.int32, (2 * BLK, 1), 0)
    n_meta_var = dmeta_ref.shape[0]

    def block(sub, carry):
        blk = qi * (tq // BLK) + sub
        r0 = pl.multiple_of(sub * BLK, BLK)
        mvar = jnp.minimum(blk, n_meta_var - 1)
        if has_band:
            bvar = jnp.minimum(blk, 1)
            koff = pl.multiple_of(jnp.maximum(blk - 1, 0) * BLK, BLK)
            kband = k_ref[0, pl.ds(koff, 2 * BLK), :]
            vband = v_ref[0, pl.ds(koff, 2 * BLK), :]
        for r in range(SWA_REP):
            q2 = q_ref[0, pl.ds(r0, BLK), r * LANES:(r + 1) * LANES]
            zero = jnp.zeros_like(q2)
            qs = jnp.concatenate([jnp.where(lane < SWA_HD, q2, zero), jnp.where(lane >= SWA_HD, q2, zero)],
                                 axis=0)
            h0, h1 = r, SWA_REP + r
            sink = jnp.where(row2 < BLK, sink_ref[h0], sink_ref[h1])
            s_meta = _dot_nt(qs, mk_ref[0]) + jnp.concatenate(
                [dmeta_ref[mvar, h0], dmeta_ref[mvar, h1]], axis=0)
            m = jnp.maximum(jnp.max(s_meta, axis=-1, keepdims=True), sink)
            if has_band:
                s_band = _dot_nt(qs, kband) + jnp.concatenate(
                    [dband_ref[bvar, h0], dband_ref[bvar, h1]], axis=0)
                m = jnp.maximum(m, jnp.max(s_band, axis=-1, keepdims=True))
            p_meta = jnp.exp(s_meta - m)
            denom = jnp.sum(p_meta, axis=-1, keepdims=True) + jnp.exp(sink - m)
            acc = _dot(p_meta.astype(bf16), mv_ref[0])
            if has_band:
                p_band = jnp.exp(s_band - m)
                denom = denom + jnp.sum(p_band, axis=-1, keepdims=True)
                acc = acc + _dot(p_band.astype(bf16), vband)
            o = acc / denom
            o_ref[0, pl.ds(r0, BLK), r * LANES:(r + 1) * LANES] = jnp.where(
                lane < SWA_HD, o[0:BLK], o[BLK:2 * BLK]).astype(bf16)
        return carry

    lax.fori_loop(0, tq // BLK, block, 0)


def _swa_attention(sq, sk, sv, meta_kv, tiles, sinks, tq):
    bsz, tlen, _ = sq.shape
    has_band = meta_kv is not None
    width = SWA_REP * LANES
    smem = pl.BlockSpec(memory_space=pltpu.SMEM)
    if has_band:
        in_specs = [
            pl.BlockSpec((1, tq, width), lambda b, i: (b, i, 0)),
            pl.BlockSpec((1, tlen, LANES), lambda b, i: (b, 0, 0)),
            pl.BlockSpec((1, tlen, LANES), lambda b, i: (b, 0, 0)),
            _const_spec((1, BLK, LANES)),
            _const_spec((1, BLK, LANES)),
            _const_spec(tiles["ds_band"].shape),
            _const_spec(tiles["ds_meta"].shape),
            smem,
        ]
        args = [sq, sk, sv, meta_kv[0], meta_kv[1], tiles["ds_band"], tiles["ds_meta"], sinks]
    else:
        in_specs = [
            pl.BlockSpec((1, tq, width), lambda b, i: (b, i, 0)),
            _const_spec((1, BLK, LANES)),
            _const_spec((1, BLK, LANES)),
            _const_spec(tiles["ds_self"].shape),
            smem,
        ]
        args = [sq, sk, sv, tiles["ds_self"], sinks]
    return pl.pallas_call(
        functools.partial(_swa_kernel, tq=tq, has_band=has_band),
        name="swa_real" if has_band else "swa_meta",
        out_shape=jax.ShapeDtypeStruct((bsz, tlen, width), bf16),
        grid=(bsz, tlen // tq),
        in_specs=in_specs,
        out_specs=pl.BlockSpec((1, tq, width), lambda b, i: (b, i, 0)),
        compiler_params=pltpu.CompilerParams(
            dimension_semantics=("parallel", "parallel"), vmem_limit_bytes=VMEM_LIMIT),
    )(*args)


def _out_ffn_kernel(h_ref, ya_ref, yb_ref, yc_ref, woa_ref, wob_ref, woc_ref, g_ref, wg_ref, wu_ref, wd_ref,
                    fg_ref, o_ref, *, final):
    h1 = (h_ref[0] + _dot(ya_ref[0], woa_ref[...]) + _dot(yb_ref[0], wob_ref[...])
          + _dot(yc_ref[0], woc_ref[...]))
    hn = _rms(h1, g_ref[...]).astype(bf16)
    gate = _dot(hn, wg_ref[...])
    up = _dot(hn, wu_ref[...])
    act = ((gate * jax.nn.sigmoid(gate)) * up).astype(bf16)
    h2 = h1 + _dot(act, wd_ref[...])
    if final:
        h2 = _rms(h2, fg_ref[...])
    o_ref[0] = h2


def _single(shape):
    nd = len(shape)
    return pl.BlockSpec(shape, lambda *_: (0,) * nd, pipeline_mode=pl.Buffered(1))


def _out_ffn(h, ya, yb, yc, lw, final_g, final, tm):
    bsz, t, _ = h.shape

    def row(w):
        return pl.BlockSpec((1, tm, w), lambda b, i: (b, i, 0))

    in_specs = [
        row(D_MODEL), row(384), row(256), row(384),
        _single((384, D_MODEL)), _single((256, D_MODEL)), _single((384, D_MODEL)),
        _single((1, D_MODEL)),
        _single((D_MODEL, D_FF)), _single((D_MODEL, D_FF)), _single((D_FF, D_MODEL)),
        _single((1, D_MODEL)),
    ]
    return pl.pallas_call(
        functools.partial(_out_ffn_kernel, final=final),
        name="out_ffn_real" if bsz > 1 else "out_ffn_meta",
        out_shape=jax.ShapeDtypeStruct((bsz, t, D_MODEL), f32),
        grid=(bsz, t // tm),
        in_specs=in_specs,
        out_specs=row(D_MODEL),
        compiler_params=pltpu.CompilerParams(
            dimension_semantics=("parallel", "parallel"), vmem_limit_bytes=VMEM_LIMIT),
    )(h, ya, yb, yc, lw["wo_a"], lw["wo_b"], lw["wo_c"], lw["ffn_norm"], lw["w_gate"], lw["w_up"],
      lw["w_down"], final_g)


def _rot_half_cols(w):
    half = MLA_ROPE // 2
    return jnp.concatenate([-w[:, half:], w[:, :half]], axis=1)


def _layer_weights(l, attn_norm, w_in, mla_q_norm, mla_w_qb, mla_kv_norm, mla_w_kvb, w_out, ffn_norm,
                   w_gate, w_up, w_down):
    w = w_in[l]
    o = 0
    parts = {}
    for name, width in (("c_q", 256), ("c_kv", 128), ("k_rope", 32), ("dq", 256), ("dk", 256), ("dv", 256),
                        ("sq", 384), ("sk", 128), ("sv", 128)):
        parts[name] = w[:, o:o + width]
        o += width
    kr = parts["k_rope"]
    ksw = _rot_half_cols(kr)
    z64 = jnp.zeros((D_MODEL, 64), f32)
    swa_order = [g * SWA_REP + r for r in range(SWA_REP) for g in range(SWA_KV_HEADS)]
    sq = jnp.concatenate([parts["sq"][:, h * SWA_HD:(h + 1) * SWA_HD] for h in swa_order], axis=1)
    w_in_p = jnp.concatenate([parts["c_q"], parts["c_kv"], z64, kr, kr, z64, ksw, ksw, parts["dq"], parts["dk"],
                              parts["dv"], sq, parts["sk"], parts["sv"]], axis=1)
    assert w_in_p.shape == (D_MODEL, D_PROJ)

    wq = mla_w_qb[l]
    q_cols = []
    for h in range(MLA_HEADS):
        base = h * (MLA_NOPE + MLA_ROPE)
        rope = wq[:, base + MLA_NOPE:base + MLA_NOPE + MLA_ROPE]
        q_cols += [wq[:, base:base + MLA_NOPE], rope, _rot_half_cols(rope)]
    w_qb_p = jnp.concatenate(q_cols, axis=1)

    wkv = mla_w_kvb[l]
    zk = jnp.zeros((MLA_KV_RANK, LANES - MLA_NOPE), f32)
    k_cols, v_cols = [], []
    for h in range(MLA_HEADS):
        base = h * (MLA_NOPE + MLA_V)
        k_cols += [wkv[:, base:base + MLA_NOPE], zk]
        v_cols.append(wkv[:, base + MLA_NOPE:base + MLA_NOPE + MLA_V])
    w_kvb_p = jnp.concatenate(k_cols + v_cols, axis=1)

    wo = w_out[l]
    na = MLA_HEADS * MLA_V
    nb = DIFF_HEADS * DIFF_V
    wo_c = wo[na + nb:]
    wo_c = jnp.concatenate([wo_c[h * SWA_HD:(h + 1) * SWA_HD] for h in swa_order], axis=0)
    return {
        "attn_norm": attn_norm[l][None], "w_in": w_in_p.astype(bf16),
        "q_norm": mla_q_norm[l][None], "w_qb": w_qb_p.astype(bf16),
        "kv_norm": mla_kv_norm[l][None], "w_kvb": w_kvb_p.astype(bf16),
        "wo_a": wo[:na].astype(bf16), "wo_b": wo[na:na + nb].astype(bf16), "wo_c": wo_c.astype(bf16),
        "ffn_norm": ffn_norm[l][None],
        "w_gate": w_gate[l].astype(bf16), "w_up": w_up[l].astype(bf16), "w_down": w_down[l].astype(bf16),
    }


def _rope_tables(pos):
    inv_freq = ROPE_THETA ** (-jnp.arange(0, MLA_ROPE, 2, dtype=f32) / MLA_ROPE)
    ang = pos.astype(f32)[:, None] * inv_freq[None, :]
    cos, sin = jnp.cos(ang), jnp.sin(ang)
    n = pos.shape[0]
    cc = jnp.concatenate([cos, cos], axis=1)
    ss = jnp.concatenate([sin, sin], axis=1)
    z = jnp.zeros((n, 64), f32)
    return {
        "csq": jnp.concatenate([jnp.ones((n, 64), f32), cc, ss], axis=1) * MLA_SCALE,
        "cka": jnp.concatenate([z, cc, cc], axis=1),
        "ckb": jnp.concatenate([z, ss, ss], axis=1),
    }


def _mixers(proj, meta_proj, tiles, lam_p, sub_g, lam_init, sinks, real):
    mq, mk, mv, dq, dk, dv, sq, sk, sv = proj
    if real:
        m_mk, m_mv, m_dk, m_dv, m_sk, m_sv = (meta_proj[i] for i in (1, 2, 4, 5, 7, 8))
        ya = _mla_attention(mq, mk, mv, (m_mk, m_mv), MLA_TILE)
        yb = _diff_attention(dq, dk, dv, (m_dk, m_dv), tiles, lam_p, sub_g, lam_init, DIFF_TILE)
        yc = _swa_attention(sq, sk, sv, (m_sk, m_sv), tiles, sinks, SWA_QTILE)
    else:
        ya = _mla_attention(mq, mk, mv, None, BLK)
        yb = _diff_attention(dq, dk, dv, None, tiles, lam_p, sub_g, lam_init, BLK)
        yc = _swa_attention(sq, sk, sv, None, tiles, sinks, BLK)
    return ya, yb, yc


def kernel(x, meta_tokens, rel_bias, attn_norm, w_in, mla_q_norm, mla_w_qb, mla_kv_norm, mla_w_kvb, diff_lambda,
           diff_subln, swa_sinks, w_out, ffn_norm, w_gate, w_up, w_down, final_norm):
    bsz, seq, _ = x.shape
    assert seq % MLA_TILE == 0 and seq % DIFF_TILE == 0 and seq % SWA_QTILE == 0 and seq % ROW_TILE == 0

    h = x
    h_meta = jnp.concatenate([meta_tokens.astype(f32), jnp.zeros((BLK - N_META, D_MODEL), f32)], axis=0)[None]
    tabs = _rope_tables(N_META + jnp.arange(seq))
    tabs_meta = _rope_tables(jnp.minimum(jnp.arange(BLK), N_META - 1))
    tiles = dict(zip(("dd_diag", "dd_prev", "dd_meta", "dd_self", "ds_band", "ds_meta", "ds_self"),
                     _bias_tiles(rel_bias)))
    final_g = final_norm[None]

    for l in range(DEPTH):
        lw = _layer_weights(l, attn_norm, w_in, mla_q_norm, mla_w_qb, mla_kv_norm, mla_w_kvb, w_out, ffn_norm,
                            w_gate, w_up, w_down)
        lam_init = 0.8 - 0.6 * math.exp(-0.3 * l)
        sub_g = jnp.concatenate([diff_subln[l], diff_subln[l]])[None]
        last = l == DEPTH - 1

        meta_proj = _project(h_meta, lw, tabs_meta, BLK)
        proj = _project(h, lw, tabs, ROW_TILE)
        if not last:
            y_meta = _mixers(meta_proj, None, tiles, diff_lambda[l], sub_g, lam_init, swa_sinks[l], real=False)
            h_meta = _out_ffn(h_meta, *y_meta, lw, final_g, False, BLK)
        y = _mixers(proj, meta_proj, tiles, diff_lambda[l], sub_g, lam_init, swa_sinks[l], real=True)
        h = _out_ffn(h, *y, lw, final_g, last, ROW_TILE)
    return h
```

```python
import functools
import math

import numpy as np
import jax
import jax.numpy as jnp
from jax import lax
from jax.experimental import pallas as pl
from jax.experimental.pallas import tpu as pltpu

D_MODEL = 1024
DEPTH = 2
N_META = 16
BLK = 128

MLA_HEADS = 6
MLA_Q_RANK = 256
MLA_KV_RANK = 128
MLA_NOPE = 64
MLA_ROPE = 32
MLA_V = 64
ROPE_THETA = 10000.0

DIFF_HEADS = 4
DIFF_QK = 32
DIFF_V = 64

SWA_HEADS = 6
SWA_KV_HEADS = 2
SWA_REP = SWA_HEADS // SWA_KV_HEADS
SWA_HD = 64
WINDOW = 128

REL_BUCKETS = 32
REL_MAX_DIST = 128
D_FF = 2816
NEG_INF = -1e30
EPS = 1e-6
LOG2E = math.log2(math.e)

MLA_SCALE = (MLA_NOPE + MLA_ROPE) ** -0.5
DIFF_SCALE = DIFF_QK ** -0.5
SWA_SCALE = SWA_HD ** -0.5

LANES = 128
HEAD_V = 64
D_NAT = 1536
D_TR = 512
VMEM_LIMIT = 56 * 1024 * 1024

MLA_TILE = 512
DIFF_TILE = 256
SWA_QTILE = 512
ROW_TILE = 512

f32 = jnp.float32
bf16 = jnp.bfloat16


def _bucket_thresholds():
    n = np.arange(0, 4 * REL_MAX_DIST)
    max_exact = REL_BUCKETS // 2
    nf = np.maximum(n, max_exact).astype(np.float32)
    large = max_exact + (np.log(nf / np.float32(max_exact)) / np.float32(math.log(REL_MAX_DIST / max_exact))
                         * np.float32(REL_BUCKETS - max_exact)).astype(np.int32)
    bucket = np.where(n < max_exact, n, np.minimum(large, REL_BUCKETS - 1))
    assert np.all(np.diff(bucket) >= 0) and bucket[-1] == REL_BUCKETS - 1
    return [int(np.argmax(bucket >= k)) for k in range(1, REL_BUCKETS)]


BUCKET_THR = _bucket_thresholds()
FAR_DIST = BUCKET_THR[-1]
assert FAR_DIST <= BLK - N_META + 1


def _dot(a, b):
    return jnp.dot(a, b, preferred_element_type=f32)


def _dot_nt(a, b):
    return lax.dot_general(a, b, (((1,), (1,)), ((), ())), preferred_element_type=f32)


def _rms(x, g):
    return (x * lax.rsqrt(jnp.mean(x * x, axis=-1, keepdims=True) + EPS)) * g


def _lane(shape):
    return lax.broadcasted_iota(jnp.int32, shape, len(shape) - 1)


def _row(shape):
    return lax.broadcasted_iota(jnp.int32, shape, 0)


def _bias_lookup(n, tab_ref, h):
    acc = jnp.full(n.shape, tab_ref[0, h], f32)
    for k, thr in enumerate(BUCKET_THR, start=1):
        acc = jnp.where(n >= thr, tab_ref[k, h], acc)
    return acc


def _bias_kernel(tab_ref, dd_diag, dd_prev, dd_meta, dd_self, ds_band, ds_meta, ds_self, *, td):
    rc = 32

    def rows_cols(r0, cols):
        a = r0 + lax.broadcasted_iota(jnp.int32, (rc, cols), 0)
        b = lax.broadcasted_iota(jnp.int32, (rc, cols), 1)
        return a, b

    def diff_body(i, carry):
        r0 = pl.multiple_of(i * rc, rc)
        k, q = rows_cols(r0, td)
        for h in range(DIFF_HEADS):
            c = tab_ref[REL_BUCKETS - 1, h]
            dd_diag[h, pl.ds(r0, rc), :] = jnp.where(
                k <= q, (_bias_lookup(jnp.maximum(q - k, 0), tab_ref, h) - c) * LOG2E, NEG_INF)
            dd_prev[h, pl.ds(r0, rc), :] = (_bias_lookup(q + td - k, tab_ref, h) - c) * LOG2E
        return carry

    lax.fori_loop(0, td // rc, diff_body, 0)

    def blk_body(i, carry):
        r0 = pl.multiple_of(i * rc, rc)
        j, q = rows_cols(r0, td)
        j2, q2 = rows_cols(r0, BLK)
        self_ok = j2 <= jnp.minimum(q2, N_META - 1)
        for h in range(DIFF_HEADS):
            c = tab_ref[REL_BUCKETS - 1, h]
            dd_meta[0, h, pl.ds(r0, rc), :] = jnp.where(
                j < N_META, (_bias_lookup(N_META + q - j, tab_ref, h) - c) * LOG2E, NEG_INF)
            dd_meta[1, h, pl.ds(r0, rc), :] = jnp.where(j < N_META, 0.0, NEG_INF)
            dd_self[h, pl.ds(r0, rc), :] = jnp.where(
                self_ok, (_bias_lookup(jnp.maximum(q2 - j2, 0), tab_ref, h) - c) * LOG2E, NEG_INF)
        a, b = rows_cols(r0, BLK)
        swa_self_ok = b <= jnp.minimum(a, N_META - 1)
        for h in range(SWA_HEADS):
            hb = DIFF_HEADS + h
            near = _bias_lookup(jnp.maximum(a - b, 0), tab_ref, hb)
            cur = jnp.where(b <= a, near, NEG_INF)
            prev = jnp.where(b > a, _bias_lookup(a + BLK - b, tab_ref, hb), NEG_INF)
            ds_band[0, h, pl.ds(r0, rc), 0:BLK] = cur
            ds_band[0, h, pl.ds(r0, rc), BLK:2 * BLK] = jnp.full((rc, BLK), NEG_INF, f32)
            ds_band[1, h, pl.ds(r0, rc), 0:BLK] = prev
            ds_band[1, h, pl.ds(r0, rc), BLK:2 * BLK] = cur
            ds_meta[0, h, pl.ds(r0, rc), :] = jnp.where(
                b < N_META, _bias_lookup(N_META + a - b, tab_ref, hb), NEG_INF)
            ds_meta[1, h, pl.ds(r0, rc), :] = jnp.where(b < N_META, tab_ref[REL_BUCKETS - 1, hb], NEG_INF)
            ds_self[0, h, pl.ds(r0, rc), :] = jnp.where(swa_self_ok, near, NEG_INF)
        return carry

    lax.fori_loop(0, BLK // rc, blk_body, 0)


def _bias_tiles(rel_bias):
    td = DIFF_TILE
    outs = (
        jax.ShapeDtypeStruct((DIFF_HEADS, td, td), f32),
        jax.ShapeDtypeStruct((DIFF_HEADS, td, td), f32),
        jax.ShapeDtypeStruct((2, DIFF_HEADS, BLK, td), f32),
        jax.ShapeDtypeStruct((DIFF_HEADS, BLK, BLK), f32),
        jax.ShapeDtypeStruct((2, SWA_HEADS, BLK, 2 * BLK), f32),
        jax.ShapeDtypeStruct((2, SWA_HEADS, BLK, LANES), f32),
        jax.ShapeDtypeStruct((1, SWA_HEADS, BLK, LANES), f32),
    )
    return pl.pallas_call(
        functools.partial(_bias_kernel, td=td),
        name="bias_tiles",
        out_shape=outs,
        in_specs=[pl.BlockSpec(memory_space=pltpu.SMEM)],
        compiler_params=pltpu.CompilerParams(vmem_limit_bytes=VMEM_LIMIT),
    )(rel_bias)


def _store_vt(out_ref, vt, heads):
    tm = vt.shape[1]
    ones_blk = jnp.where(_row((HEAD_V, tm)) == 0, 1.0, 0.0).astype(bf16)
    for hd in range(heads):
        out_ref[0, hd * LANES:hd * LANES + HEAD_V, :] = vt[hd * HEAD_V:(hd + 1) * HEAD_V].astype(bf16)
        out_ref[0, hd * LANES + HEAD_V:(hd + 1) * LANES, :] = ones_blk


def _proj_kernel(h_ref, g_ref, wnat_ref, wtr_ref, qn_ref, wqbt_ref, kvn_ref, wkb_ref, wvt_ref,
                 csqt_ref, cka_ref, ckb_ref,
                 mqt_ref, mk_ref, mvt_ref, dqt_ref, dk_ref, dvt_ref, sq_ref, sk_ref, sv_ref):
    hn = _rms(h_ref[0], g_ref[...]).astype(bf16)
    proj = _dot(hn, wnat_ref[...])
    tr = _dot_nt(wtr_ref[...], hn)
    c_q = proj[:, 0:256]
    c_kv = proj[:, 256:384]
    rope_a = proj[:, 384:512]
    rope_b = proj[:, 512:640]
    dk_ref[0] = proj[:, 640:896].astype(bf16)
    sq_ref[0] = (proj[:, 896:1280] * SWA_SCALE).astype(bf16)
    sk_ref[0] = proj[:, 1280:1408].astype(bf16)
    sv_ref[0] = proj[:, 1408:1536].astype(bf16)
    dqt_ref[0] = (tr[0:256] * (DIFF_SCALE * LOG2E)).astype(bf16)
    _store_vt(dvt_ref, tr[256:512], DIFF_HEADS)

    cqn = _rms(c_q, qn_ref[...]).astype(bf16)
    ckvn = _rms(c_kv, kvn_ref[...]).astype(bf16)
    qt = _dot_nt(wqbt_ref[...], cqn)
    kk = _dot(ckvn, wkb_ref[...])
    vt = _dot_nt(wvt_ref[...], ckvn)
    csqt = csqt_ref[...]
    k_rot = rope_a * cka_ref[...] + rope_b * ckb_ref[...]
    for hd in range(MLA_HEADS):
        sl = slice(hd * LANES, (hd + 1) * LANES)
        mqt_ref[0, sl, :] = (qt[sl] * csqt).astype(bf16)
        mk_ref[0, :, sl] = (kk[:, sl] + k_rot).astype(bf16)
    _store_vt(mvt_ref, vt, MLA_HEADS)


def _const_spec(shape):
    nd = len(shape)
    return pl.BlockSpec(shape, lambda *_: (0,) * nd)


def _project(h, lw, tabs, tm):
    bsz, t, _ = h.shape
    outs = ((768, True), (768, False), (768, True), (256, True), (256, False), (512, True),
            (384, False), (128, False), (128, False))
    out_shape = tuple(jax.ShapeDtypeStruct((bsz, w, t) if tr else (bsz, t, w), bf16) for w, tr in outs)
    out_specs = tuple(pl.BlockSpec((1, w, tm), lambda b, i: (b, 0, i)) if tr
                      else pl.BlockSpec((1, tm, w), lambda b, i: (b, i, 0)) for w, tr in outs)
    tab_spec = pl.BlockSpec((tm, LANES), lambda b, i: (i, 0))
    in_specs = [
        pl.BlockSpec((1, tm, D_MODEL), lambda b, i: (b, i, 0)),
        _const_spec((1, D_MODEL)),
        _const_spec((D_MODEL, D_NAT)),
        _const_spec((D_TR, D_MODEL)),
        _const_spec((1, MLA_Q_RANK)),
        _const_spec((768, MLA_Q_RANK)),
        _const_spec((1, MLA_KV_RANK)),
        _const_spec((MLA_KV_RANK, 768)),
        _const_spec((384, MLA_KV_RANK)),
        pl.BlockSpec((LANES, tm), lambda b, i: (0, i)),
        tab_spec, tab_spec,
    ]
    return pl.pallas_call(
        _proj_kernel,
        name="proj_real" if bsz > 1 else "proj_meta",
        out_shape=out_shape,
        grid=(bsz, t // tm),
        in_specs=in_specs,
        out_specs=out_specs,
        compiler_params=pltpu.CompilerParams(
            dimension_semantics=("parallel", "parallel"), vmem_limit_bytes=VMEM_LIMIT),
    )(h, lw["attn_norm"], lw["w_nat"], lw["w_tr"], lw["q_norm"], lw["w_qbt"], lw["kv_norm"], lw["w_kb"],
      lw["w_vt"], tabs["csqt"], tabs["cka"], tabs["ckb"])


def _flash_update(s, vt, m_ref, acc_ref, idx):
    m_prev = m_ref[idx]
    m_new = jnp.maximum(m_prev, jnp.max(s, axis=0, keepdims=True))
    alpha = jnp.exp2(m_prev - m_new)
    p = jnp.exp2(s - m_new).astype(bf16)
    acc_ref[idx] = acc_ref[idx] * alpha + _dot(vt, p)
    m_ref[idx] = m_new


def _init_stats(m_ref, acc_ref):
    m_ref[...] = jnp.full(m_ref.shape, NEG_INF, f32)
    acc_ref[...] = jnp.zeros(acc_ref.shape, f32)


def _normalized(acc_ref, idx, cols):
    return acc_ref[idx, 0:HEAD_V, cols] / acc_ref[idx, HEAD_V:HEAD_V + 1, cols]


def _mla_kernel(*refs, t, has_meta):
    if has_meta:
        qt_ref, k_ref, vt_ref, mk_ref, mvt_ref, o_ref, m_ref, acc_ref = refs
    else:
        qt_ref, k_ref, vt_ref, o_ref, m_ref, acc_ref = refs
    qi = pl.program_id(2)
    _init_stats(m_ref, acc_ref)

    def head(hh):
        return slice(hh * LANES, (hh + 1) * LANES)

    if has_meta:
        valid = _row((BLK, t)) < N_META
        for hh in range(2):
            s = _dot(mk_ref[0, :, head(hh)], qt_ref[0, head(hh), :])
            _flash_update(jnp.where(valid, s, NEG_INF), mvt_ref[0, head(hh), :], m_ref, acc_ref, hh)

    def body(kj, carry):
        off = pl.multiple_of(kj * t, t)
        for hh in range(2):
            s = _dot(k_ref[0, pl.ds(off, t), head(hh)], qt_ref[0, head(hh), :])
            _flash_update(s, vt_ref[0, head(hh), pl.ds(off, t)], m_ref, acc_ref, hh)
        return carry

    lax.fori_loop(0, qi, body, 0)

    off = pl.multiple_of(qi * t, t)
    key = _row((t, t))
    qry = _lane((t, t))
    if has_meta:
        mask = key <= qry
    else:
        mask = key <= jnp.minimum(qry, N_META - 1)
    for hh in range(2):
        s = _dot(k_ref[0, pl.ds(off, t), head(hh)], qt_ref[0, head(hh), :])
        _flash_update(jnp.where(mask, s, NEG_INF), vt_ref[0, head(hh), pl.ds(off, t)], m_ref, acc_ref, hh)

    full = slice(0, t)
    ot = jnp.concatenate([_normalized(acc_ref, 0, full), _normalized(acc_ref, 1, full)], axis=0)
    o_ref[0] = ot.T.astype(bf16)


def _mla_attention(mqt, mk, mvt, meta_kv, t):
    bsz, tlen, _ = mk.shape
    has_meta = meta_kv is not None
    npair = MLA_HEADS // 2
    in_specs = [
        pl.BlockSpec((1, 2 * LANES, t), lambda b, p, i: (b, p, i)),
        pl.BlockSpec((1, tlen, 2 * LANES), lambda b, p, i: (b, 0, p)),
        pl.BlockSpec((1, 2 * LANES, tlen), lambda b, p, i: (b, p, 0)),
    ]
    args = [mqt, mk, mvt]
    if has_meta:
        in_specs += [
            pl.BlockSpec((1, BLK, 2 * LANES), lambda b, p, i: (0, 0, p)),
            pl.BlockSpec((1, 2 * LANES, BLK), lambda b, p, i: (0, p, 0)),
        ]
        args += list(meta_kv)
    return pl.pallas_call(
        functools.partial(_mla_kernel, t=t, has_meta=has_meta),
        name="mla_real" if has_meta else "mla_meta",
        out_shape=jax.ShapeDtypeStruct((bsz, tlen, npair * LANES), bf16),
        grid=(bsz, npair, tlen // t),
        in_specs=in_specs,
        out_specs=pl.BlockSpec((1, t, LANES), lambda b, p, i: (b, i, p)),
        scratch_shapes=[
            pltpu.VMEM((2, 1, t), f32),
            pltpu.VMEM((2, LANES, t), f32),
        ],
        compiler_params=pltpu.CompilerParams(
            dimension_semantics=("parallel", "parallel", "arbitrary"), vmem_limit_bytes=VMEM_LIMIT),
    )(*args)


def _diff_kernel(*refs, t, has_meta, lam_init):
    if has_meta:
        (qt_ref, k_ref, vt_ref, mk_ref, mvt_ref, ddiag_ref, dprev_ref, dmeta_ref, lam_ref, subg_ref,
         o_ref, qs_ref, m_ref, acc_ref) = refs
    else:
        (qt_ref, k_ref, vt_ref, ddiag_ref, lam_ref, subg_ref, o_ref, qs_ref, m_ref, acc_ref) = refs
    qi = pl.program_id(2)
    _init_stats(m_ref, acc_ref)

    qt = qt_ref[0]
    group = _row((LANES, t)) // DIFF_QK
    for hh in range(2):
        for c in range(2):
            qs_ref[hh, :, c * t:(c + 1) * t] = jnp.where(group == 2 * hh + c, qt, jnp.zeros_like(qt))

    def head(hh):
        return slice(hh * LANES, (hh + 1) * LANES)

    def step(k, vt_of, bias_of):
        for hh in range(2):
            s = _dot(k, qs_ref[hh])
            if bias_of is not None:
                b = bias_of(hh)
                s = s + jnp.concatenate([b, b], axis=1)
            _flash_update(s, vt_of(hh), m_ref, acc_ref, hh)

    if has_meta:
        step(mk_ref[0], lambda hh: mvt_ref[0, head(hh), :], lambda hh: dmeta_ref[0, hh])

        def body(kj, carry):
            off = pl.multiple_of(kj * t, t)
            step(k_ref[0, pl.ds(off, t), :], lambda hh: vt_ref[0, head(hh), pl.ds(off, t)], None)
            return carry

        lax.fori_loop(0, jnp.maximum(qi - 1, 0), body, 0)

        @pl.when(qi >= 1)
        def _():
            off = pl.multiple_of((qi - 1) * t, t)
            step(k_ref[0, pl.ds(off, t), :], lambda hh: vt_ref[0, head(hh), pl.ds(off, t)],
                 lambda hh: dprev_ref[hh])

    off = pl.multiple_of(qi * t, t)
    step(k_ref[0, pl.ds(off, t), :], lambda hh: vt_ref[0, head(hh), pl.ds(off, t)], lambda hh: ddiag_ref[hh])

    lp = lam_ref[...]
    lam = (jnp.exp(jnp.sum(lp[0:1] * lp[1:2], axis=-1, keepdims=True))
           - jnp.exp(jnp.sum(lp[2:3] * lp[3:4], axis=-1, keepdims=True)) + lam_init)
    outs = []
    for hh in range(2):
        o = _normalized(acc_ref, hh, slice(0, t)) - lam * _normalized(acc_ref, hh, slice(t, 2 * t))
        ms = jnp.mean(o * o, axis=0, keepdims=True)
        outs.append(o * lax.rsqrt(ms + EPS))
    o_nat = jnp.concatenate(outs, axis=0).T
    o_ref[0] = ((o_nat * subg_ref[...]) * (1.0 - lam_init)).astype(bf16)


def _diff_attention(dqt, dk, dvt, meta_kv, tiles, lam_p, sub_g, lam_init, t):
    bsz, tlen, _ = dk.shape
    has_meta = meta_kv is not None
    npair = DIFF_HEADS // 2
    in_specs = [
        pl.BlockSpec((1, LANES, t), lambda b, p, i: (b, p, i)),
        pl.BlockSpec((1, tlen, LANES), lambda b, p, i: (b, 0, p)),
        pl.BlockSpec((1, 2 * LANES, tlen), lambda b, p, i: (b, p, 0)),
    ]
    args = [dqt, dk, dvt]
    if has_meta:
        in_specs += [
            pl.BlockSpec((1, BLK, LANES), lambda b, p, i: (0, 0, p)),
            pl.BlockSpec((1, 2 * LANES, BLK), lambda b, p, i: (0, p, 0)),
            pl.BlockSpec((2, t, t), lambda b, p, i: (p, 0, 0)),
            pl.BlockSpec((2, t, t), lambda b, p, i: (p, 0, 0)),
            pl.BlockSpec((1, 2, BLK, t), lambda b, p, i: (jnp.minimum(i, 1), p, 0, 0)),
        ]
        args += list(meta_kv) + [tiles["dd_diag"], tiles["dd_prev"], tiles["dd_meta"]]
    else:
        in_specs += [pl.BlockSpec((2, t, t), lambda b, p, i: (p, 0, 0))]
        args += [tiles["dd_self"]]
    in_specs += [_const_spec((4, DIFF_QK)), _const_spec((1, LANES))]
    args += [lam_p, sub_g]
    return pl.pallas_call(
        functools.partial(_diff_kernel, t=t, has_meta=has_meta, lam_init=lam_init),
        name="diff_real" if has_meta else "diff_meta",
        out_shape=jax.ShapeDtypeStruct((bsz, tlen, npair * LANES), bf16),
        grid=(bsz, npair, tlen // t),
        in_specs=in_specs,
        out_specs=pl.BlockSpec((1, t, LANES), lambda b, p, i: (b, i, p)),
        scratch_shapes=[
            pltpu.VMEM((2, LANES, 2 * t), bf16),
            pltpu.VMEM((2, 1, 2 * t), f32),
            pltpu.VMEM((2, LANES, 2 * t), f32),
        ],
        compiler_params=pltpu.CompilerParams(
            dimension_semantics=("parallel", "parallel", "arbitrary"), vmem_limit_bytes=VMEM_LIMIT),
    )(*args)


def _swa_kernel(*refs, tq, has_band):
    if has_band:
        q_ref, k_ref, v_ref, mk_ref, mv_ref, dband_ref, dmeta_ref, sink_ref, o_ref = refs
    else:
        q_ref, mk_ref, mv_ref, dmeta_ref, sink_ref, o_ref = refs
    qi = pl.program_id(1)
    lane = _lane((BLK, LANES))
    row2 = lax.broadcasted_iota(jnp.int32, (2 * BLK, 1), 0)
    n_meta_var = dmeta_ref.shape[0]

    def block(sub, carry):
        blk = qi * (tq // BLK) + sub
        r0 = pl.multiple_of(sub * BLK, BLK)
        mvar = jnp.minimum(blk, n_meta_var - 1)
        if has_band:
            bvar = jnp.minimum(blk, 1)
            koff = pl.multiple_of(jnp.maximum(blk - 1, 0) * BLK, BLK)
            kband = k_ref[0, pl.ds(koff, 2 * BLK), :]
            vband = v_ref[0, pl.ds(koff, 2 * BLK), :]
        for r in range(SWA_REP):
            q2 = q_ref[0, pl.ds(r0, BLK), r * LANES:(r + 1) * LANES]
            zero = jnp.zeros_like(q2)
            qs = jnp.concatenate([jnp.where(lane < SWA_HD, q2, zero), jnp.where(lane >= SWA_HD, q2, zero)],
                                 axis=0)
            h0, h1 = r, SWA_REP + r
            sink = jnp.where(row2 < BLK, sink_ref[h0], sink_ref[h1])
            s_meta = _dot_nt(qs, mk_ref[0]) + jnp.concatenate(
                [dmeta_ref[mvar, h0], dmeta_ref[mvar, h1]], axis=0)
            m = jnp.maximum(jnp.max(s_meta, axis=-1, keepdims=True), sink)
            if has_band:
                s_band = _dot_nt(qs, kband) + jnp.concatenate(
                    [dband_ref[bvar, h0], dband_ref[bvar, h1]], axis=0)
                m = jnp.maximum(m, jnp.max(s_band, axis=-1, keepdims=True))
            p_meta = jnp.exp(s_meta - m)
            denom = jnp.sum(p_meta, axis=-1, keepdims=True) + jnp.exp(sink - m)
            acc = _dot(p_meta.astype(bf16), mv_ref[0])
            if has_band:
                p_band = jnp.exp(s_band - m)
                denom = denom + jnp.sum(p_band, axis=-1, keepdims=True)
                acc = acc + _dot(p_band.astype(bf16), vband)
            o = acc / denom
            o_ref[0, pl.ds(r0, BLK), r * LANES:(r + 1) * LANES] = jnp.where(
                lane < SWA_HD, o[0:BLK], o[BLK:2 * BLK]).astype(bf16)
        return carry

    lax.fori_loop(0, tq // BLK, block, 0)


def _swa_attention(sq, sk, sv, meta_kv, tiles, sinks, tq):
    bsz, tlen, _ = sq.shape
    has_band = meta_kv is not None
    width = SWA_REP * LANES
    smem = pl.BlockSpec(memory_space=pltpu.SMEM)
    if has_band:
        in_specs = [
            pl.BlockSpec((1, tq, width), lambda b, i: (b, i, 0)),
            pl.BlockSpec((1, tlen, LANES), lambda b, i: (b, 0, 0)),
            pl.BlockSpec((1, tlen, LANES), lambda b, i: (b, 0, 0)),
            _const_spec((1, BLK, LANES)),
            _const_spec((1, BLK, LANES)),
            _const_spec(tiles["ds_band"].shape),
            _const_spec(tiles["ds_meta"].shape),
            smem,
        ]
        args = [sq, sk, sv, meta_kv[0], meta_kv[1], tiles["ds_band"], tiles["ds_meta"], sinks]
    else:
        in_specs = [
            pl.BlockSpec((1, tq, width), lambda b, i: (b, i, 0)),
            _const_spec((1, BLK, LANES)),
            _const_spec((1, BLK, LANES)),
            _const_spec(tiles["ds_self"].shape),
            smem,
        ]
        args = [sq, sk, sv, tiles["ds_self"], sinks]
    return pl.pallas_call(
        functools.partial(_swa_kernel, tq=tq, has_band=has_band),
        name="swa_real" if has_band else "swa_meta",
        out_shape=jax.ShapeDtypeStruct((bsz, tlen, width), bf16),
        grid=(bsz, tlen // tq),
        in_specs=in_specs,
        out_specs=pl.BlockSpec((1, tq, width), lambda b, i: (b, i, 0)),
        compiler_params=pltpu.CompilerParams(
            dimension_semantics=("parallel", "parallel"), vmem_limit_bytes=VMEM_LIMIT),
    )(*args)


def _out_ffn_kernel(h_ref, ya_ref, yb_ref, yc_ref, woa_ref, wob_ref, woc_ref, g_ref, wg_ref, wu_ref, wd_ref,
                    fg_ref, o_ref, *, final):
    h1 = (h_ref[0] + _dot(ya_ref[0], woa_ref[...]) + _dot(yb_ref[0], wob_ref[...])
          + _dot(yc_ref[0], woc_ref[...]))
    hn = _rms(h1, g_ref[...]).astype(bf16)
    gate = _dot(hn, wg_ref[...])
    up = _dot(hn, wu_ref[...])
    act = ((gate * jax.nn.sigmoid(gate)) * up).astype(bf16)
    h2 = h1 + _dot(act, wd_ref[...])
    if final:
        h2 = _rms(h2, fg_ref[...])
    o_ref[0] = h2


def _single(shape):
    nd = len(shape)
    return pl.BlockSpec(shape, lambda *_: (0,) * nd, pipeline_mode=pl.Buffered(1))


def _out_ffn(h, ya, yb, yc, lw, final_g, final, tm):
    bsz, t, _ = h.shape

    def row(w):
        return pl.BlockSpec((1, tm, w), lambda b, i: (b, i, 0))

    in_specs = [
        row(D_MODEL), row(384), row(256), row(384),
        _single((384, D_MODEL)), _single((256, D_MODEL)), _single((384, D_MODEL)),
        _single((1, D_MODEL)),
        _single((D_MODEL, D_FF)), _single((D_MODEL, D_FF)), _single((D_FF, D_MODEL)),
        _single((1, D_MODEL)),
    ]
    return pl.pallas_call(
        functools.partial(_out_ffn_kernel, final=final),
        name="out_ffn_real" if bsz > 1 else "out_ffn_meta",
        out_shape=jax.ShapeDtypeStruct((bsz, t, D_MODEL), f32),
        grid=(bsz, t // tm),
        in_specs=in_specs,
        out_specs=row(D_MODEL),
        compiler_params=pltpu.CompilerParams(
            dimension_semantics=("parallel", "parallel"), vmem_limit_bytes=VMEM_LIMIT),
    )(h, ya, yb, yc, lw["wo_a"], lw["wo_b"], lw["wo_c"], lw["ffn_norm"], lw["w_gate"], lw["w_up"],
      lw["w_down"], final_g)


def _rot_half_cols(w):
    half = MLA_ROPE // 2
    return jnp.concatenate([-w[:, half:], w[:, :half]], axis=1)


def _layer_weights(l, attn_norm, w_in, mla_q_norm, mla_w_qb, mla_kv_norm, mla_w_kvb, w_out, ffn_norm,
                   w_gate, w_up, w_down):
    w = w_in[l]
    o = 0
    parts = {}
    for name, width in (("c_q", 256), ("c_kv", 128), ("k_rope", 32), ("dq", 256), ("dk", 256), ("dv", 256),
                        ("sq", 384), ("sk", 128), ("sv", 128)):
        parts[name] = w[:, o:o + width]
        o += width
    kr = parts["k_rope"]
    ksw = _rot_half_cols(kr)
    z64 = jnp.zeros((D_MODEL, 64), f32)
    swa_order = [g * SWA_REP + r for r in range(SWA_REP) for g in range(SWA_KV_HEADS)]
    sq = jnp.concatenate([parts["sq"][:, h * SWA_HD:(h + 1) * SWA_HD] for h in swa_order], axis=1)
    w_nat = jnp.concatenate([parts["c_q"], parts["c_kv"], z64, kr, kr, z64, ksw, ksw, parts["dk"], sq,
                             parts["sk"], parts["sv"]], axis=1)
    assert w_nat.shape == (D_MODEL, D_NAT)
    w_tr = jnp.concatenate([parts["dq"], parts["dv"]], axis=1).T
    assert w_tr.shape == (D_TR, D_MODEL)

    wq = mla_w_qb[l]
    q_cols = []
    for h in range(MLA_HEADS):
        base = h * (MLA_NOPE + MLA_ROPE)
        rope = wq[:, base + MLA_NOPE:base + MLA_NOPE + MLA_ROPE]
        q_cols += [wq[:, base:base + MLA_NOPE], rope, _rot_half_cols(rope)]
    w_qbt = jnp.concatenate(q_cols, axis=1).T

    wkv = mla_w_kvb[l]
    zk = jnp.zeros((MLA_KV_RANK, LANES - MLA_NOPE), f32)
    k_cols, v_cols = [], []
    for h in range(MLA_HEADS):
        base = h * (MLA_NOPE + MLA_V)
        k_cols += [wkv[:, base:base + MLA_NOPE], zk]
        v_cols.append(wkv[:, base + MLA_NOPE:base + MLA_NOPE + MLA_V])
    w_kb = jnp.concatenate(k_cols, axis=1)
    w_vt = jnp.concatenate(v_cols, axis=1).T

    wo = w_out[l]
    na = MLA_HEADS * MLA_V
    nb = DIFF_HEADS * DIFF_V
    wo_c = wo[na + nb:]
    wo_c = jnp.concatenate([wo_c[h * SWA_HD:(h + 1) * SWA_HD] for h in swa_order], axis=0)
    return {
        "attn_norm": attn_norm[l][None], "w_nat": w_nat.astype(bf16), "w_tr": w_tr.astype(bf16),
        "q_norm": mla_q_norm[l][None], "w_qbt": w_qbt.astype(bf16),
        "kv_norm": mla_kv_norm[l][None], "w_kb": w_kb.astype(bf16), "w_vt": w_vt.astype(bf16),
        "wo_a": wo[:na].astype(bf16), "wo_b": wo[na:na + nb].astype(bf16), "wo_c": wo_c.astype(bf16),
        "ffn_norm": ffn_norm[l][None],
        "w_gate": w_gate[l].astype(bf16), "w_up": w_up[l].astype(bf16), "w_down": w_down[l].astype(bf16),
    }


def _rope_tables(pos):
    inv_freq = ROPE_THETA ** (-jnp.arange(0, MLA_ROPE, 2, dtype=f32) / MLA_ROPE)
    ang = pos.astype(f32)[:, None] * inv_freq[None, :]
    cos, sin = jnp.cos(ang), jnp.sin(ang)
    n = pos.shape[0]
    cc = jnp.concatenate([cos, cos], axis=1)
    ss = jnp.concatenate([sin, sin], axis=1)
    z = jnp.zeros((n, 64), f32)
    return {
        "csqt": (jnp.concatenate([jnp.ones((n, 64), f32), cc, ss], axis=1) * (MLA_SCALE * LOG2E)).T,
        "cka": jnp.concatenate([z, cc, cc], axis=1),
        "ckb": jnp.concatenate([z, ss, ss], axis=1),
    }


def _mixers(proj, meta_proj, tiles, lam_p, sub_g, lam_init, sinks, real):
    mqt, mk, mvt, dqt, dk, dvt, sq, sk, sv = proj
    if real:
        m_mk, m_mvt, m_dk, m_dvt, m_sk, m_sv = (meta_proj[i] for i in (1, 2, 4, 5, 7, 8))
        ya = _mla_attention(mqt, mk, mvt, (m_mk, m_mvt), MLA_TILE)
        yb = _diff_attention(dqt, dk, dvt, (m_dk, m_dvt), tiles, lam_p, sub_g, lam_init, DIFF_TILE)
        yc = _swa_attention(sq, sk, sv, (m_sk, m_sv), tiles, sinks, SWA_QTILE)
    else:
        ya = _mla_attention(mqt, mk, mvt, None, BLK)
        yb = _diff_attention(dqt, dk, dvt, None, tiles, lam_p, sub_g, lam_init, BLK)
        yc = _swa_attention(sq, sk, sv, None, tiles, sinks, BLK)
    return ya, yb, yc


def kernel(x, meta_tokens, rel_bias, attn_norm, w_in, mla_q_norm, mla_w_qb, mla_kv_norm, mla_w_kvb, diff_lambda,
           diff_subln, swa_sinks, w_out, ffn_norm, w_gate, w_up, w_down, final_norm):
    bsz, seq, _ = x.shape
    assert seq % MLA_TILE == 0 and seq % DIFF_TILE == 0 and seq % SWA_QTILE == 0 and seq % ROW_TILE == 0

    h = x
    h_meta = jnp.concatenate([meta_tokens.astype(f32), jnp.zeros((BLK - N_META, D_MODEL), f32)], axis=0)[None]
    tabs = _rope_tables(N_META + jnp.arange(seq))
    tabs_meta = _rope_tables(jnp.minimum(jnp.arange(BLK), N_META - 1))
    tiles = dict(zip(("dd_diag", "dd_prev", "dd_meta", "dd_self", "ds_band", "ds_meta", "ds_self"),
                     _bias_tiles(rel_bias)))
    final_g = final_norm[None]

    for l in range(DEPTH):
        lw = _layer_weights(l, attn_norm, w_in, mla_q_norm, mla_w_qb, mla_kv_norm, mla_w_kvb, w_out, ffn_norm,
                            w_gate, w_up, w_down)
        lam_init = 0.8 - 0.6 * math.exp(-0.3 * l)
        sub_g = jnp.concatenate([diff_subln[l], diff_subln[l]])[None]
        last = l == DEPTH - 1

        meta_proj = _project(h_meta, lw, tabs_meta, BLK)
        proj = _project(h, lw, tabs, ROW_TILE)
        if not last:
            y_meta = _mixers(meta_proj, None, tiles, diff_lambda[l], sub_g, lam_init, swa_sinks[l], real=False)
            h_meta = _out_ffn(h_meta, *y_meta, lw, final_g, False, BLK)
        y = _mixers(proj, meta_proj, tiles, diff_lambda[l], sub_g, lam_init, swa_sinks[l], real=True)
        h = _out_ffn(h, *y, lw, final_g, last, ROW_TILE)
    return h
```

```python
import functools
import math

import numpy as np
import jax
import jax.numpy as jnp
from jax import lax
from jax.experimental import pallas as pl
from jax.experimental.pallas import tpu as pltpu

D_MODEL = 1024
DEPTH = 2
N_META = 16
BLK = 128

MLA_HEADS = 6
MLA_Q_RANK = 256
MLA_KV_RANK = 128
MLA_NOPE = 64
MLA_ROPE = 32
MLA_V = 64
ROPE_THETA = 10000.0

DIFF_HEADS = 4
DIFF_QK = 32
DIFF_V = 64

SWA_HEADS = 6
SWA_KV_HEADS = 2
SWA_REP = SWA_HEADS // SWA_KV_HEADS
SWA_HD = 64
WINDOW = 128

REL_BUCKETS = 32
REL_MAX_DIST = 128
D_FF = 2816
NEG_INF = -1e30
EPS = 1e-6
LOG2E = math.log2(math.e)

MLA_SCALE = (MLA_NOPE + MLA_ROPE) ** -0.5
DIFF_SCALE = DIFF_QK ** -0.5
SWA_SCALE = SWA_HD ** -0.5

LANES = 128
HEAD_V = 64
D_NAT = 1536
D_TR = 512
VMEM_LIMIT = 56 * 1024 * 1024

MLA_TILE = 512
DIFF_TILE = 256
SWA_QTILE = 512
ROW_TILE = 512

f32 = jnp.float32
bf16 = jnp.bfloat16


def _bucket_thresholds():
    n = np.arange(0, 4 * REL_MAX_DIST)
    max_exact = REL_BUCKETS // 2
    nf = np.maximum(n, max_exact).astype(np.float32)
    large = max_exact + (np.log(nf / np.float32(max_exact)) / np.float32(math.log(REL_MAX_DIST / max_exact))
                         * np.float32(REL_BUCKETS - max_exact)).astype(np.int32)
    bucket = np.where(n < max_exact, n, np.minimum(large, REL_BUCKETS - 1))
    assert np.all(np.diff(bucket) >= 0) and bucket[-1] == REL_BUCKETS - 1
    return [int(np.argmax(bucket >= k)) for k in range(1, REL_BUCKETS)]


BUCKET_THR = _bucket_thresholds()
FAR_DIST = BUCKET_THR[-1]
assert FAR_DIST <= BLK - N_META + 1


def _dot(a, b):
    return jnp.dot(a, b, preferred_element_type=f32)


def _dot_nt(a, b):
    return lax.dot_general(a, b, (((1,), (1,)), ((), ())), preferred_element_type=f32)


def _rms(x, g):
    return (x * lax.rsqrt(jnp.mean(x * x, axis=-1, keepdims=True) + EPS)) * g


def _lane(shape):
    return lax.broadcasted_iota(jnp.int32, shape, len(shape) - 1)


def _row(shape):
    return lax.broadcasted_iota(jnp.int32, shape, 0)


def _bias_lookup(n, tab_ref, h):
    acc = jnp.full(n.shape, tab_ref[0, h], f32)
    for k, thr in enumerate(BUCKET_THR, start=1):
        acc = jnp.where(n >= thr, tab_ref[k, h], acc)
    return acc


def _bias_kernel(tab_ref, dd_diag, dd_prev, dd_meta, dd_self, ds_band, ds_meta, ds_self, *, td):
    rc = 32

    def rows_cols(r0, cols):
        a = r0 + lax.broadcasted_iota(jnp.int32, (rc, cols), 0)
        b = lax.broadcasted_iota(jnp.int32, (rc, cols), 1)
        return a, b

    def diff_body(i, carry):
        r0 = pl.multiple_of(i * rc, rc)
        k, q = rows_cols(r0, td)
        for h in range(DIFF_HEADS):
            c = tab_ref[REL_BUCKETS - 1, h]
            dd_diag[h, pl.ds(r0, rc), :] = jnp.where(
                k <= q, (_bias_lookup(jnp.maximum(q - k, 0), tab_ref, h) - c) * LOG2E, NEG_INF)
            dd_prev[h, pl.ds(r0, rc), :] = (_bias_lookup(q + td - k, tab_ref, h) - c) * LOG2E
        return carry

    lax.fori_loop(0, td // rc, diff_body, 0)

    def blk_body(i, carry):
        r0 = pl.multiple_of(i * rc, rc)
        j, q = rows_cols(r0, td)
        j2, q2 = rows_cols(r0, BLK)
        self_ok = j2 <= jnp.minimum(q2, N_META - 1)
        for h in range(DIFF_HEADS):
            c = tab_ref[REL_BUCKETS - 1, h]
            dd_meta[0, h, pl.ds(r0, rc), :] = jnp.where(
                j < N_META, (_bias_lookup(N_META + q - j, tab_ref, h) - c) * LOG2E, NEG_INF)
            dd_meta[1, h, pl.ds(r0, rc), :] = jnp.where(j < N_META, 0.0, NEG_INF)
            dd_self[h, pl.ds(r0, rc), :] = jnp.where(
                self_ok, (_bias_lookup(jnp.maximum(q2 - j2, 0), tab_ref, h) - c) * LOG2E, NEG_INF)
        a, b = rows_cols(r0, BLK)
        swa_self_ok = b <= jnp.minimum(a, N_META - 1)
        for h in range(SWA_HEADS):
            hb = DIFF_HEADS + h
            near = _bias_lookup(jnp.maximum(a - b, 0), tab_ref, hb)
            cur = jnp.where(b <= a, near, NEG_INF)
            prev = jnp.where(b > a, _bias_lookup(a + BLK - b, tab_ref, hb), NEG_INF)
            ds_band[0, h, pl.ds(r0, rc), 0:BLK] = cur
            ds_band[0, h, pl.ds(r0, rc), BLK:2 * BLK] = jnp.full((rc, BLK), NEG_INF, f32)
            ds_band[1, h, pl.ds(r0, rc), 0:BLK] = prev
            ds_band[1, h, pl.ds(r0, rc), BLK:2 * BLK] = cur
            ds_meta[0, h, pl.ds(r0, rc), :] = jnp.where(
                b < N_META, _bias_lookup(N_META + a - b, tab_ref, hb), NEG_INF)
            ds_meta[1, h, pl.ds(r0, rc), :] = jnp.where(b < N_META, tab_ref[REL_BUCKETS - 1, hb], NEG_INF)
            ds_self[0, h, pl.ds(r0, rc), :] = jnp.where(swa_self_ok, near, NEG_INF)
        return carry

    lax.fori_loop(0, BLK // rc, blk_body, 0)


def _bias_tiles(rel_bias):
    td = DIFF_TILE
    outs = (
        jax.ShapeDtypeStruct((DIFF_HEADS, td, td), f32),
        jax.ShapeDtypeStruct((DIFF_HEADS, td, td), f32),
        jax.ShapeDtypeStruct((2, DIFF_HEADS, BLK, td), f32),
        jax.ShapeDtypeStruct((DIFF_HEADS, BLK, BLK), f32),
        jax.ShapeDtypeStruct((2, SWA_HEADS, BLK, 2 * BLK), f32),
        jax.ShapeDtypeStruct((2, SWA_HEADS, BLK, LANES), f32),
        jax.ShapeDtypeStruct((1, SWA_HEADS, BLK, LANES), f32),
    )
    return pl.pallas_call(
        functools.partial(_bias_kernel, td=td),
        name="bias_tiles",
        out_shape=outs,
        in_specs=[pl.BlockSpec(memory_space=pltpu.SMEM)],
        compiler_params=pltpu.CompilerParams(vmem_limit_bytes=VMEM_LIMIT),
    )(rel_bias)


def _store_vt(out_ref, vt, heads):
    tm = vt.shape[1]
    ones_blk = jnp.where(_row((HEAD_V, tm)) == 0, 1.0, 0.0).astype(bf16)
    for hd in range(heads):
        out_ref[0, hd * LANES:hd * LANES + HEAD_V, :] = vt[hd * HEAD_V:(hd + 1) * HEAD_V].astype(bf16)
        out_ref[0, hd * LANES + HEAD_V:(hd + 1) * LANES, :] = ones_blk


def _proj_kernel(h_ref, g_ref, wnat_ref, wtr_ref, qn_ref, wqbt_ref, kvn_ref, wkb_ref, wvt_ref,
                 csqt_ref, cka_ref, ckb_ref,
                 mqt_ref, mk_ref, mvt_ref, dqt_ref, dk_ref, dvt_ref, sq_ref, sk_ref, sv_ref):
    hn = _rms(h_ref[0], g_ref[...]).astype(bf16)
    proj = _dot(hn, wnat_ref[...])
    tr = _dot_nt(wtr_ref[...], hn)
    c_q = proj[:, 0:256]
    c_kv = proj[:, 256:384]
    rope_a = proj[:, 384:512]
    rope_b = proj[:, 512:640]
    dk_ref[0] = proj[:, 640:896].astype(bf16)
    sq_ref[0] = (proj[:, 896:1280] * SWA_SCALE).astype(bf16)
    sk_ref[0] = proj[:, 1280:1408].astype(bf16)
    sv_ref[0] = proj[:, 1408:1536].astype(bf16)
    dqt_ref[0] = (tr[0:256] * (DIFF_SCALE * LOG2E)).astype(bf16)
    _store_vt(dvt_ref, tr[256:512], DIFF_HEADS)

    cqn = _rms(c_q, qn_ref[...]).astype(bf16)
    ckvn = _rms(c_kv, kvn_ref[...]).astype(bf16)
    qt = _dot_nt(wqbt_ref[...], cqn)
    kk = _dot(ckvn, wkb_ref[...])
    vt = _dot_nt(wvt_ref[...], ckvn)
    csqt = csqt_ref[...]
    k_rot = rope_a * cka_ref[...] + rope_b * ckb_ref[...]
    for hd in range(MLA_HEADS):
        sl = slice(hd * LANES, (hd + 1) * LANES)
        mqt_ref[0, sl, :] = (qt[sl] * csqt).astype(bf16)
        mk_ref[0, :, sl] = (kk[:, sl] + k_rot).astype(bf16)
    _store_vt(mvt_ref, vt, MLA_HEADS)


def _const_spec(shape):
    nd = len(shape)
    return pl.BlockSpec(shape, lambda *_: (0,) * nd)


def _project(h, lw, tabs, tm):
    bsz, t, _ = h.shape
    outs = ((768, True), (768, False), (768, True), (256, True), (256, False), (512, True),
            (384, False), (128, False), (128, False))
    out_shape = tuple(jax.ShapeDtypeStruct((bsz, w, t) if tr else (bsz, t, w), bf16) for w, tr in outs)
    out_specs = tuple(pl.BlockSpec((1, w, tm), lambda b, i: (b, 0, i)) if tr
                      else pl.BlockSpec((1, tm, w), lambda b, i: (b, i, 0)) for w, tr in outs)
    tab_spec = pl.BlockSpec((tm, LANES), lambda b, i: (i, 0))
    in_specs = [
        pl.BlockSpec((1, tm, D_MODEL), lambda b, i: (b, i, 0)),
        _const_spec((1, D_MODEL)),
        _const_spec((D_MODEL, D_NAT)),
        _const_spec((D_TR, D_MODEL)),
        _const_spec((1, MLA_Q_RANK)),
        _const_spec((768, MLA_Q_RANK)),
        _const_spec((1, MLA_KV_RANK)),
        _const_spec((MLA_KV_RANK, 768)),
        _const_spec((384, MLA_KV_RANK)),
        pl.BlockSpec((LANES, tm), lambda b, i: (0, i)),
        tab_spec, tab_spec,
    ]
    return pl.pallas_call(
        _proj_kernel,
        name="proj_real" if bsz > 1 else "proj_meta",
        out_shape=out_shape,
        grid=(bsz, t // tm),
        in_specs=in_specs,
        out_specs=out_specs,
        compiler_params=pltpu.CompilerParams(
            dimension_semantics=("parallel", "parallel"), vmem_limit_bytes=VMEM_LIMIT),
    )(h, lw["attn_norm"], lw["w_nat"], lw["w_tr"], lw["q_norm"], lw["w_qbt"], lw["kv_norm"], lw["w_kb"],
      lw["w_vt"], tabs["csqt"], tabs["cka"], tabs["ckb"])


def _flash_update(s, vt, m_ref, acc_ref, idx):
    m_prev = m_ref[idx]
    m_new = jnp.maximum(m_prev, jnp.max(s, axis=0, keepdims=True))
    alpha = jnp.exp2(m_prev - m_new)
    p = jnp.exp2(s - m_new).astype(bf16)
    acc_ref[idx] = acc_ref[idx] * alpha + _dot(vt, p)
    m_ref[idx] = m_new


def _init_stats(m_ref, acc_ref):
    m_ref[...] = jnp.full(m_ref.shape, NEG_INF, f32)
    acc_ref[...] = jnp.zeros(acc_ref.shape, f32)


def _normalized(acc_ref, idx, cols):
    return acc_ref[idx, 0:HEAD_V, cols] / acc_ref[idx, HEAD_V:HEAD_V + 1, cols]


def _mla_kernel(*refs, t, has_meta):
    if has_meta:
        qt_ref, k_ref, vt_ref, mk_ref, mvt_ref, o_ref, m_ref, acc_ref, s_ref = refs
    else:
        qt_ref, k_ref, vt_ref, o_ref, m_ref, acc_ref, s_ref = refs
    qi = pl.program_id(2)
    _init_stats(m_ref, acc_ref)

    def head(hh):
        return slice(hh * LANES, (hh + 1) * LANES)

    if has_meta:
        valid = _row((BLK, t)) < N_META
        for hh in range(2):
            s = _dot(mk_ref[0, :, head(hh)], qt_ref[0, head(hh), :])
            _flash_update(jnp.where(valid, s, NEG_INF), mvt_ref[0, head(hh), :], m_ref, acc_ref, hh)

    def stage_a(j, hh):
        s_ref[hh] = _dot(k_ref[0, pl.ds(pl.multiple_of(j * t, t), t), head(hh)], qt_ref[0, head(hh), :])

    def stage_b(j, hh, mask):
        s = s_ref[hh]
        if mask is not None:
            s = jnp.where(mask, s, NEG_INF)
        _flash_update(s, vt_ref[0, head(hh), pl.ds(pl.multiple_of(j * t, t), t)], m_ref, acc_ref, hh)

    def tile_step(j, mask, prefetch):
        stage_a(j, 1)
        stage_b(j, 0, mask)
        if prefetch:
            stage_a(j + 1, 0)
        stage_b(j, 1, mask)

    stage_a(0, 0)

    def body(j, carry):
        tile_step(j, None, True)
        return carry

    lax.fori_loop(0, qi, body, 0)

    key = _row((t, t))
    qry = _lane((t, t))
    if has_meta:
        mask = key <= qry
    else:
        mask = key <= jnp.minimum(qry, N_META - 1)
    tile_step(qi, mask, False)

    full = slice(0, t)
    ot = jnp.concatenate([_normalized(acc_ref, 0, full), _normalized(acc_ref, 1, full)], axis=0)
    o_ref[0] = ot.T.astype(bf16)


def _mla_attention(mqt, mk, mvt, meta_kv, t):
    bsz, tlen, _ = mk.shape
    has_meta = meta_kv is not None
    npair = MLA_HEADS // 2
    in_specs = [
        pl.BlockSpec((1, 2 * LANES, t), lambda b, p, i: (b, p, i)),
        pl.BlockSpec((1, tlen, 2 * LANES), lambda b, p, i: (b, 0, p)),
        pl.BlockSpec((1, 2 * LANES, tlen), lambda b, p, i: (b, p, 0)),
    ]
    args = [mqt, mk, mvt]
    if has_meta:
        in_specs += [
            pl.BlockSpec((1, BLK, 2 * LANES), lambda b, p, i: (0, 0, p)),
            pl.BlockSpec((1, 2 * LANES, BLK), lambda b, p, i: (0, p, 0)),
        ]
        args += list(meta_kv)
    return pl.pallas_call(
        functools.partial(_mla_kernel, t=t, has_meta=has_meta),
        name="mla_real" if has_meta else "mla_meta",
        out_shape=jax.ShapeDtypeStruct((bsz, tlen, npair * LANES), bf16),
        grid=(bsz, npair, tlen // t),
        in_specs=in_specs,
        out_specs=pl.BlockSpec((1, t, LANES), lambda b, p, i: (b, i, p)),
        scratch_shapes=[
            pltpu.VMEM((2, 1, t), f32),
            pltpu.VMEM((2, LANES, t), f32),
            pltpu.VMEM((2, t, t), f32),
        ],
        compiler_params=pltpu.CompilerParams(
            dimension_semantics=("parallel", "parallel", "arbitrary"), vmem_limit_bytes=VMEM_LIMIT),
    )(*args)


def _diff_kernel(*refs, t, has_meta, lam_init):
    if has_meta:
        (qt_ref, k_ref, vt_ref, mk_ref, mvt_ref, ddiag_ref, dprev_ref, dmeta_ref, lam_ref, subg_ref,
         o_ref, qs_ref, m_ref, acc_ref, s_ref) = refs
    else:
        (qt_ref, k_ref, vt_ref, ddiag_ref, lam_ref, subg_ref, o_ref, qs_ref, m_ref, acc_ref, s_ref) = refs
    qi = pl.program_id(2)
    _init_stats(m_ref, acc_ref)

    qt = qt_ref[0]
    group = _row((LANES, t)) // DIFF_QK
    for hh in range(2):
        for c in range(2):
            qs_ref[hh, :, c * t:(c + 1) * t] = jnp.where(group == 2 * hh + c, qt, jnp.zeros_like(qt))

    def head(hh):
        return slice(hh * LANES, (hh + 1) * LANES)

    def stage_a(j, hh):
        s_ref[hh] = _dot(k_ref[0, pl.ds(pl.multiple_of(j * t, t), t), :], qs_ref[hh])

    def stage_b(j, hh, bias):
        s = s_ref[hh]
        if bias is not None:
            s = s + jnp.concatenate([bias, bias], axis=1)
        _flash_update(s, vt_ref[0, head(hh), pl.ds(pl.multiple_of(j * t, t), t)], m_ref, acc_ref, hh)

    def tile_step(j, bias_ref, prefetch):
        stage_a(j, 1)
        stage_b(j, 0, None if bias_ref is None else bias_ref[0])
        if prefetch:
            stage_a(j + 1, 0)
        stage_b(j, 1, None if bias_ref is None else bias_ref[1])

    if has_meta:
        for hh in range(2):
            s = _dot(mk_ref[0], qs_ref[hh]) + jnp.concatenate([dmeta_ref[0, hh]] * 2, axis=1)
            _flash_update(s, mvt_ref[0, head(hh), :], m_ref, acc_ref, hh)

    stage_a(0, 0)
    if has_meta:
        n_plain = jnp.maximum(qi - 1, 0)

        def body(i, carry):
            tile_step(2 * i, None, True)
            tile_step(2 * i + 1, None, True)
            return carry

        lax.fori_loop(0, n_plain // 2, body, 0)

        @pl.when(n_plain % 2 == 1)
        def _():
            tile_step(n_plain - 1, None, True)

        @pl.when(qi >= 1)
        def _():
            tile_step(qi - 1, dprev_ref, True)

    tile_step(qi, ddiag_ref, False)

    lp = lam_ref[...]
    lam = (jnp.exp(jnp.sum(lp[0:1] * lp[1:2], axis=-1, keepdims=True))
           - jnp.exp(jnp.sum(lp[2:3] * lp[3:4], axis=-1, keepdims=True)) + lam_init)
    outs = []
    for hh in range(2):
        o = _normalized(acc_ref, hh, slice(0, t)) - lam * _normalized(acc_ref, hh, slice(t, 2 * t))
        ms = jnp.mean(o * o, axis=0, keepdims=True)
        outs.append(o * lax.rsqrt(ms + EPS))
    o_nat = jnp.concatenate(outs, axis=0).T
    o_ref[0] = ((o_nat * subg_ref[...]) * (1.0 - lam_init)).astype(bf16)


def _diff_attention(dqt, dk, dvt, meta_kv, tiles, lam_p, sub_g, lam_init, t):
    bsz, tlen, _ = dk.shape
    has_meta = meta_kv is not None
    npair = DIFF_HEADS // 2
    in_specs = [
        pl.BlockSpec((1, LANES, t), lambda b, p, i: (b, p, i)),
        pl.BlockSpec((1, tlen, LANES), lambda b, p, i: (b, 0, p)),
        pl.BlockSpec((1, 2 * LANES, tlen), lambda b, p, i: (b, p, 0)),
    ]
    args = [dqt, dk, dvt]
    if has_meta:
        in_specs += [
            pl.BlockSpec((1, BLK, LANES), lambda b, p, i: (0, 0, p)),
            pl.BlockSpec((1, 2 * LANES, BLK), lambda b, p, i: (0, p, 0)),
            pl.BlockSpec((2, t, t), lambda b, p, i: (p, 0, 0)),
            pl.BlockSpec((2, t, t), lambda b, p, i: (p, 0, 0)),
            pl.BlockSpec((1, 2, BLK, t), lambda b, p, i: (jnp.minimum(i, 1), p, 0, 0)),
        ]
        args += list(meta_kv) + [tiles["dd_diag"], tiles["dd_prev"], tiles["dd_meta"]]
    else:
        in_specs += [pl.BlockSpec((2, t, t), lambda b, p, i: (p, 0, 0))]
        args += [tiles["dd_self"]]
    in_specs += [_const_spec((4, DIFF_QK)), _const_spec((1, LANES))]
    args += [lam_p, sub_g]
    return pl.pallas_call(
        functools.partial(_diff_kernel, t=t, has_meta=has_meta, lam_init=lam_init),
        name="diff_real" if has_meta else "diff_meta",
        out_shape=jax.ShapeDtypeStruct((bsz, tlen, npair * LANES), bf16),
        grid=(bsz, npair, tlen // t),
        in_specs=in_specs,
        out_specs=pl.BlockSpec((1, t, LANES), lambda b, p, i: (b, i, p)),
        scratch_shapes=[
            pltpu.VMEM((2, LANES, 2 * t), bf16),
            pltpu.VMEM((2, 1, 2 * t), f32),
            pltpu.VMEM((2, LANES, 2 * t), f32),
            pltpu.VMEM((2, t, 2 * t), f32),
        ],
        compiler_params=pltpu.CompilerParams(
            dimension_semantics=("parallel", "parallel", "arbitrary"), vmem_limit_bytes=VMEM_LIMIT),
    )(*args)


def _swa_kernel(*refs, tq, has_band):
    if has_band:
        q_ref, k_ref, v_ref, mk_ref, mv_ref, dband_ref, dmeta_ref, sink_ref, o_ref = refs
    else:
        q_ref, mk_ref, mv_ref, dmeta_ref, sink_ref, o_ref = refs
    qi = pl.program_id(1)
    lane = _lane((BLK, LANES))
    row2 = lax.broadcasted_iota(jnp.int32, (2 * BLK, 1), 0)
    n_meta_var = dmeta_ref.shape[0]

    def block(sub, carry):
        blk = qi * (tq // BLK) + sub
        r0 = pl.multiple_of(sub * BLK, BLK)
        mvar = jnp.minimum(blk, n_meta_var - 1)
        if has_band:
            bvar = jnp.minimum(blk, 1)
            koff = pl.multiple_of(jnp.maximum(blk - 1, 0) * BLK, BLK)
            kband = k_ref[0, pl.ds(koff, 2 * BLK), :]
            vband = v_ref[0, pl.ds(koff, 2 * BLK), :]
        for r in range(SWA_REP):
            q2 = q_ref[0, pl.ds(r0, BLK), r * LANES:(r + 1) * LANES]
            zero = jnp.zeros_like(q2)
            qs = jnp.concatenate([jnp.where(lane < SWA_HD, q2, zero), jnp.where(lane >= SWA_HD, q2, zero)],
                                 axis=0)
            h0, h1 = r, SWA_REP + r
            sink = jnp.where(row2 < BLK, sink_ref[h0], sink_ref[h1])
            s_meta = _dot_nt(qs, mk_ref[0]) + jnp.concatenate(
                [dmeta_ref[mvar, h0], dmeta_ref[mvar, h1]], axis=0)
            m = jnp.maximum(jnp.max(s_meta, axis=-1, keepdims=True), sink)
            if has_band:
                s_band = _dot_nt(qs, kband) + jnp.concatenate(
                    [dband_ref[bvar, h0], dband_ref[bvar, h1]], axis=0)
                m = jnp.maximum(m, jnp.max(s_band, axis=-1, keepdims=True))
            p_meta = jnp.exp(s_meta - m)
            denom = jnp.sum(p_meta, axis=-1, keepdims=True) + jnp.exp(sink - m)
            acc = _dot(p_meta.astype(bf16), mv_ref[0])
            if has_band:
                p_band = jnp.exp(s_band - m)
                denom = denom + jnp.sum(p_band, axis=-1, keepdims=True)
                acc = acc + _dot(p_band.astype(bf16), vband)
            o = acc / denom
            o_ref[0, pl.ds(r0, BLK), r * LANES:(r + 1) * LANES] = jnp.where(
                lane < SWA_HD, o[0:BLK], o[BLK:2 * BLK]).astype(bf16)
        return carry

    lax.fori_loop(0, tq // BLK, block, 0)


def _swa_attention(sq, sk, sv, meta_kv, tiles, sinks, tq):
    bsz, tlen, _ = sq.shape
    has_band = meta_kv is not None
    width = SWA_REP * LANES
    smem = pl.BlockSpec(memory_space=pltpu.SMEM)
    if has_band:
        in_specs = [
            pl.BlockSpec((1, tq, width), lambda b, i: (b, i, 0)),
            pl.BlockSpec((1, tlen, LANES), lambda b, i: (b, 0, 0)),
            pl.BlockSpec((1, tlen, LANES), lambda b, i: (b, 0, 0)),
            _const_spec((1, BLK, LANES)),
            _const_spec((1, BLK, LANES)),
            _const_spec(tiles["ds_band"].shape),
            _const_spec(tiles["ds_meta"].shape),
            smem,
        ]
        args = [sq, sk, sv, meta_kv[0], meta_kv[1], tiles["ds_band"], tiles["ds_meta"], sinks]
    else:
        in_specs = [
            pl.BlockSpec((1, tq, width), lambda b, i: (b, i, 0)),
            _const_spec((1, BLK, LANES)),
            _const_spec((1, BLK, LANES)),
            _const_spec(tiles["ds_self"].shape),
            smem,
        ]
        args = [sq, sk, sv, tiles["ds_self"], sinks]
    return pl.pallas_call(
        functools.partial(_swa_kernel, tq=tq, has_band=has_band),
        name="swa_real" if has_band else "swa_meta",
        out_shape=jax.ShapeDtypeStruct((bsz, tlen, width), bf16),
        grid=(bsz, tlen // tq),
        in_specs=in_specs,
        out_specs=pl.BlockSpec((1, tq, width), lambda b, i: (b, i, 0)),
        compiler_params=pltpu.CompilerParams(
            dimension_semantics=("parallel", "parallel"), vmem_limit_bytes=VMEM_LIMIT),
    )(*args)


def _out_ffn_kernel(h_ref, ya_ref, yb_ref, yc_ref, woa_ref, wob_ref, woc_ref, g_ref, wg_ref, wu_ref, wd_ref,
                    fg_ref, o_ref, *, final):
    h1 = (h_ref[0] + _dot(ya_ref[0], woa_ref[...]) + _dot(yb_ref[0], wob_ref[...])
          + _dot(yc_ref[0], woc_ref[...]))
    hn = _rms(h1, g_ref[...]).astype(bf16)
    gate = _dot(hn, wg_ref[...])
    up = _dot(hn, wu_ref[...])
    act = ((gate * jax.nn.sigmoid(gate)) * up).astype(bf16)
    h2 = h1 + _dot(act, wd_ref[...])
    if final:
        h2 = _rms(h2, fg_ref[...])
    o_ref[0] = h2


def _single(shape):
    nd = len(shape)
    return pl.BlockSpec(shape, lambda *_: (0,) * nd, pipeline_mode=pl.Buffered(1))


def _out_ffn(h, ya, yb, yc, lw, final_g, final, tm):
    bsz, t, _ = h.shape

    def row(w):
        return pl.BlockSpec((1, tm, w), lambda b, i: (b, i, 0))

    in_specs = [
        row(D_MODEL), row(384), row(256), row(384),
        _single((384, D_MODEL)), _single((256, D_MODEL)), _single((384, D_MODEL)),
        _single((1, D_MODEL)),
        _single((D_MODEL, D_FF)), _single((D_MODEL, D_FF)), _single((D_FF, D_MODEL)),
        _single((1, D_MODEL)),
    ]
    return pl.pallas_call(
        functools.partial(_out_ffn_kernel, final=final),
        name="out_ffn_real" if bsz > 1 else "out_ffn_meta",
        out_shape=jax.ShapeDtypeStruct((bsz, t, D_MODEL), f32),
        grid=(bsz, t // tm),
        in_specs=in_specs,
        out_specs=row(D_MODEL),
        compiler_params=pltpu.CompilerParams(
            dimension_semantics=("parallel", "parallel"), vmem_limit_bytes=VMEM_LIMIT),
    )(h, ya, yb, yc, lw["wo_a"], lw["wo_b"], lw["wo_c"], lw["ffn_norm"], lw["w_gate"], lw["w_up"],
      lw["w_down"], final_g)


def _rot_half_cols(w):
    half = MLA_ROPE // 2
    return jnp.concatenate([-w[:, half:], w[:, :half]], axis=1)


def _layer_weights(l, attn_norm, w_in, mla_q_norm, mla_w_qb, mla_kv_norm, mla_w_kvb, w_out, ffn_norm,
                   w_gate, w_up, w_down):
    w = w_in[l]
    o = 0
    parts = {}
    for name, width in (("c_q", 256), ("c_kv", 128), ("k_rope", 32), ("dq", 256), ("dk", 256), ("dv", 256),
                        ("sq", 384), ("sk", 128), ("sv", 128)):
        parts[name] = w[:, o:o + width]
        o += width
    kr = parts["k_rope"]
    ksw = _rot_half_cols(kr)
    z64 = jnp.zeros((D_MODEL, 64), f32)
    swa_order = [g * SWA_REP + r for r in range(SWA_REP) for g in range(SWA_KV_HEADS)]
    sq = jnp.concatenate([parts["sq"][:, h * SWA_HD:(h + 1) * SWA_HD] for h in swa_order], axis=1)
    w_nat = jnp.concatenate([parts["c_q"], parts["c_kv"], z64, kr, kr, z64, ksw, ksw, parts["dk"], sq,
                             parts["sk"], parts["sv"]], axis=1)
    assert w_nat.shape == (D_MODEL, D_NAT)
    w_tr = jnp.concatenate([parts["dq"], parts["dv"]], axis=1).T
    assert w_tr.shape == (D_TR, D_MODEL)

    wq = mla_w_qb[l]
    q_cols = []
    for h in range(MLA_HEADS):
        base = h * (MLA_NOPE + MLA_ROPE)
        rope = wq[:, base + MLA_NOPE:base + MLA_NOPE + MLA_ROPE]
        q_cols += [wq[:, base:base + MLA_NOPE], rope, _rot_half_cols(rope)]
    w_qbt = jnp.concatenate(q_cols, axis=1).T

    wkv = mla_w_kvb[l]
    zk = jnp.zeros((MLA_KV_RANK, LANES - MLA_NOPE), f32)
    k_cols, v_cols = [], []
    for h in range(MLA_HEADS):
        base = h * (MLA_NOPE + MLA_V)
        k_cols += [wkv[:, base:base + MLA_NOPE], zk]
        v_cols.append(wkv[:, base + MLA_NOPE:base + MLA_NOPE + MLA_V])
    w_kb = jnp.concatenate(k_cols, axis=1)
    w_vt = jnp.concatenate(v_cols, axis=1).T

    wo = w_out[l]
    na = MLA_HEADS * MLA_V
    nb = DIFF_HEADS * DIFF_V
    wo_c = wo[na + nb:]
    wo_c = jnp.concatenate([wo_c[h * SWA_HD:(h + 1) * SWA_HD] for h in swa_order], axis=0)
    return {
        "attn_norm": attn_norm[l][None], "w_nat": w_nat.astype(bf16), "w_tr": w_tr.astype(bf16),
        "q_norm": mla_q_norm[l][None], "w_qbt": w_qbt.astype(bf16),
        "kv_norm": mla_kv_norm[l][None], "w_kb": w_kb.astype(bf16), "w_vt": w_vt.astype(bf16),
        "wo_a": wo[:na].astype(bf16), "wo_b": wo[na:na + nb].astype(bf16), "wo_c": wo_c.astype(bf16),
        "ffn_norm": ffn_norm[l][None],
        "w_gate": w_gate[l].astype(bf16), "w_up": w_up[l].astype(bf16), "w_down": w_down[l].astype(bf16),
    }


def _rope_tables(pos):
    inv_freq = ROPE_THETA ** (-jnp.arange(0, MLA_ROPE, 2, dtype=f32) / MLA_ROPE)
    ang = pos.astype(f32)[:, None] * inv_freq[None, :]
    cos, sin = jnp.cos(ang), jnp.sin(ang)
    n = pos.shape[0]
    cc = jnp.concatenate([cos, cos], axis=1)
    ss = jnp.concatenate([sin, sin], axis=1)
    z = jnp.zeros((n, 64), f32)
    return {
        "csqt": (jnp.concatenate([jnp.ones((n, 64), f32), cc, ss], axis=1) * (MLA_SCALE * LOG2E)).T,
        "cka": jnp.concatenate([z, cc, cc], axis=1),
        "ckb": jnp.concatenate([z, ss, ss], axis=1),
    }


def _mixers(proj, meta_proj, tiles, lam_p, sub_g, lam_init, sinks, real):
    mqt, mk, mvt, dqt, dk, dvt, sq, sk, sv = proj
    if real:
        m_mk, m_mvt, m_dk, m_dvt, m_sk, m_sv = (meta_proj[i] for i in (1, 2, 4, 5, 7, 8))
        ya = _mla_attention(mqt, mk, mvt, (m_mk, m_mvt), MLA_TILE)
        yb = _diff_attention(dqt, dk, dvt, (m_dk, m_dvt), tiles, lam_p, sub_g, lam_init, DIFF_TILE)
        yc = _swa_attention(sq, sk, sv, (m_sk, m_sv), tiles, sinks, SWA_QTILE)
    else:
        ya = _mla_attention(mqt, mk, mvt, None, BLK)
        yb = _diff_attention(dqt, dk, dvt, None, tiles, lam_p, sub_g, lam_init, BLK)
        yc = _swa_attention(sq, sk, sv, None, tiles, sinks, BLK)
    return ya, yb, yc


def kernel(x, meta_tokens, rel_bias, attn_norm, w_in, mla_q_norm, mla_w_qb, mla_kv_norm, mla_w_kvb, diff_lambda,
           diff_subln, swa_sinks, w_out, ffn_norm, w_gate, w_up, w_down, final_norm):
    bsz, seq, _ = x.shape
    assert seq % MLA_TILE == 0 and seq % DIFF_TILE == 0 and seq % SWA_QTILE == 0 and seq % ROW_TILE == 0

    h = x
    h_meta = jnp.concatenate([meta_tokens.astype(f32), jnp.zeros((BLK - N_META, D_MODEL), f32)], axis=0)[None]
    tabs = _rope_tables(N_META + jnp.arange(seq))
    tabs_meta = _rope_tables(jnp.minimum(jnp.arange(BLK), N_META - 1))
    tiles = dict(zip(("dd_diag", "dd_prev", "dd_meta", "dd_self", "ds_band", "ds_meta", "ds_self"),
                     _bias_tiles(rel_bias)))
    final_g = final_norm[None]

    for l in range(DEPTH):
        lw = _layer_weights(l, attn_norm, w_in, mla_q_norm, mla_w_qb, mla_kv_norm, mla_w_kvb, w_out, ffn_norm,
                            w_gate, w_up, w_down)
        lam_init = 0.8 - 0.6 * math.exp(-0.3 * l)
        sub_g = jnp.concatenate([diff_subln[l], diff_subln[l]])[None]
        last = l == DEPTH - 1

        meta_proj = _project(h_meta, lw, tabs_meta, BLK)
        proj = _project(h, lw, tabs, ROW_TILE)
        if not last:
            y_meta = _mixers(meta_proj, None, tiles, diff_lambda[l], sub_g, lam_init, swa_sinks[l], real=False)
            h_meta = _out_ffn(h_meta, *y_meta, lw, final_g, False, BLK)
        y = _mixers(proj, meta_proj, tiles, diff_lambda[l], sub_g, lam_init, swa_sinks[l], real=True)
        h = _out_ffn(h, *y, lw, final_g, last, ROW_TILE)
    return h
```

```python
import functools
import math

import numpy as np
import jax
import jax.numpy as jnp
from jax import lax
from jax.experimental import pallas as pl
from jax.experimental.pallas import tpu as pltpu

D_MODEL = 1024
DEPTH = 2
N_META = 16
BLK = 128

MLA_HEADS = 6
MLA_Q_RANK = 256
MLA_KV_RANK = 128
MLA_NOPE = 64
MLA_ROPE = 32
MLA_V = 64
ROPE_THETA = 10000.0

DIFF_HEADS = 4
DIFF_QK = 32
DIFF_V = 64

SWA_HEADS = 6
SWA_KV_HEADS = 2
SWA_REP = SWA_HEADS // SWA_KV_HEADS
SWA_HD = 64
WINDOW = 128

REL_BUCKETS = 32
REL_MAX_DIST = 128
D_FF = 2816
NEG_INF = -1e30
EPS = 1e-6
LOG2E = math.log2(math.e)

MLA_SCALE = (MLA_NOPE + MLA_ROPE) ** -0.5
DIFF_SCALE = DIFF_QK ** -0.5
SWA_SCALE = SWA_HD ** -0.5

LANES = 128
HEAD_V = 64
D_NAT = 1024
D_TR = 1024
VMEM_LIMIT = 56 * 1024 * 1024

MLA_TILE = 512
DIFF_TILE = 512
SWA_QTILE = 512
ROW_TILE = 512

f32 = jnp.float32
bf16 = jnp.bfloat16


def _bucket_thresholds():
    n = np.arange(0, 4 * REL_MAX_DIST)
    max_exact = REL_BUCKETS // 2
    nf = np.maximum(n, max_exact).astype(np.float32)
    large = max_exact + (np.log(nf / np.float32(max_exact)) / np.float32(math.log(REL_MAX_DIST / max_exact))
                         * np.float32(REL_BUCKETS - max_exact)).astype(np.int32)
    bucket = np.where(n < max_exact, n, np.minimum(large, REL_BUCKETS - 1))
    assert np.all(np.diff(bucket) >= 0) and bucket[-1] == REL_BUCKETS - 1
    return [int(np.argmax(bucket >= k)) for k in range(1, REL_BUCKETS)]


BUCKET_THR = _bucket_thresholds()
FAR_DIST = BUCKET_THR[-1]
assert FAR_DIST <= BLK - N_META + 1


def _dot(a, b):
    return jnp.dot(a, b, preferred_element_type=f32)


def _dot_nt(a, b):
    return lax.dot_general(a, b, (((1,), (1,)), ((), ())), preferred_element_type=f32)


def _rms(x, g):
    return (x * lax.rsqrt(jnp.mean(x * x, axis=-1, keepdims=True) + EPS)) * g


def _lane(shape):
    return lax.broadcasted_iota(jnp.int32, shape, len(shape) - 1)


def _row(shape):
    return lax.broadcasted_iota(jnp.int32, shape, 0)


def _bias_lookup(n, tab_ref, h):
    acc = jnp.full(n.shape, tab_ref[0, h], f32)
    for k, thr in enumerate(BUCKET_THR, start=1):
        acc = jnp.where(n >= thr, tab_ref[k, h], acc)
    return acc


def _bias_kernel(tab_ref, dd_diag, dd_prev, dd_meta, dd_self, ds_band, ds_meta, ds_self, *, td):
    rc = 32

    def rows_cols(r0, cols):
        a = r0 + lax.broadcasted_iota(jnp.int32, (rc, cols), 0)
        b = lax.broadcasted_iota(jnp.int32, (rc, cols), 1)
        return a, b

    def diff_body(i, carry):
        r0 = pl.multiple_of(i * rc, rc)
        k, q = rows_cols(r0, td)
        for h in range(DIFF_HEADS):
            c = tab_ref[REL_BUCKETS - 1, h]
            dd_diag[h, pl.ds(r0, rc), :] = jnp.where(
                k <= q, (_bias_lookup(jnp.maximum(q - k, 0), tab_ref, h) - c) * LOG2E, NEG_INF)
            dd_prev[h, pl.ds(r0, rc), :] = (_bias_lookup(q + td - k, tab_ref, h) - c) * LOG2E
        return carry

    lax.fori_loop(0, td // rc, diff_body, 0)

    def blk_body(i, carry):
        r0 = pl.multiple_of(i * rc, rc)
        j, q = rows_cols(r0, td)
        j2, q2 = rows_cols(r0, BLK)
        self_ok = j2 <= jnp.minimum(q2, N_META - 1)
        for h in range(DIFF_HEADS):
            c = tab_ref[REL_BUCKETS - 1, h]
            dd_meta[0, h, pl.ds(r0, rc), :] = jnp.where(
                j < N_META, (_bias_lookup(N_META + q - j, tab_ref, h) - c) * LOG2E, NEG_INF)
            dd_meta[1, h, pl.ds(r0, rc), :] = jnp.where(j < N_META, 0.0, NEG_INF)
            dd_self[h, pl.ds(r0, rc), :] = jnp.where(
                self_ok, (_bias_lookup(jnp.maximum(q2 - j2, 0), tab_ref, h) - c) * LOG2E, NEG_INF)
        k, a = j2, q2
        for h in range(SWA_HEADS):
            hb = DIFF_HEADS + h
            near = _bias_lookup(jnp.maximum(a - k, 0), tab_ref, hb) * LOG2E
            cur = jnp.where(k <= a, near, NEG_INF)
            prev = jnp.where(k > a, _bias_lookup(a + BLK - k, tab_ref, hb) * LOG2E, NEG_INF)
            ds_band[0, h, pl.ds(r0, rc), :] = cur
            ds_band[0, h, pl.ds(BLK + r0, rc), :] = jnp.full((rc, BLK), NEG_INF, f32)
            ds_band[1, h, pl.ds(r0, rc), :] = prev
            ds_band[1, h, pl.ds(BLK + r0, rc), :] = cur
            ds_meta[0, h, pl.ds(r0, rc), :] = jnp.where(
                k < N_META, _bias_lookup(N_META + a - k, tab_ref, hb) * LOG2E, NEG_INF)
            far = jnp.full((rc, BLK), tab_ref[REL_BUCKETS - 1, hb], f32) * LOG2E
            ds_meta[1, h, pl.ds(r0, rc), :] = jnp.where(k < N_META, far, NEG_INF)
            ds_self[0, h, pl.ds(r0, rc), :] = jnp.where(self_ok, near, NEG_INF)
        return carry

    lax.fori_loop(0, BLK // rc, blk_body, 0)


def _bias_tiles(rel_bias):
    td = DIFF_TILE
    outs = (
        jax.ShapeDtypeStruct((DIFF_HEADS, td, td), f32),
        jax.ShapeDtypeStruct((DIFF_HEADS, td, td), f32),
        jax.ShapeDtypeStruct((2, DIFF_HEADS, BLK, td), f32),
        jax.ShapeDtypeStruct((DIFF_HEADS, BLK, BLK), f32),
        jax.ShapeDtypeStruct((2, SWA_HEADS, 2 * BLK, BLK), f32),
        jax.ShapeDtypeStruct((2, SWA_HEADS, BLK, LANES), f32),
        jax.ShapeDtypeStruct((1, SWA_HEADS, BLK, LANES), f32),
    )
    return pl.pallas_call(
        functools.partial(_bias_kernel, td=td),
        name="bias_tiles",
        out_shape=outs,
        in_specs=[pl.BlockSpec(memory_space=pltpu.SMEM)],
        compiler_params=pltpu.CompilerParams(vmem_limit_bytes=VMEM_LIMIT),
    )(rel_bias)


def _store_vt(out_ref, vt, heads):
    tm = vt.shape[1]
    ones_blk = jnp.where(_row((HEAD_V, tm)) == 0, 1.0, 0.0).astype(bf16)
    for hd in range(heads):
        out_ref[0, hd * LANES:hd * LANES + HEAD_V, :] = vt[hd * HEAD_V:(hd + 1) * HEAD_V].astype(bf16)
        out_ref[0, hd * LANES + HEAD_V:(hd + 1) * LANES, :] = ones_blk


def _proj_kernel(h_ref, g_ref, wnat_ref, wtr_ref, qn_ref, wqbt_ref, kvn_ref, wkb_ref, wvt_ref,
                 csqt_ref, cka_ref, ckb_ref,
                 mqt_ref, mk_ref, mvt_ref, dqt_ref, dk_ref, dvt_ref, sqt_ref, sk_ref, svt_ref):
    hn = _rms(h_ref[0], g_ref[...]).astype(bf16)
    proj = _dot(hn, wnat_ref[...])
    tr = _dot_nt(wtr_ref[...], hn)
    c_q = proj[:, 0:256]
    c_kv = proj[:, 256:384]
    rope_a = proj[:, 384:512]
    rope_b = proj[:, 512:640]
    dk_ref[0] = proj[:, 640:896].astype(bf16)
    sk_ref[0] = proj[:, 896:1024].astype(bf16)
    dqt_ref[0] = (tr[0:256] * (DIFF_SCALE * LOG2E)).astype(bf16)
    _store_vt(dvt_ref, tr[256:512], DIFF_HEADS)
    sqt_ref[0] = (tr[512:896] * (SWA_SCALE * LOG2E)).astype(bf16)
    svt_ref[0] = tr[896:1024].astype(bf16)

    cqn = _rms(c_q, qn_ref[...]).astype(bf16)
    ckvn = _rms(c_kv, kvn_ref[...]).astype(bf16)
    qt = _dot_nt(wqbt_ref[...], cqn)
    kk = _dot(ckvn, wkb_ref[...])
    vt = _dot_nt(wvt_ref[...], ckvn)
    csqt = csqt_ref[...]
    k_rot = rope_a * cka_ref[...] + rope_b * ckb_ref[...]
    for hd in range(MLA_HEADS):
        sl = slice(hd * LANES, (hd + 1) * LANES)
        mqt_ref[0, sl, :] = (qt[sl] * csqt).astype(bf16)
        mk_ref[0, :, sl] = (kk[:, sl] + k_rot).astype(bf16)
    _store_vt(mvt_ref, vt, MLA_HEADS)


def _const_spec(shape):
    nd = len(shape)
    return pl.BlockSpec(shape, lambda *_: (0,) * nd)


def _project(h, lw, tabs, tm):
    bsz, t, _ = h.shape
    outs = ((768, True), (768, False), (768, True), (256, True), (256, False), (512, True),
            (384, True), (128, False), (128, True))
    out_shape = tuple(jax.ShapeDtypeStruct((bsz, w, t) if tr else (bsz, t, w), bf16) for w, tr in outs)
    out_specs = tuple(pl.BlockSpec((1, w, tm), lambda b, i: (b, 0, i)) if tr
                      else pl.BlockSpec((1, tm, w), lambda b, i: (b, i, 0)) for w, tr in outs)
    tab_spec = pl.BlockSpec((tm, LANES), lambda b, i: (i, 0))
    in_specs = [
        pl.BlockSpec((1, tm, D_MODEL), lambda b, i: (b, i, 0)),
        _const_spec((1, D_MODEL)),
        _const_spec((D_MODEL, D_NAT)),
        _const_spec((D_TR, D_MODEL)),
        _const_spec((1, MLA_Q_RANK)),
        _const_spec((768, MLA_Q_RANK)),
        _const_spec((1, MLA_KV_RANK)),
        _const_spec((MLA_KV_RANK, 768)),
        _const_spec((384, MLA_KV_RANK)),
        pl.BlockSpec((LANES, tm), lambda b, i: (0, i)),
        tab_spec, tab_spec,
    ]
    return pl.pallas_call(
        _proj_kernel,
        name="proj_real" if bsz > 1 else "proj_meta",
        out_shape=out_shape,
        grid=(bsz, t // tm),
        in_specs=in_specs,
        out_specs=out_specs,
        compiler_params=pltpu.CompilerParams(
            dimension_semantics=("parallel", "parallel"), vmem_limit_bytes=VMEM_LIMIT),
    )(h, lw["attn_norm"], lw["w_nat"], lw["w_tr"], lw["q_norm"], lw["w_qbt"], lw["kv_norm"], lw["w_kb"],
      lw["w_vt"], tabs["csqt"], tabs["cka"], tabs["ckb"])


def _flash_update(s, vt, m_ref, acc_ref, idx):
    m_prev = m_ref[idx]
    m_new = jnp.maximum(m_prev, jnp.max(s, axis=0, keepdims=True))
    alpha = jnp.exp2(m_prev - m_new)
    p = jnp.exp2(s - m_new).astype(bf16)
    acc_ref[idx] = acc_ref[idx] * alpha + _dot(vt, p)
    m_ref[idx] = m_new


def _init_stats(m_ref, acc_ref):
    m_ref[...] = jnp.full(m_ref.shape, NEG_INF, f32)
    acc_ref[...] = jnp.zeros(acc_ref.shape, f32)


def _normalized(acc_ref, idx, cols):
    return acc_ref[idx, 0:HEAD_V, cols] / acc_ref[idx, HEAD_V:HEAD_V + 1, cols]


def _mla_kernel(*refs, t, has_meta):
    if has_meta:
        qt_ref, k_ref, vt_ref, mk_ref, mvt_ref, o_ref, m_ref, acc_ref, s_ref = refs
    else:
        qt_ref, k_ref, vt_ref, o_ref, m_ref, acc_ref, s_ref = refs
    qi = pl.program_id(2)
    _init_stats(m_ref, acc_ref)

    def head(hh):
        return slice(hh * LANES, (hh + 1) * LANES)

    if has_meta:
        valid = _row((BLK, t)) < N_META
        for hh in range(2):
            s = _dot(mk_ref[0, :, head(hh)], qt_ref[0, head(hh), :])
            _flash_update(jnp.where(valid, s, NEG_INF), mvt_ref[0, head(hh), :], m_ref, acc_ref, hh)

    def stage_a(j, hh):
        s_ref[hh] = _dot(k_ref[0, pl.ds(pl.multiple_of(j * t, t), t), head(hh)], qt_ref[0, head(hh), :])

    def stage_b(j, hh, mask):
        s = s_ref[hh]
        if mask is not None:
            s = jnp.where(mask, s, NEG_INF)
        _flash_update(s, vt_ref[0, head(hh), pl.ds(pl.multiple_of(j * t, t), t)], m_ref, acc_ref, hh)

    def tile_step(j, mask, prefetch):
        stage_a(j, 1)
        stage_b(j, 0, mask)
        if prefetch:
            stage_a(j + 1, 0)
        stage_b(j, 1, mask)

    stage_a(0, 0)

    def body(i, carry):
        tile_step(2 * i, None, True)
        tile_step(2 * i + 1, None, True)
        return carry

    lax.fori_loop(0, qi // 2, body, 0)

    @pl.when(qi % 2 == 1)
    def _():
        tile_step(qi - 1, None, True)

    key = _row((t, t))
    qry = _lane((t, t))
    if has_meta:
        mask = key <= qry
    else:
        mask = key <= jnp.minimum(qry, N_META - 1)
    tile_step(qi, mask, False)

    full = slice(0, t)
    ot = jnp.concatenate([_normalized(acc_ref, 0, full), _normalized(acc_ref, 1, full)], axis=0)
    o_ref[0] = ot.T.astype(bf16)


def _mla_attention(mqt, mk, mvt, meta_kv, t):
    bsz, tlen, _ = mk.shape
    has_meta = meta_kv is not None
    npair = MLA_HEADS // 2
    in_specs = [
        pl.BlockSpec((1, 2 * LANES, t), lambda b, p, i: (b, p, i)),
        pl.BlockSpec((1, tlen, 2 * LANES), lambda b, p, i: (b, 0, p)),
        pl.BlockSpec((1, 2 * LANES, tlen), lambda b, p, i: (b, p, 0)),
    ]
    args = [mqt, mk, mvt]
    if has_meta:
        in_specs += [
            pl.BlockSpec((1, BLK, 2 * LANES), lambda b, p, i: (0, 0, p)),
            pl.BlockSpec((1, 2 * LANES, BLK), lambda b, p, i: (0, p, 0)),
        ]
        args += list(meta_kv)
    return pl.pallas_call(
        functools.partial(_mla_kernel, t=t, has_meta=has_meta),
        name="mla_real" if has_meta else "mla_meta",
        out_shape=jax.ShapeDtypeStruct((bsz, tlen, npair * LANES), bf16),
        grid=(bsz, npair, tlen // t),
        in_specs=in_specs,
        out_specs=pl.BlockSpec((1, t, LANES), lambda b, p, i: (b, i, p)),
        scratch_shapes=[
            pltpu.VMEM((2, 1, t), f32),
            pltpu.VMEM((2, LANES, t), f32),
            pltpu.VMEM((2, t, t), f32),
        ],
        compiler_params=pltpu.CompilerParams(
            dimension_semantics=("parallel", "parallel", "arbitrary"), vmem_limit_bytes=VMEM_LIMIT),
    )(*args)


def _diff_kernel(*refs, t, has_meta, lam_init):
    if has_meta:
        (qt_ref, k_ref, vt_ref, mk_ref, mvt_ref, ddiag_ref, dprev_ref, dmeta_ref, lam_ref, subg_ref,
         o_ref, qs_ref, m_ref, acc_ref, s_ref) = refs
    else:
        (qt_ref, k_ref, vt_ref, ddiag_ref, lam_ref, subg_ref, o_ref, qs_ref, m_ref, acc_ref, s_ref) = refs
    qi = pl.program_id(2)
    _init_stats(m_ref, acc_ref)

    qt = qt_ref[0]
    group = _row((LANES, t)) // DIFF_QK
    for hh in range(2):
        for c in range(2):
            qs_ref[hh, :, c * t:(c + 1) * t] = jnp.where(group == 2 * hh + c, qt, jnp.zeros_like(qt))

    def head(hh):
        return slice(hh * LANES, (hh + 1) * LANES)

    def stage_a(j, hh):
        s_ref[hh] = _dot(k_ref[0, pl.ds(pl.multiple_of(j * t, t), t), :], qs_ref[hh])

    def stage_b(j, hh, bias):
        s = s_ref[hh]
        if bias is not None:
            s = s + jnp.concatenate([bias, bias], axis=1)
        _flash_update(s, vt_ref[0, head(hh), pl.ds(pl.multiple_of(j * t, t), t)], m_ref, acc_ref, hh)

    def tile_step(j, bias_ref, prefetch):
        stage_a(j, 1)
        stage_b(j, 0, None if bias_ref is None else bias_ref[0])
        if prefetch:
            stage_a(j + 1, 0)
        stage_b(j, 1, None if bias_ref is None else bias_ref[1])

    if has_meta:
        for hh in range(2):
            s = _dot(mk_ref[0], qs_ref[hh]) + jnp.concatenate([dmeta_ref[0, hh]] * 2, axis=1)
            _flash_update(s, mvt_ref[0, head(hh), :], m_ref, acc_ref, hh)

    stage_a(0, 0)
    if has_meta:
        n_plain = jnp.maximum(qi - 1, 0)

        def body(j, carry):
            tile_step(j, None, True)
            return carry

        lax.fori_loop(0, n_plain, body, 0)

        @pl.when(qi >= 1)
        def _():
            tile_step(qi - 1, dprev_ref, True)

    tile_step(qi, ddiag_ref, False)

    lp = lam_ref[...]
    lam = (jnp.exp(jnp.sum(lp[0:1] * lp[1:2], axis=-1, keepdims=True))
           - jnp.exp(jnp.sum(lp[2:3] * lp[3:4], axis=-1, keepdims=True)) + lam_init)
    outs = []
    for hh in range(2):
        o = _normalized(acc_ref, hh, slice(0, t)) - lam * _normalized(acc_ref, hh, slice(t, 2 * t))
        ms = jnp.mean(o * o, axis=0, keepdims=True)
        outs.append(o * lax.rsqrt(ms + EPS))
    o_nat = jnp.concatenate(outs, axis=0).T
    o_ref[0] = ((o_nat * subg_ref[...]) * (1.0 - lam_init)).astype(bf16)


def _diff_attention(dqt, dk, dvt, meta_kv, tiles, lam_p, sub_g, lam_init, t):
    bsz, tlen, _ = dk.shape
    has_meta = meta_kv is not None
    npair = DIFF_HEADS // 2
    in_specs = [
        pl.BlockSpec((1, LANES, t), lambda b, p, i: (b, p, i)),
        pl.BlockSpec((1, tlen, LANES), lambda b, p, i: (b, 0, p)),
        pl.BlockSpec((1, 2 * LANES, tlen), lambda b, p, i: (b, p, 0)),
    ]
    args = [dqt, dk, dvt]
    if has_meta:
        in_specs += [
            pl.BlockSpec((1, BLK, LANES), lambda b, p, i: (0, 0, p)),
            pl.BlockSpec((1, 2 * LANES, BLK), lambda b, p, i: (0, p, 0)),
            pl.BlockSpec((2, t, t), lambda b, p, i: (p, 0, 0)),
            pl.BlockSpec((2, t, t), lambda b, p, i: (p, 0, 0)),
            pl.BlockSpec((1, 2, BLK, t), lambda b, p, i: (jnp.minimum(i, 1), p, 0, 0)),
        ]
        args += list(meta_kv) + [tiles["dd_diag"], tiles["dd_prev"], tiles["dd_meta"]]
    else:
        in_specs += [pl.BlockSpec((2, t, t), lambda b, p, i: (p, 0, 0))]
        args += [tiles["dd_self"]]
    in_specs += [_const_spec((4, DIFF_QK)), _const_spec((1, LANES))]
    args += [lam_p, sub_g]
    return pl.pallas_call(
        functools.partial(_diff_kernel, t=t, has_meta=has_meta, lam_init=lam_init),
        name="diff_real" if has_meta else "diff_meta",
        out_shape=jax.ShapeDtypeStruct((bsz, tlen, npair * LANES), bf16),
        grid=(bsz, npair, tlen // t),
        in_specs=in_specs,
        out_specs=pl.BlockSpec((1, t, LANES), lambda b, p, i: (b, i, p)),
        scratch_shapes=[
            pltpu.VMEM((2, LANES, 2 * t), bf16),
            pltpu.VMEM((2, 1, 2 * t), f32),
            pltpu.VMEM((2, LANES, 2 * t), f32),
            pltpu.VMEM((2, t, 2 * t), f32),
        ],
        compiler_params=pltpu.CompilerParams(
            dimension_semantics=("parallel", "parallel", "arbitrary"), vmem_limit_bytes=VMEM_LIMIT),
    )(*args)


def _swa_kernel(*refs, tq, has_band):
    if has_band:
        qt_ref, k_ref, vt_ref, mk_ref, mvt_ref, dband_ref, dmeta_ref, sink_ref, o_ref, sm_ref, sb_ref = refs
    else:
        qt_ref, mk_ref, mvt_ref, dmeta_ref, sink_ref, o_ref, sm_ref = refs
    qi = pl.program_id(1)
    feat = _row((LANES, BLK))
    first_head = _lane((1, 2 * BLK)) < BLK
    n_meta_var = dmeta_ref.shape[0]

    def block(sub):
        blk = qi * (tq // BLK) + sub
        c0 = sub * BLK
        mvar = jnp.minimum(blk, n_meta_var - 1)
        if has_band:
            bvar = jnp.minimum(blk, 1)
            koff = pl.multiple_of(jnp.maximum(blk - 1, 0) * BLK, BLK)
            kband = k_ref[0, pl.ds(koff, 2 * BLK), :]
            vtband = vt_ref[0, :, pl.ds(koff, 2 * BLK)]

        def stage_a(r):
            q2t = qt_ref[0, r * LANES:(r + 1) * LANES, c0:c0 + BLK]
            zero = jnp.zeros_like(q2t)
            qs = jnp.concatenate([jnp.where(feat < SWA_HD, q2t, zero), jnp.where(feat >= SWA_HD, q2t, zero)],
                                 axis=1)
            sm_ref[r] = _dot(mk_ref[0], qs)
            if has_band:
                sb_ref[r] = _dot(kband, qs)

        def stage_b(r):
            h0, h1 = r, SWA_REP + r
            sink = jnp.where(first_head, sink_ref[h0], sink_ref[h1]) * LOG2E
            s_meta = sm_ref[r] + jnp.concatenate([dmeta_ref[mvar, h0], dmeta_ref[mvar, h1]], axis=1)
            m = jnp.maximum(jnp.max(s_meta, axis=0, keepdims=True), sink)
            if has_band:
                s_band = sb_ref[r] + jnp.concatenate([dband_ref[bvar, h0], dband_ref[bvar, h1]], axis=1)
                m = jnp.maximum(m, jnp.max(s_band, axis=0, keepdims=True))
            p_meta = jnp.exp2(s_meta - m)
            denom = jnp.sum(p_meta, axis=0, keepdims=True) + jnp.exp2(sink - m)
            acc = _dot(mvt_ref[0], p_meta.astype(bf16))
            if has_band:
                p_band = jnp.exp2(s_band - m)
                denom = denom + jnp.sum(p_band, axis=0, keepdims=True)
                acc = acc + _dot(vtband, p_band.astype(bf16))
            ot = acc / denom
            pair = jnp.where(feat < SWA_HD, ot[:, 0:BLK], ot[:, BLK:2 * BLK])
            o_ref[0, c0:c0 + BLK, r * LANES:(r + 1) * LANES] = pair.T.astype(bf16)

        return stage_a, stage_b

    stages = [block(sub) for sub in range(tq // BLK)]
    chains = [(sub, r) for sub in range(tq // BLK) for r in range(SWA_REP)]
    stages[0][0](0)
    for i, (sub, r) in enumerate(chains):
        if i + 1 < len(chains):
            nsub, nr = chains[i + 1]
            stages[nsub][0](nr)
        stages[sub][1](r)


def _swa_attention(sqt, sk, svt, meta_kv, tiles, sinks, tq):
    bsz, width, tlen = sqt.shape
    has_band = meta_kv is not None
    smem = pl.BlockSpec(memory_space=pltpu.SMEM)
    if has_band:
        in_specs = [
            pl.BlockSpec((1, width, tq), lambda b, i: (b, 0, i)),
            pl.BlockSpec((1, tlen, LANES), lambda b, i: (b, 0, 0)),
            pl.BlockSpec((1, LANES, tlen), lambda b, i: (b, 0, 0)),
            _const_spec((1, BLK, LANES)),
            _const_spec((1, LANES, BLK)),
            _const_spec(tiles["ds_band"].shape),
            _const_spec(tiles["ds_meta"].shape),
            smem,
        ]
        args = [sqt, sk, svt, meta_kv[0], meta_kv[1], tiles["ds_band"], tiles["ds_meta"], sinks]
    else:
        in_specs = [
            pl.BlockSpec((1, width, tq), lambda b, i: (b, 0, i)),
            _const_spec((1, BLK, LANES)),
            _const_spec((1, LANES, BLK)),
            _const_spec(tiles["ds_self"].shape),
            smem,
        ]
        args = [sqt, sk, svt, tiles["ds_self"], sinks]
    return pl.pallas_call(
        functools.partial(_swa_kernel, tq=tq, has_band=has_band),
        name="swa_real" if has_band else "swa_meta",
        out_shape=jax.ShapeDtypeStruct((bsz, tlen, width), bf16),
        grid=(bsz, tlen // tq),
        in_specs=in_specs,
        out_specs=pl.BlockSpec((1, tq, width), lambda b, i: (b, i, 0)),
        scratch_shapes=[pltpu.VMEM((SWA_REP, BLK, 2 * BLK), f32)]
        + ([pltpu.VMEM((SWA_REP, 2 * BLK, 2 * BLK), f32)] if has_band else []),
        compiler_params=pltpu.CompilerParams(
            dimension_semantics=("parallel", "parallel"), vmem_limit_bytes=VMEM_LIMIT),
    )(*args)


def _out_ffn_kernel(h_ref, ya_ref, yb_ref, yc_ref, woa_ref, wob_ref, woc_ref, g_ref, wg_ref, wu_ref, wd_ref,
                    fg_ref, o_ref, *, final):
    h1 = (h_ref[0] + _dot(ya_ref[0], woa_ref[...]) + _dot(yb_ref[0], wob_ref[...])
          + _dot(yc_ref[0], woc_ref[...]))
    hn = _rms(h1, g_ref[...]).astype(bf16)
    gate = _dot(hn, wg_ref[...])
    up = _dot(hn, wu_ref[...])
    act = ((gate * jax.nn.sigmoid(gate)) * up).astype(bf16)
    h2 = h1 + _dot(act, wd_ref[...])
    if final:
        h2 = _rms(h2, fg_ref[...])
    o_ref[0] = h2


def _single(shape):
    nd = len(shape)
    return pl.BlockSpec(shape, lambda *_: (0,) * nd, pipeline_mode=pl.Buffered(1))


def _out_ffn(h, ya, yb, yc, lw, final_g, final, tm):
    bsz, t, _ = h.shape

    def row(w):
        return pl.BlockSpec((1, tm, w), lambda b, i: (b, i, 0))

    in_specs = [
        row(D_MODEL), row(384), row(256), row(384),
        _single((384, D_MODEL)), _single((256, D_MODEL)), _single((384, D_MODEL)),
        _single((1, D_MODEL)),
        _single((D_MODEL, D_FF)), _single((D_MODEL, D_FF)), _single((D_FF, D_MODEL)),
        _single((1, D_MODEL)),
    ]
    return pl.pallas_call(
        functools.partial(_out_ffn_kernel, final=final),
        name="out_ffn_real" if bsz > 1 else "out_ffn_meta",
        out_shape=jax.ShapeDtypeStruct((bsz, t, D_MODEL), f32),
        grid=(bsz, t // tm),
        in_specs=in_specs,
        out_specs=row(D_MODEL),
        compiler_params=pltpu.CompilerParams(
            dimension_semantics=("parallel", "parallel"), vmem_limit_bytes=VMEM_LIMIT),
    )(h, ya, yb, yc, lw["wo_a"], lw["wo_b"], lw["wo_c"], lw["ffn_norm"], lw["w_gate"], lw["w_up"],
      lw["w_down"], final_g)


def _rot_half_cols(w):
    half = MLA_ROPE // 2
    return jnp.concatenate([-w[:, half:], w[:, :half]], axis=1)


def _layer_weights(l, attn_norm, w_in, mla_q_norm, mla_w_qb, mla_kv_norm, mla_w_kvb, w_out, ffn_norm,
                   w_gate, w_up, w_down):
    w = w_in[l]
    o = 0
    parts = {}
    for name, width in (("c_q", 256), ("c_kv", 128), ("k_rope", 32), ("dq", 256), ("dk", 256), ("dv", 256),
                        ("sq", 384), ("sk", 128), ("sv", 128)):
        parts[name] = w[:, o:o + width]
        o += width
    kr = parts["k_rope"]
    ksw = _rot_half_cols(kr)
    z64 = jnp.zeros((D_MODEL, 64), f32)
    swa_order = [g * SWA_REP + r for r in range(SWA_REP) for g in range(SWA_KV_HEADS)]
    sq = jnp.concatenate([parts["sq"][:, h * SWA_HD:(h + 1) * SWA_HD] for h in swa_order], axis=1)
    w_nat = jnp.concatenate([parts["c_q"], parts["c_kv"], z64, kr, kr, z64, ksw, ksw, parts["dk"],
                             parts["sk"]], axis=1)
    assert w_nat.shape == (D_MODEL, D_NAT)
    w_tr = jnp.concatenate([parts["dq"], parts["dv"], sq, parts["sv"]], axis=1).T
    assert w_tr.shape == (D_TR, D_MODEL)

    wq = mla_w_qb[l]
    q_cols = []
    for h in range(MLA_HEADS):
        base = h * (MLA_NOPE + MLA_ROPE)
        rope = wq[:, base + MLA_NOPE:base + MLA_NOPE + MLA_ROPE]
        q_cols += [wq[:, base:base + MLA_NOPE], rope, _rot_half_cols(rope)]
    w_qbt = jnp.concatenate(q_cols, axis=1).T

    wkv = mla_w_kvb[l]
    zk = jnp.zeros((MLA_KV_RANK, LANES - MLA_NOPE), f32)
    k_cols, v_cols = [], []
    for h in range(MLA_HEADS):
        base = h * (MLA_NOPE + MLA_V)
        k_cols += [wkv[:, base:base + MLA_NOPE], zk]
        v_cols.append(wkv[:, base + MLA_NOPE:base + MLA_NOPE + MLA_V])
    w_kb = jnp.concatenate(k_cols, axis=1)
    w_vt = jnp.concatenate(v_cols, axis=1).T

    wo = w_out[l]
    na = MLA_HEADS * MLA_V
    nb = DIFF_HEADS * DIFF_V
    wo_c = wo[na + nb:]
    wo_c = jnp.concatenate([wo_c[h * SWA_HD:(h + 1) * SWA_HD] for h in swa_order], axis=0)
    return {
        "attn_norm": attn_norm[l][None], "w_nat": w_nat.astype(bf16), "w_tr": w_tr.astype(bf16),
        "q_norm": mla_q_norm[l][None], "w_qbt": w_qbt.astype(bf16),
        "kv_norm": mla_kv_norm[l][None], "w_kb": w_kb.astype(bf16), "w_vt": w_vt.astype(bf16),
        "wo_a": wo[:na].astype(bf16), "wo_b": wo[na:na + nb].astype(bf16), "wo_c": wo_c.astype(bf16),
        "ffn_norm": ffn_norm[l][None],
        "w_gate": w_gate[l].astype(bf16), "w_up": w_up[l].astype(bf16), "w_down": w_down[l].astype(bf16),
    }


def _rope_tables(pos):
    inv_freq = ROPE_THETA ** (-jnp.arange(0, MLA_ROPE, 2, dtype=f32) / MLA_ROPE)
    ang = pos.astype(f32)[:, None] * inv_freq[None, :]
    cos, sin = jnp.cos(ang), jnp.sin(ang)
    n = pos.shape[0]
    cc = jnp.concatenate([cos, cos], axis=1)
    ss = jnp.concatenate([sin, sin], axis=1)
    z = jnp.zeros((n, 64), f32)
    return {
        "csqt": (jnp.concatenate([jnp.ones((n, 64), f32), cc, ss], axis=1) * (MLA_SCALE * LOG2E)).T,
        "cka": jnp.concatenate([z, cc, cc], axis=1),
        "ckb": jnp.concatenate([z, ss, ss], axis=1),
    }


def _mixers(proj, meta_proj, tiles, lam_p, sub_g, lam_init, sinks, real):
    mqt, mk, mvt, dqt, dk, dvt, sq, sk, sv = proj
    if real:
        m_mk, m_mvt, m_dk, m_dvt, m_sk, m_sv = (meta_proj[i] for i in (1, 2, 4, 5, 7, 8))
        ya = _mla_attention(mqt, mk, mvt, (m_mk, m_mvt), MLA_TILE)
        yb = _diff_attention(dqt, dk, dvt, (m_dk, m_dvt), tiles, lam_p, sub_g, lam_init, DIFF_TILE)
        yc = _swa_attention(sq, sk, sv, (m_sk, m_sv), tiles, sinks, SWA_QTILE)
    else:
        ya = _mla_attention(mqt, mk, mvt, None, BLK)
        yb = _diff_attention(dqt, dk, dvt, None, tiles, lam_p, sub_g, lam_init, BLK)
        yc = _swa_attention(sq, sk, sv, None, tiles, sinks, BLK)
    return ya, yb, yc


def kernel(x, meta_tokens, rel_bias, attn_norm, w_in, mla_q_norm, mla_w_qb, mla_kv_norm, mla_w_kvb, diff_lambda,
           diff_subln, swa_sinks, w_out, ffn_norm, w_gate, w_up, w_down, final_norm):
    bsz, seq, _ = x.shape
    assert seq % MLA_TILE == 0 and seq % DIFF_TILE == 0 and seq % SWA_QTILE == 0 and seq % ROW_TILE == 0

    h = x
    h_meta = jnp.concatenate([meta_tokens.astype(f32), jnp.zeros((BLK - N_META, D_MODEL), f32)], axis=0)[None]
    tabs = _rope_tables(N_META + jnp.arange(seq))
    tabs_meta = _rope_tables(jnp.minimum(jnp.arange(BLK), N_META - 1))
    tiles = dict(zip(("dd_diag", "dd_prev", "dd_meta", "dd_self", "ds_band", "ds_meta", "ds_self"),
                     _bias_tiles(rel_bias)))
    final_g = final_norm[None]

    for l in range(DEPTH):
        lw = _layer_weights(l, attn_norm, w_in, mla_q_norm, mla_w_qb, mla_kv_norm, mla_w_kvb, w_out, ffn_norm,
                            w_gate, w_up, w_down)
        lam_init = 0.8 - 0.6 * math.exp(-0.3 * l)
        sub_g = jnp.concatenate([diff_subln[l], diff_subln[l]])[None]
        last = l == DEPTH - 1

        meta_proj = _project(h_meta, lw, tabs_meta, BLK)
        proj = _project(h, lw, tabs, ROW_TILE)
        if not last:
            y_meta = _mixers(meta_proj, None, tiles, diff_lambda[l], sub_g, lam_init, swa_sinks[l], real=False)
            h_meta = _out_ffn(h_meta, *y_meta, lw, final_g, False, BLK)
        y = _mixers(proj, meta_proj, tiles, diff_lambda[l], sub_g, lam_init, swa_sinks[l], real=True)
        h = _out_ffn(h, *y, lw, final_g, last, ROW_TILE)
    return h
```

```python
import functools
import math

import numpy as np
import jax
import jax.numpy as jnp
from jax import lax
from jax.experimental import pallas as pl
from jax.experimental.pallas import tpu as pltpu

D_MODEL = 1024
DEPTH = 2
N_META = 16
BLK = 128

MLA_HEADS = 6
MLA_Q_RANK = 256
MLA_KV_RANK = 128
MLA_NOPE = 64
MLA_ROPE = 32
MLA_V = 64
ROPE_THETA = 10000.0

DIFF_HEADS = 4
DIFF_QK = 32
DIFF_V = 64

SWA_HEADS = 6
SWA_KV_HEADS = 2
SWA_REP = SWA_HEADS // SWA_KV_HEADS
SWA_HD = 64
WINDOW = 128

REL_BUCKETS = 32
REL_MAX_DIST = 128
D_FF = 2816
NEG_INF = -1e30
EPS = 1e-6
LOG2E = math.log2(math.e)

MLA_SCALE = (MLA_NOPE + MLA_ROPE) ** -0.5
DIFF_SCALE = DIFF_QK ** -0.5
SWA_SCALE = SWA_HD ** -0.5

LANES = 128
HEAD_V = 64
D_NAT = 1024
D_TR = 1024
VMEM_LIMIT = 56 * 1024 * 1024

MLA_QTILE = 1024
MLA_KTILE = 512
DIFF_TILE = 512
SWA_QTILE = 512
ROW_TILE = 512

f32 = jnp.float32
bf16 = jnp.bfloat16


def _bucket_thresholds():
    n = np.arange(0, 4 * REL_MAX_DIST)
    max_exact = REL_BUCKETS // 2
    nf = np.maximum(n, max_exact).astype(np.float32)
    large = max_exact + (np.log(nf / np.float32(max_exact)) / np.float32(math.log(REL_MAX_DIST / max_exact))
                         * np.float32(REL_BUCKETS - max_exact)).astype(np.int32)
    bucket = np.where(n < max_exact, n, np.minimum(large, REL_BUCKETS - 1))
    assert np.all(np.diff(bucket) >= 0) and bucket[-1] == REL_BUCKETS - 1
    return [int(np.argmax(bucket >= k)) for k in range(1, REL_BUCKETS)]


BUCKET_THR = _bucket_thresholds()
FAR_DIST = BUCKET_THR[-1]
assert FAR_DIST <= BLK - N_META + 1


def _dot(a, b):
    return jnp.dot(a, b, preferred_element_type=f32)


def _dot_nt(a, b):
    return lax.dot_general(a, b, (((1,), (1,)), ((), ())), preferred_element_type=f32)


def _rms(x, g):
    return (x * lax.rsqrt(jnp.mean(x * x, axis=-1, keepdims=True) + EPS)) * g


def _lane(shape):
    return lax.broadcasted_iota(jnp.int32, shape, len(shape) - 1)


def _row(shape):
    return lax.broadcasted_iota(jnp.int32, shape, 0)


def _bias_lookup(n, tab_ref, h):
    acc = jnp.full(n.shape, tab_ref[0, h], f32)
    for k, thr in enumerate(BUCKET_THR, start=1):
        acc = jnp.where(n >= thr, tab_ref[k, h], acc)
    return acc


def _bias_kernel(tab_ref, dd_diag, dd_prev, dd_meta, dd_self, ds_band, ds_meta, ds_self, *, td):
    rc = 32

    def rows_cols(r0, cols):
        a = r0 + lax.broadcasted_iota(jnp.int32, (rc, cols), 0)
        b = lax.broadcasted_iota(jnp.int32, (rc, cols), 1)
        return a, b

    def diff_body(i, carry):
        r0 = pl.multiple_of(i * rc, rc)
        k, q = rows_cols(r0, td)
        for h in range(DIFF_HEADS):
            c = tab_ref[REL_BUCKETS - 1, h]
            dd_diag[h, pl.ds(r0, rc), :] = jnp.where(
                k <= q, (_bias_lookup(jnp.maximum(q - k, 0), tab_ref, h) - c) * LOG2E, NEG_INF)
            dd_prev[h, pl.ds(r0, rc), :] = (_bias_lookup(q + td - k, tab_ref, h) - c) * LOG2E
        return carry

    lax.fori_loop(0, td // rc, diff_body, 0)

    def blk_body(i, carry):
        r0 = pl.multiple_of(i * rc, rc)
        j, q = rows_cols(r0, td)
        j2, q2 = rows_cols(r0, BLK)
        self_ok = j2 <= jnp.minimum(q2, N_META - 1)
        for h in range(DIFF_HEADS):
            c = tab_ref[REL_BUCKETS - 1, h]
            dd_meta[0, h, pl.ds(r0, rc), :] = jnp.where(
                j < N_META, (_bias_lookup(N_META + q - j, tab_ref, h) - c) * LOG2E, NEG_INF)
            dd_meta[1, h, pl.ds(r0, rc), :] = jnp.where(j < N_META, 0.0, NEG_INF)
            dd_self[h, pl.ds(r0, rc), :] = jnp.where(
                self_ok, (_bias_lookup(jnp.maximum(q2 - j2, 0), tab_ref, h) - c) * LOG2E, NEG_INF)
        k, a = j2, q2
        for h in range(SWA_HEADS):
            hb = DIFF_HEADS + h
            near = _bias_lookup(jnp.maximum(a - k, 0), tab_ref, hb) * LOG2E
            cur = jnp.where(k <= a, near, NEG_INF)
            prev = jnp.where(k > a, _bias_lookup(a + BLK - k, tab_ref, hb) * LOG2E, NEG_INF)
            ds_band[0, h, pl.ds(r0, rc), :] = cur
            ds_band[0, h, pl.ds(BLK + r0, rc), :] = jnp.full((rc, BLK), NEG_INF, f32)
            ds_band[1, h, pl.ds(r0, rc), :] = prev
            ds_band[1, h, pl.ds(BLK + r0, rc), :] = cur
            ds_meta[0, h, pl.ds(r0, rc), :] = jnp.where(
                k < N_META, _bias_lookup(N_META + a - k, tab_ref, hb) * LOG2E, NEG_INF)
            far = jnp.full((rc, BLK), tab_ref[REL_BUCKETS - 1, hb], f32) * LOG2E
            ds_meta[1, h, pl.ds(r0, rc), :] = jnp.where(k < N_META, far, NEG_INF)
            ds_self[0, h, pl.ds(r0, rc), :] = jnp.where(self_ok, near, NEG_INF)
        return carry

    lax.fori_loop(0, BLK // rc, blk_body, 0)


def _bias_tiles(rel_bias):
    td = DIFF_TILE
    outs = (
        jax.ShapeDtypeStruct((DIFF_HEADS, td, td), f32),
        jax.ShapeDtypeStruct((DIFF_HEADS, td, td), f32),
        jax.ShapeDtypeStruct((2, DIFF_HEADS, BLK, td), f32),
        jax.ShapeDtypeStruct((DIFF_HEADS, BLK, BLK), f32),
        jax.ShapeDtypeStruct((2, SWA_HEADS, 2 * BLK, BLK), f32),
        jax.ShapeDtypeStruct((2, SWA_HEADS, BLK, LANES), f32),
        jax.ShapeDtypeStruct((1, SWA_HEADS, BLK, LANES), f32),
    )
    return pl.pallas_call(
        functools.partial(_bias_kernel, td=td),
        name="bias_tiles",
        out_shape=outs,
        in_specs=[pl.BlockSpec(memory_space=pltpu.SMEM)],
        compiler_params=pltpu.CompilerParams(vmem_limit_bytes=VMEM_LIMIT),
    )(rel_bias)


def _store_vt(out_ref, vt, heads):
    tm = vt.shape[1]
    ones_blk = jnp.where(_row((HEAD_V, tm)) == 0, 1.0, 0.0).astype(bf16)
    for hd in range(heads):
        out_ref[0, hd * LANES:hd * LANES + HEAD_V, :] = vt[hd * HEAD_V:(hd + 1) * HEAD_V].astype(bf16)
        out_ref[0, hd * LANES + HEAD_V:(hd + 1) * LANES, :] = ones_blk


def _proj_kernel(h_ref, g_ref, wnat_ref, wtr_ref, qn_ref, wqbt_ref, kvn_ref, wkb_ref, wvt_ref,
                 csqt_ref, cka_ref, ckb_ref,
                 mqt_ref, mk_ref, mvt_ref, dqt_ref, dk_ref, dvt_ref, sqt_ref, sk_ref, svt_ref):
    hn = _rms(h_ref[0], g_ref[...]).astype(bf16)
    proj = _dot(hn, wnat_ref[...])
    tr = _dot_nt(wtr_ref[...], hn)
    c_q = proj[:, 0:256]
    c_kv = proj[:, 256:384]
    rope_a = proj[:, 384:512]
    rope_b = proj[:, 512:640]
    dk_ref[0] = proj[:, 640:896].astype(bf16)
    sk_ref[0] = proj[:, 896:1024].astype(bf16)
    dqt_ref[0] = (tr[0:256] * (DIFF_SCALE * LOG2E)).astype(bf16)
    _store_vt(dvt_ref, tr[256:512], DIFF_HEADS)
    sqt_ref[0] = (tr[512:896] * (SWA_SCALE * LOG2E)).astype(bf16)
    svt_ref[0] = tr[896:1024].astype(bf16)

    cqn = _rms(c_q, qn_ref[...]).astype(bf16)
    ckvn = _rms(c_kv, kvn_ref[...]).astype(bf16)
    qt = _dot_nt(wqbt_ref[...], cqn)
    kk = _dot(ckvn, wkb_ref[...])
    vt = _dot_nt(wvt_ref[...], ckvn)
    csqt = csqt_ref[...]
    k_rot = rope_a * cka_ref[...] + rope_b * ckb_ref[...]
    for hd in range(MLA_HEADS):
        sl = slice(hd * LANES, (hd + 1) * LANES)
        mqt_ref[0, sl, :] = (qt[sl] * csqt).astype(bf16)
        mk_ref[0, :, sl] = (kk[:, sl] + k_rot).astype(bf16)
    _store_vt(mvt_ref, vt, MLA_HEADS)


def _const_spec(shape):
    nd = len(shape)
    return pl.BlockSpec(shape, lambda *_: (0,) * nd)


def _project(h, lw, tabs, tm):
    bsz, t, _ = h.shape
    outs = ((768, True), (768, False), (768, True), (256, True), (256, False), (512, True),
            (384, True), (128, False), (128, True))
    out_shape = tuple(jax.ShapeDtypeStruct((bsz, w, t) if tr else (bsz, t, w), bf16) for w, tr in outs)
    out_specs = tuple(pl.BlockSpec((1, w, tm), lambda b, i: (b, 0, i)) if tr
                      else pl.BlockSpec((1, tm, w), lambda b, i: (b, i, 0)) for w, tr in outs)
    tab_spec = pl.BlockSpec((tm, LANES), lambda b, i: (i, 0))
    in_specs = [
        pl.BlockSpec((1, tm, D_MODEL), lambda b, i: (b, i, 0)),
        _const_spec((1, D_MODEL)),
        _const_spec((D_MODEL, D_NAT)),
        _const_spec((D_TR, D_MODEL)),
        _const_spec((1, MLA_Q_RANK)),
        _const_spec((768, MLA_Q_RANK)),
        _const_spec((1, MLA_KV_RANK)),
        _const_spec((MLA_KV_RANK, 768)),
        _const_spec((384, MLA_KV_RANK)),
        pl.BlockSpec((LANES, tm), lambda b, i: (0, i)),
        tab_spec, tab_spec,
    ]
    return pl.pallas_call(
        _proj_kernel,
        name="proj_real" if bsz > 1 else "proj_meta",
        out_shape=out_shape,
        grid=(bsz, t // tm),
        in_specs=in_specs,
        out_specs=out_specs,
        compiler_params=pltpu.CompilerParams(
            dimension_semantics=("parallel", "parallel"), vmem_limit_bytes=VMEM_LIMIT),
    )(h, lw["attn_norm"], lw["w_nat"], lw["w_tr"], lw["q_norm"], lw["w_qbt"], lw["kv_norm"], lw["w_kb"],
      lw["w_vt"], tabs["csqt"], tabs["cka"], tabs["ckb"])


def _flash_update(s, vt, m_ref, acc_ref, idx, cols=slice(None)):
    m_prev = m_ref[idx, :, cols]
    m_new = jnp.maximum(m_prev, jnp.max(s, axis=0, keepdims=True))
    alpha = jnp.exp2(m_prev - m_new)
    p = jnp.exp2(s - m_new).astype(bf16)
    acc_ref[idx, :, cols] = acc_ref[idx, :, cols] * alpha + _dot(vt, p)
    m_ref[idx, :, cols] = m_new


def _init_stats(m_ref, acc_ref):
    m_ref[...] = jnp.full(m_ref.shape, NEG_INF, f32)
    acc_ref[...] = jnp.zeros(acc_ref.shape, f32)


def _normalized(acc_ref, idx, cols):
    return acc_ref[idx, 0:HEAD_V, cols] / acc_ref[idx, HEAD_V:HEAD_V + 1, cols]


def _mla_kernel(*refs, tq, tk, has_meta):
    if has_meta:
        qt_ref, k_ref, vt_ref, mk_ref, mvt_ref, o_ref, m_ref, acc_ref, s_ref = refs
    else:
        qt_ref, k_ref, vt_ref, o_ref, m_ref, acc_ref, s_ref = refs
    qi = pl.program_id(2)
    ratio = tq // tk
    _init_stats(m_ref, acc_ref)

    def head(hh):
        return slice(hh * LANES, (hh + 1) * LANES)

    if has_meta:
        valid = _row((BLK, tq)) < N_META
        for hh in range(2):
            s = _dot(mk_ref[0, :, head(hh)], qt_ref[0, head(hh), :])
            _flash_update(jnp.where(valid, s, NEG_INF), mvt_ref[0, head(hh), :], m_ref, acc_ref, hh)

    def keys(j):
        return pl.ds(pl.multiple_of(j * tk, tk), tk)

    def stage_a(j, hh, cols):
        s_ref[hh, :, cols] = _dot(k_ref[0, keys(j), head(hh)], qt_ref[0, head(hh), cols])

    def stage_b(j, hh, cols, mask):
        s = s_ref[hh, :, cols]
        if mask is not None:
            s = jnp.where(mask, s, NEG_INF)
        _flash_update(s, vt_ref[0, head(hh), keys(j)], m_ref, acc_ref, hh, cols)

    def tile_step(j, cols, mask, next_cols):
        stage_a(j, 1, cols)
        stage_b(j, 0, cols, mask)
        if next_cols is not None:
            stage_a(j + 1, 0, next_cols)
        stage_b(j, 1, cols, mask)

    full = slice(0, tq)
    stage_a(0, 0, full)
    n_plain = qi * ratio

    def body(i, carry):
        tile_step(2 * i, full, None, full)
        tile_step(2 * i + 1, full, None, full)
        return carry

    lax.fori_loop(0, n_plain // 2, body, 0)
    if ratio % 2 == 1:
        @pl.when(n_plain % 2 == 1)
        def _():
            tile_step(n_plain - 1, full, None, full)

    for d in range(ratio):
        width = tq - d * tk
        key = _row((tk, width))
        qry = _lane((tk, width))
        if has_meta:
            mask = key <= qry
        else:
            mask = key <= jnp.minimum(qry, N_META - 1)
        next_cols = slice((d + 1) * tk, tq) if d + 1 < ratio else None
        tile_step(n_plain + d, slice(d * tk, tq), mask, next_cols)

    ot = jnp.concatenate([_normalized(acc_ref, 0, full), _normalized(acc_ref, 1, full)], axis=0)
    o_ref[0] = ot.T.astype(bf16)


def _mla_attention(mqt, mk, mvt, meta_kv, tq, tk):
    bsz, tlen, _ = mk.shape
    has_meta = meta_kv is not None
    npair = MLA_HEADS // 2
    in_specs = [
        pl.BlockSpec((1, 2 * LANES, tq), lambda b, p, i: (b, p, i)),
        pl.BlockSpec((1, tlen, 2 * LANES), lambda b, p, i: (b, 0, p)),
        pl.BlockSpec((1, 2 * LANES, tlen), lambda b, p, i: (b, p, 0)),
    ]
    args = [mqt, mk, mvt]
    if has_meta:
        in_specs += [
            pl.BlockSpec((1, BLK, 2 * LANES), lambda b, p, i: (0, 0, p)),
            pl.BlockSpec((1, 2 * LANES, BLK), lambda b, p, i: (0, p, 0)),
        ]
        args += list(meta_kv)
    return pl.pallas_call(
        functools.partial(_mla_kernel, tq=tq, tk=tk, has_meta=has_meta),
        name="mla_real" if has_meta else "mla_meta",
        out_shape=jax.ShapeDtypeStruct((bsz, tlen, npair * LANES), bf16),
        grid=(bsz, npair, tlen // tq),
        in_specs=in_specs,
        out_specs=pl.BlockSpec((1, tq, LANES), lambda b, p, i: (b, i, p)),
        scratch_shapes=[
            pltpu.VMEM((2, 1, tq), f32),
            pltpu.VMEM((2, LANES, tq), f32),
            pltpu.VMEM((2, tk, tq), f32),
        ],
        compiler_params=pltpu.CompilerParams(
            dimension_semantics=("parallel", "parallel", "arbitrary"), vmem_limit_bytes=VMEM_LIMIT),
    )(*args)


def _diff_kernel(*refs, t, has_meta, lam_init):
    if has_meta:
        (qt_ref, k_ref, vt_ref, mk_ref, mvt_ref, ddiag_ref, dprev_ref, dmeta_ref, lam_ref, subg_ref,
         o_ref, qs_ref, m_ref, acc_ref, s_ref) = refs
    else:
        (qt_ref, k_ref, vt_ref, ddiag_ref, lam_ref, subg_ref, o_ref, qs_ref, m_ref, acc_ref, s_ref) = refs
    qi = pl.program_id(2)
    _init_stats(m_ref, acc_ref)

    qt = qt_ref[0]
    group = _row((LANES, t)) // DIFF_QK
    for hh in range(2):
        for c in range(2):
            qs_ref[hh, :, c * t:(c + 1) * t] = jnp.where(group == 2 * hh + c, qt, jnp.zeros_like(qt))

    def head(hh):
        return slice(hh * LANES, (hh + 1) * LANES)

    def stage_a(j, hh):
        s_ref[hh] = _dot(k_ref[0, pl.ds(pl.multiple_of(j * t, t), t), :], qs_ref[hh])

    def stage_b(j, hh, bias):
        s = s_ref[hh]
        if bias is not None:
            s = s + jnp.concatenate([bias, bias], axis=1)
        _flash_update(s, vt_ref[0, head(hh), pl.ds(pl.multiple_of(j * t, t), t)], m_ref, acc_ref, hh)

    def tile_step(j, bias_ref, prefetch):
        stage_a(j, 1)
        stage_b(j, 0, None if bias_ref is None else bias_ref[0])
        if prefetch:
            stage_a(j + 1, 0)
        stage_b(j, 1, None if bias_ref is None else bias_ref[1])

    if has_meta:
        for hh in range(2):
            s = _dot(mk_ref[0], qs_ref[hh]) + jnp.concatenate([dmeta_ref[0, hh]] * 2, axis=1)
            _flash_update(s, mvt_ref[0, head(hh), :], m_ref, acc_ref, hh)

    stage_a(0, 0)
    if has_meta:
        n_plain = jnp.maximum(qi - 1, 0)

        def body(i, carry):
            tile_step(2 * i, None, True)
            tile_step(2 * i + 1, None, True)
            return carry

        lax.fori_loop(0, n_plain // 2, body, 0)

        @pl.when(n_plain % 2 == 1)
        def _():
            tile_step(n_plain - 1, None, True)

        @pl.when(qi >= 1)
        def _():
            tile_step(qi - 1, dprev_ref, True)

    tile_step(qi, ddiag_ref, False)

    lp = lam_ref[...]
    lam = (jnp.exp(jnp.sum(lp[0:1] * lp[1:2], axis=-1, keepdims=True))
           - jnp.exp(jnp.sum(lp[2:3] * lp[3:4], axis=-1, keepdims=True)) + lam_init)
    outs = []
    for hh in range(2):
        o = _normalized(acc_ref, hh, slice(0, t)) - lam * _normalized(acc_ref, hh, slice(t, 2 * t))
        ms = jnp.mean(o * o, axis=0, keepdims=True)
        outs.append(o * lax.rsqrt(ms + EPS))
    o_nat = jnp.concatenate(outs, axis=0).T
    o_ref[0] = ((o_nat * subg_ref[...]) * (1.0 - lam_init)).astype(bf16)


def _diff_attention(dqt, dk, dvt, meta_kv, tiles, lam_p, sub_g, lam_init, t):
    bsz, tlen, _ = dk.shape
    has_meta = meta_kv is not None
    npair = DIFF_HEADS // 2
    in_specs = [
        pl.BlockSpec((1, LANES, t), lambda b, p, i: (b, p, i)),
        pl.BlockSpec((1, tlen, LANES), lambda b, p, i: (b, 0, p)),
        pl.BlockSpec((1, 2 * LANES, tlen), lambda b, p, i: (b, p, 0)),
    ]
    args = [dqt, dk, dvt]
    if has_meta:
        in_specs += [
            pl.BlockSpec((1, BLK, LANES), lambda b, p, i: (0, 0, p)),
            pl.BlockSpec((1, 2 * LANES, BLK), lambda b, p, i: (0, p, 0)),
            pl.BlockSpec((2, t, t), lambda b, p, i: (p, 0, 0)),
            pl.BlockSpec((2, t, t), lambda b, p, i: (p, 0, 0)),
            pl.BlockSpec((1, 2, BLK, t), lambda b, p, i: (jnp.minimum(i, 1), p, 0, 0)),
        ]
        args += list(meta_kv) + [tiles["dd_diag"], tiles["dd_prev"], tiles["dd_meta"]]
    else:
        in_specs += [pl.BlockSpec((2, t, t), lambda b, p, i: (p, 0, 0))]
        args += [tiles["dd_self"]]
    in_specs += [_const_spec((4, DIFF_QK)), _const_spec((1, LANES))]
    args += [lam_p, sub_g]
    return pl.pallas_call(
        functools.partial(_diff_kernel, t=t, has_meta=has_meta, lam_init=lam_init),
        name="diff_real" if has_meta else "diff_meta",
        out_shape=jax.ShapeDtypeStruct((bsz, tlen, npair * LANES), bf16),
        grid=(bsz, npair, tlen // t),
        in_specs=in_specs,
        out_specs=pl.BlockSpec((1, t, LANES), lambda b, p, i: (b, i, p)),
        scratch_shapes=[
            pltpu.VMEM((2, LANES, 2 * t), bf16),
            pltpu.VMEM((2, 1, 2 * t), f32),
            pltpu.VMEM((2, LANES, 2 * t), f32),
            pltpu.VMEM((2, t, 2 * t), f32),
        ],
        compiler_params=pltpu.CompilerParams(
            dimension_semantics=("parallel", "parallel", "arbitrary"), vmem_limit_bytes=VMEM_LIMIT),
    )(*args)


def _swa_kernel(*refs, tq, has_band):
    if has_band:
        qt_ref, k_ref, vt_ref, mk_ref, mvt_ref, dband_ref, dmeta_ref, sink_ref, o_ref, sm_ref, sb_ref = refs
    else:
        qt_ref, mk_ref, mvt_ref, dmeta_ref, sink_ref, o_ref, sm_ref = refs
    qi = pl.program_id(1)
    feat = _row((LANES, BLK))
    first_head = _lane((1, 2 * BLK)) < BLK
    n_meta_var = dmeta_ref.shape[0]

    def block(sub):
        blk = qi * (tq // BLK) + sub
        c0 = sub * BLK
        mvar = jnp.minimum(blk, n_meta_var - 1)
        if has_band:
            bvar = jnp.minimum(blk, 1)
            koff = pl.multiple_of(jnp.maximum(blk - 1, 0) * BLK, BLK)
            kband = k_ref[0, pl.ds(koff, 2 * BLK), :]
            vtband = vt_ref[0, :, pl.ds(koff, 2 * BLK)]

        def stage_a(r):
            q2t = qt_ref[0, r * LANES:(r + 1) * LANES, c0:c0 + BLK]
            zero = jnp.zeros_like(q2t)
            qs = jnp.concatenate([jnp.where(feat < SWA_HD, q2t, zero), jnp.where(feat >= SWA_HD, q2t, zero)],
                                 axis=1)
            sm_ref[r] = _dot(mk_ref[0], qs)
            if has_band:
                sb_ref[r] = _dot(kband, qs)

        def stage_b(r):
            h0, h1 = r, SWA_REP + r
            sink = jnp.where(first_head, sink_ref[h0], sink_ref[h1]) * LOG2E
            s_meta = sm_ref[r] + jnp.concatenate([dmeta_ref[mvar, h0], dmeta_ref[mvar, h1]], axis=1)
            m = jnp.maximum(jnp.max(s_meta, axis=0, keepdims=True), sink)
            if has_band:
                s_band = sb_ref[r] + jnp.concatenate([dband_ref[bvar, h0], dband_ref[bvar, h1]], axis=1)
                m = jnp.maximum(m, jnp.max(s_band, axis=0, keepdims=True))
            p_meta = jnp.exp2(s_meta - m)
            denom = jnp.sum(p_meta, axis=0, keepdims=True) + jnp.exp2(sink - m)
            acc = _dot(mvt_ref[0], p_meta.astype(bf16))
            if has_band:
                p_band = jnp.exp2(s_band - m)
                denom = denom + jnp.sum(p_band, axis=0, keepdims=True)
                acc = acc + _dot(vtband, p_band.astype(bf16))
            ot = acc / denom
            pair = jnp.where(feat < SWA_HD, ot[:, 0:BLK], ot[:, BLK:2 * BLK])
            o_ref[0, c0:c0 + BLK, r * LANES:(r + 1) * LANES] = pair.T.astype(bf16)

        return stage_a, stage_b

    stages = [block(sub) for sub in range(tq // BLK)]
    chains = [(sub, r) for sub in range(tq // BLK) for r in range(SWA_REP)]
    stages[0][0](0)
    for i, (sub, r) in enumerate(chains):
        if i + 1 < len(chains):
            nsub, nr = chains[i + 1]
            stages[nsub][0](nr)
        stages[sub][1](r)


def _swa_attention(sqt, sk, svt, meta_kv, tiles, sinks, tq):
    bsz, width, tlen = sqt.shape
    has_band = meta_kv is not None
    smem = pl.BlockSpec(memory_space=pltpu.SMEM)
    if has_band:
        in_specs = [
            pl.BlockSpec((1, width, tq), lambda b, i: (b, 0, i)),
            pl.BlockSpec((1, tlen, LANES), lambda b, i: (b, 0, 0)),
            pl.BlockSpec((1, LANES, tlen), lambda b, i: (b, 0, 0)),
            _const_spec((1, BLK, LANES)),
            _const_spec((1, LANES, BLK)),
            _const_spec(tiles["ds_band"].shape),
            _const_spec(tiles["ds_meta"].shape),
            smem,
        ]
        args = [sqt, sk, svt, meta_kv[0], meta_kv[1], tiles["ds_band"], tiles["ds_meta"], sinks]
    else:
        in_specs = [
            pl.BlockSpec((1, width, tq), lambda b, i: (b, 0, i)),
            _const_spec((1, BLK, LANES)),
            _const_spec((1, LANES, BLK)),
            _const_spec(tiles["ds_self"].shape),
            smem,
        ]
        args = [sqt, sk, svt, tiles["ds_self"], sinks]
    return pl.pallas_call(
        functools.partial(_swa_kernel, tq=tq, has_band=has_band),
        name="swa_real" if has_band else "swa_meta",
        out_shape=jax.ShapeDtypeStruct((bsz, tlen, width), bf16),
        grid=(bsz, tlen // tq),
        in_specs=in_specs,
        out_specs=pl.BlockSpec((1, tq, width), lambda b, i: (b, i, 0)),
        scratch_shapes=[pltpu.VMEM((SWA_REP, BLK, 2 * BLK), f32)]
        + ([pltpu.VMEM((SWA_REP, 2 * BLK, 2 * BLK), f32)] if has_band else []),
        compiler_params=pltpu.CompilerParams(
            dimension_semantics=("parallel", "parallel"), vmem_limit_bytes=VMEM_LIMIT),
    )(*args)


def _out_ffn_kernel(h_ref, ya_ref, yb_ref, yc_ref, woa_ref, wob_ref, woc_ref, g_ref, wg_ref, wu_ref, wd_ref,
                    fg_ref, o_ref, *, final):
    h1 = (h_ref[0] + _dot(ya_ref[0], woa_ref[...]) + _dot(yb_ref[0], wob_ref[...])
          + _dot(yc_ref[0], woc_ref[...]))
    hn = _rms(h1, g_ref[...]).astype(bf16)
    gate = _dot(hn, wg_ref[...])
    up = _dot(hn, wu_ref[...])
    act = ((gate * jax.nn.sigmoid(gate)) * up).astype(bf16)
    h2 = h1 + _dot(act, wd_ref[...])
    if final:
        h2 = _rms(h2, fg_ref[...])
    o_ref[0] = h2


def _single(shape):
    nd = len(shape)
    return pl.BlockSpec(shape, lambda *_: (0,) * nd, pipeline_mode=pl.Buffered(1))


def _out_ffn(h, ya, yb, yc, lw, final_g, final, tm):
    bsz, t, _ = h.shape

    def row(w):
        return pl.BlockSpec((1, tm, w), lambda b, i: (b, i, 0))

    in_specs = [
        row(D_MODEL), row(384), row(256), row(384),
        _single((384, D_MODEL)), _single((256, D_MODEL)), _single((384, D_MODEL)),
        _single((1, D_MODEL)),
        _single((D_MODEL, D_FF)), _single((D_MODEL, D_FF)), _single((D_FF, D_MODEL)),
        _single((1, D_MODEL)),
    ]
    return pl.pallas_call(
        functools.partial(_out_ffn_kernel, final=final),
        name="out_ffn_real" if bsz > 1 else "out_ffn_meta",
        out_shape=jax.ShapeDtypeStruct((bsz, t, D_MODEL), f32),
        grid=(bsz, t // tm),
        in_specs=in_specs,
        out_specs=row(D_MODEL),
        compiler_params=pltpu.CompilerParams(
            dimension_semantics=("parallel", "parallel"), vmem_limit_bytes=VMEM_LIMIT),
    )(h, ya, yb, yc, lw["wo_a"], lw["wo_b"], lw["wo_c"], lw["ffn_norm"], lw["w_gate"], lw["w_up"],
      lw["w_down"], final_g)


def _rot_half_cols(w):
    half = MLA_ROPE // 2
    return jnp.concatenate([-w[:, half:], w[:, :half]], axis=1)


def _layer_weights(l, attn_norm, w_in, mla_q_norm, mla_w_qb, mla_kv_norm, mla_w_kvb, w_out, ffn_norm,
                   w_gate, w_up, w_down):
    w = w_in[l]
    o = 0
    parts = {}
    for name, width in (("c_q", 256), ("c_kv", 128), ("k_rope", 32), ("dq", 256), ("dk", 256), ("dv", 256),
                        ("sq", 384), ("sk", 128), ("sv", 128)):
        parts[name] = w[:, o:o + width]
        o += width
    kr = parts["k_rope"]
    ksw = _rot_half_cols(kr)
    z64 = jnp.zeros((D_MODEL, 64), f32)
    swa_order = [g * SWA_REP + r for r in range(SWA_REP) for g in range(SWA_KV_HEADS)]
    sq = jnp.concatenate([parts["sq"][:, h * SWA_HD:(h + 1) * SWA_HD] for h in swa_order], axis=1)
    w_nat = jnp.concatenate([parts["c_q"], parts["c_kv"], z64, kr, kr, z64, ksw, ksw, parts["dk"],
                             parts["sk"]], axis=1)
    assert w_nat.shape == (D_MODEL, D_NAT)
    w_tr = jnp.concatenate([parts["dq"], parts["dv"], sq, parts["sv"]], axis=1).T
    assert w_tr.shape == (D_TR, D_MODEL)

    wq = mla_w_qb[l]
    q_cols = []
    for h in range(MLA_HEADS):
        base = h * (MLA_NOPE + MLA_ROPE)
        rope = wq[:, base + MLA_NOPE:base + MLA_NOPE + MLA_ROPE]
        q_cols += [wq[:, base:base + MLA_NOPE], rope, _rot_half_cols(rope)]
    w_qbt = jnp.concatenate(q_cols, axis=1).T

    wkv = mla_w_kvb[l]
    zk = jnp.zeros((MLA_KV_RANK, LANES - MLA_NOPE), f32)
    k_cols, v_cols = [], []
    for h in range(MLA_HEADS):
        base = h * (MLA_NOPE + MLA_V)
        k_cols += [wkv[:, base:base + MLA_NOPE], zk]
        v_cols.append(wkv[:, base + MLA_NOPE:base + MLA_NOPE + MLA_V])
    w_kb = jnp.concatenate(k_cols, axis=1)
    w_vt = jnp.concatenate(v_cols, axis=1).T

    wo = w_out[l]
    na = MLA_HEADS * MLA_V
    nb = DIFF_HEADS * DIFF_V
    wo_c = wo[na + nb:]
    wo_c = jnp.concatenate([wo_c[h * SWA_HD:(h + 1) * SWA_HD] for h in swa_order], axis=0)
    return {
        "attn_norm": attn_norm[l][None], "w_nat": w_nat.astype(bf16), "w_tr": w_tr.astype(bf16),
        "q_norm": mla_q_norm[l][None], "w_qbt": w_qbt.astype(bf16),
        "kv_norm": mla_kv_norm[l][None], "w_kb": w_kb.astype(bf16), "w_vt": w_vt.astype(bf16),
        "wo_a": wo[:na].astype(bf16), "wo_b": wo[na:na + nb].astype(bf16), "wo_c": wo_c.astype(bf16),
        "ffn_norm": ffn_norm[l][None],
        "w_gate": w_gate[l].astype(bf16), "w_up": w_up[l].astype(bf16), "w_down": w_down[l].astype(bf16),
    }


def _rope_tables(pos):
    inv_freq = ROPE_THETA ** (-jnp.arange(0, MLA_ROPE, 2, dtype=f32) / MLA_ROPE)
    ang = pos.astype(f32)[:, None] * inv_freq[None, :]
    cos, sin = jnp.cos(ang), jnp.sin(ang)
    n = pos.shape[0]
    cc = jnp.concatenate([cos, cos], axis=1)
    ss = jnp.concatenate([sin, sin], axis=1)
    z = jnp.zeros((n, 64), f32)
    return {
        "csqt": (jnp.concatenate([jnp.ones((n, 64), f32), cc, ss], axis=1) * (MLA_SCALE * LOG2E)).T,
        "cka": jnp.concatenate([z, cc, cc], axis=1),
        "ckb": jnp.concatenate([z, ss, ss], axis=1),
    }


def _mixers(proj, meta_proj, tiles, lam_p, sub_g, lam_init, sinks, real):
    mqt, mk, mvt, dqt, dk, dvt, sq, sk, sv = proj
    if real:
        m_mk, m_mvt, m_dk, m_dvt, m_sk, m_sv = (meta_proj[i] for i in (1, 2, 4, 5, 7, 8))
        ya = _mla_attention(mqt, mk, mvt, (m_mk, m_mvt), MLA_QTILE, MLA_KTILE)
        yb = _diff_attention(dqt, dk, dvt, (m_dk, m_dvt), tiles, lam_p, sub_g, lam_init, DIFF_TILE)
        yc = _swa_attention(sq, sk, sv, (m_sk, m_sv), tiles, sinks, SWA_QTILE)
    else:
        ya = _mla_attention(mqt, mk, mvt, None, BLK, BLK)
        yb = _diff_attention(dqt, dk, dvt, None, tiles, lam_p, sub_g, lam_init, BLK)
        yc = _swa_attention(sq, sk, sv, None, tiles, sinks, BLK)
    return ya, yb, yc


def kernel(x, meta_tokens, rel_bias, attn_norm, w_in, mla_q_norm, mla_w_qb, mla_kv_norm, mla_w_kvb, diff_lambda,
           diff_subln, swa_sinks, w_out, ffn_norm, w_gate, w_up, w_down, final_norm):
    bsz, seq, _ = x.shape
    assert seq % MLA_QTILE == 0 and MLA_QTILE % MLA_KTILE == 0 and seq % DIFF_TILE == 0 and seq % SWA_QTILE == 0 and seq % ROW_TILE == 0

    h = x
    h_meta = jnp.concatenate([meta_tokens.astype(f32), jnp.zeros((BLK - N_META, D_MODEL), f32)], axis=0)[None]
    tabs = _rope_tables(N_META + jnp.arange(seq))
    tabs_meta = _rope_tables(jnp.minimum(jnp.arange(BLK), N_META - 1))
    tiles = dict(zip(("dd_diag", "dd_prev", "dd_meta", "dd_self", "ds_band", "ds_meta", "ds_self"),
                     _bias_tiles(rel_bias)))
    final_g = final_norm[None]

    for l in range(DEPTH):
        lw = _layer_weights(l, attn_norm, w_in, mla_q_norm, mla_w_qb, mla_kv_norm, mla_w_kvb, w_out, ffn_norm,
                            w_gate, w_up, w_down)
        lam_init = 0.8 - 0.6 * math.exp(-0.3 * l)
        sub_g = jnp.concatenate([diff_subln[l], diff_subln[l]])[None]
        last = l == DEPTH - 1

        meta_proj = _project(h_meta, lw, tabs_meta, BLK)
        proj = _project(h, lw, tabs, ROW_TILE)
        if not last:
            y_meta = _mixers(meta_proj, None, tiles, diff_lambda[l], sub_g, lam_init, swa_sinks[l], real=False)
            h_meta = _out_ffn(h_meta, *y_meta, lw, final_g, False, BLK)
        y = _mixers(proj, meta_proj, tiles, diff_lambda[l], sub_g, lam_init, swa_sinks[l], real=True)
        h = _out_ffn(h, *y, lw, final_g, last, ROW_TILE)
    return h
```

```python
import functools
import math

import numpy as np
import jax
import jax.numpy as jnp
from jax import lax
from jax.experimental import pallas as pl
from jax.experimental.pallas import tpu as pltpu

D_MODEL = 1024
DEPTH = 2
N_META = 16
BLK = 128

MLA_HEADS = 6
MLA_Q_RANK = 256
MLA_KV_RANK = 128
MLA_NOPE = 64
MLA_ROPE = 32
MLA_V = 64
ROPE_THETA = 10000.0

DIFF_HEADS = 4
DIFF_QK = 32
DIFF_V = 64

SWA_HEADS = 6
SWA_KV_HEADS = 2
SWA_REP = SWA_HEADS // SWA_KV_HEADS
SWA_HD = 64
WINDOW = 128

REL_BUCKETS = 32
REL_MAX_DIST = 128
D_FF = 2816
NEG_INF = -1e30
EPS = 1e-6
LOG2E = math.log2(math.e)

MLA_SCALE = (MLA_NOPE + MLA_ROPE) ** -0.5
DIFF_SCALE = DIFF_QK ** -0.5
SWA_SCALE = SWA_HD ** -0.5

LANES = 128
HEAD_V = 64
D_NAT = 1024
D_TR = 1024
VMEM_LIMIT = 56 * 1024 * 1024

MLA_QTILE = 2048
MLA_KTILE = 512
DIFF_QTILE = 1024
DIFF_TILE = 512
SWA_QTILE = 512
ROW_TILE = 512

f32 = jnp.float32
bf16 = jnp.bfloat16


def _bucket_thresholds():
    n = np.arange(0, 4 * REL_MAX_DIST)
    max_exact = REL_BUCKETS // 2
    nf = np.maximum(n, max_exact).astype(np.float32)
    large = max_exact + (np.log(nf / np.float32(max_exact)) / np.float32(math.log(REL_MAX_DIST / max_exact))
                         * np.float32(REL_BUCKETS - max_exact)).astype(np.int32)
    bucket = np.where(n < max_exact, n, np.minimum(large, REL_BUCKETS - 1))
    assert np.all(np.diff(bucket) >= 0) and bucket[-1] == REL_BUCKETS - 1
    return [int(np.argmax(bucket >= k)) for k in range(1, REL_BUCKETS)]


BUCKET_THR = _bucket_thresholds()
FAR_DIST = BUCKET_THR[-1]
assert FAR_DIST <= BLK - N_META + 1


def _dot(a, b):
    return jnp.dot(a, b, preferred_element_type=f32)


def _dot_nt(a, b):
    return lax.dot_general(a, b, (((1,), (1,)), ((), ())), preferred_element_type=f32)


def _rms(x, g):
    return (x * lax.rsqrt(jnp.mean(x * x, axis=-1, keepdims=True) + EPS)) * g


def _lane(shape):
    return lax.broadcasted_iota(jnp.int32, shape, len(shape) - 1)


def _row(shape):
    return lax.broadcasted_iota(jnp.int32, shape, 0)


def _bias_lookup(n, tab_ref, h):
    acc = jnp.full(n.shape, tab_ref[0, h], f32)
    for k, thr in enumerate(BUCKET_THR, start=1):
        acc = jnp.where(n >= thr, tab_ref[k, h], acc)
    return acc


def _bias_kernel(tab_ref, dd_diag, dd_prev, dd_meta, dd_self, ds_band, ds_meta, ds_self, *, td):
    rc = 32

    def rows_cols(r0, cols):
        a = r0 + lax.broadcasted_iota(jnp.int32, (rc, cols), 0)
        b = lax.broadcasted_iota(jnp.int32, (rc, cols), 1)
        return a, b

    def diff_body(i, carry):
        r0 = pl.multiple_of(i * rc, rc)
        k, q = rows_cols(r0, td)
        for h in range(DIFF_HEADS):
            c = tab_ref[REL_BUCKETS - 1, h]
            dd_diag[h, pl.ds(r0, rc), :] = jnp.where(
                k <= q, (_bias_lookup(jnp.maximum(q - k, 0), tab_ref, h) - c) * LOG2E, NEG_INF)
            dd_prev[h, pl.ds(r0, rc), :] = (_bias_lookup(q + td - k, tab_ref, h) - c) * LOG2E
        return carry

    lax.fori_loop(0, td // rc, diff_body, 0)

    def blk_body(i, carry):
        r0 = pl.multiple_of(i * rc, rc)
        j, q = rows_cols(r0, td)
        j2, q2 = rows_cols(r0, BLK)
        self_ok = j2 <= jnp.minimum(q2, N_META - 1)
        for h in range(DIFF_HEADS):
            c = tab_ref[REL_BUCKETS - 1, h]
            dd_meta[0, h, pl.ds(r0, rc), :] = jnp.where(
                j < N_META, (_bias_lookup(N_META + q - j, tab_ref, h) - c) * LOG2E, NEG_INF)
            dd_meta[1, h, pl.ds(r0, rc), :] = jnp.where(j < N_META, 0.0, NEG_INF)
            dd_self[h, pl.ds(r0, rc), :] = jnp.where(
                self_ok, (_bias_lookup(jnp.maximum(q2 - j2, 0), tab_ref, h) - c) * LOG2E, NEG_INF)
        k, a = j2, q2
        for h in range(SWA_HEADS):
            hb = DIFF_HEADS + h
            near = _bias_lookup(jnp.maximum(a - k, 0), tab_ref, hb) * LOG2E
            cur = jnp.where(k <= a, near, NEG_INF)
            prev = jnp.where(k > a, _bias_lookup(a + BLK - k, tab_ref, hb) * LOG2E, NEG_INF)
            ds_band[0, h, pl.ds(r0, rc), :] = cur
            ds_band[0, h, pl.ds(BLK + r0, rc), :] = jnp.full((rc, BLK), NEG_INF, f32)
            ds_band[1, h, pl.ds(r0, rc), :] = prev
            ds_band[1, h, pl.ds(BLK + r0, rc), :] = cur
            ds_meta[0, h, pl.ds(r0, rc), :] = jnp.where(
                k < N_META, _bias_lookup(N_META + a - k, tab_ref, hb) * LOG2E, NEG_INF)
            far = jnp.full((rc, BLK), tab_ref[REL_BUCKETS - 1, hb], f32) * LOG2E
            ds_meta[1, h, pl.ds(r0, rc), :] = jnp.where(k < N_META, far, NEG_INF)
            ds_self[0, h, pl.ds(r0, rc), :] = jnp.where(self_ok, near, NEG_INF)
        return carry

    lax.fori_loop(0, BLK // rc, blk_body, 0)


def _bias_tiles(rel_bias):
    td = DIFF_TILE
    outs = (
        jax.ShapeDtypeStruct((DIFF_HEADS, td, td), f32),
        jax.ShapeDtypeStruct((DIFF_HEADS, td, td), f32),
        jax.ShapeDtypeStruct((2, DIFF_HEADS, BLK, td), f32),
        jax.ShapeDtypeStruct((DIFF_HEADS, BLK, BLK), f32),
        jax.ShapeDtypeStruct((2, SWA_HEADS, 2 * BLK, BLK), f32),
        jax.ShapeDtypeStruct((2, SWA_HEADS, BLK, LANES), f32),
        jax.ShapeDtypeStruct((1, SWA_HEADS, BLK, LANES), f32),
    )
    return pl.pallas_call(
        functools.partial(_bias_kernel, td=td),
        name="bias_tiles",
        out_shape=outs,
        in_specs=[pl.BlockSpec(memory_space=pltpu.SMEM)],
        compiler_params=pltpu.CompilerParams(vmem_limit_bytes=VMEM_LIMIT),
    )(rel_bias)


def _store_vt(out_ref, vt, heads):
    tm = vt.shape[1]
    ones_blk = jnp.where(_row((HEAD_V, tm)) == 0, 1.0, 0.0).astype(bf16)
    for hd in range(heads):
        out_ref[0, hd * LANES:hd * LANES + HEAD_V, :] = vt[hd * HEAD_V:(hd + 1) * HEAD_V].astype(bf16)
        out_ref[0, hd * LANES + HEAD_V:(hd + 1) * LANES, :] = ones_blk


def _proj_kernel(h_ref, g_ref, wnat_ref, wtr_ref, qn_ref, wqbt_ref, kvn_ref, wkb_ref, wvt_ref,
                 csqt_ref, cka_ref, ckb_ref,
                 mqt_ref, mk_ref, mvt_ref, dqt_ref, dk_ref, dvt_ref, sqt_ref, sk_ref, svt_ref):
    hn = _rms(h_ref[0], g_ref[...]).astype(bf16)
    proj = _dot(hn, wnat_ref[...])
    tr = _dot_nt(wtr_ref[...], hn)
    c_q = proj[:, 0:256]
    c_kv = proj[:, 256:384]
    rope_a = proj[:, 384:512]
    rope_b = proj[:, 512:640]
    dk_ref[0] = proj[:, 640:896].astype(bf16)
    sk_ref[0] = proj[:, 896:1024].astype(bf16)
    dqt_ref[0] = (tr[0:256] * (DIFF_SCALE * LOG2E)).astype(bf16)
    _store_vt(dvt_ref, tr[256:512], DIFF_HEADS)
    sqt_ref[0] = (tr[512:896] * (SWA_SCALE * LOG2E)).astype(bf16)
    svt_ref[0] = tr[896:1024].astype(bf16)

    cqn = _rms(c_q, qn_ref[...]).astype(bf16)
    ckvn = _rms(c_kv, kvn_ref[...]).astype(bf16)
    qt = _dot_nt(wqbt_ref[...], cqn)
    kk = _dot(ckvn, wkb_ref[...])
    vt = _dot_nt(wvt_ref[...], ckvn)
    csqt = csqt_ref[...]
    k_rot = rope_a * cka_ref[...] + rope_b * ckb_ref[...]
    for hd in range(MLA_HEADS):
        sl = slice(hd * LANES, (hd + 1) * LANES)
        mqt_ref[0, sl, :] = (qt[sl] * csqt).astype(bf16)
        mk_ref[0, :, sl] = (kk[:, sl] + k_rot).astype(bf16)
    _store_vt(mvt_ref, vt, MLA_HEADS)


def _const_spec(shape):
    nd = len(shape)
    return pl.BlockSpec(shape, lambda *_: (0,) * nd)


def _project(h, lw, tabs, tm):
    bsz, t, _ = h.shape
    outs = ((768, True), (768, False), (768, True), (256, True), (256, False), (512, True),
            (384, True), (128, False), (128, True))
    out_shape = tuple(jax.ShapeDtypeStruct((bsz, w, t) if tr else (bsz, t, w), bf16) for w, tr in outs)
    out_specs = tuple(pl.BlockSpec((1, w, tm), lambda b, i: (b, 0, i)) if tr
                      else pl.BlockSpec((1, tm, w), lambda b, i: (b, i, 0)) for w, tr in outs)
    tab_spec = pl.BlockSpec((tm, LANES), lambda b, i: (i, 0))
    in_specs = [
        pl.BlockSpec((1, tm, D_MODEL), lambda b, i: (b, i, 0)),
        _const_spec((1, D_MODEL)),
        _const_spec((D_MODEL, D_NAT)),
        _const_spec((D_TR, D_MODEL)),
        _const_spec((1, MLA_Q_RANK)),
        _const_spec((768, MLA_Q_RANK)),
        _const_spec((1, MLA_KV_RANK)),
        _const_spec((MLA_KV_RANK, 768)),
        _const_spec((384, MLA_KV_RANK)),
        pl.BlockSpec((LANES, tm), lambda b, i: (0, i)),
        tab_spec, tab_spec,
    ]
    return pl.pallas_call(
        _proj_kernel,
        name="proj_real" if bsz > 1 else "proj_meta",
        out_shape=out_shape,
        grid=(bsz, t // tm),
        in_specs=in_specs,
        out_specs=out_specs,
        compiler_params=pltpu.CompilerParams(
            dimension_semantics=("parallel", "parallel"), vmem_limit_bytes=VMEM_LIMIT),
    )(h, lw["attn_norm"], lw["w_nat"], lw["w_tr"], lw["q_norm"], lw["w_qbt"], lw["kv_norm"], lw["w_kb"],
      lw["w_vt"], tabs["csqt"], tabs["cka"], tabs["ckb"])


def _flash_update(s, vt, m_ref, acc_ref, idx, cols=slice(None)):
    m_prev = m_ref[idx, :, cols]
    m_new = jnp.maximum(m_prev, jnp.max(s, axis=0, keepdims=True))
    alpha = jnp.exp2(m_prev - m_new)
    p = jnp.exp2(s - m_new).astype(bf16)
    acc_ref[idx, :, cols] = acc_ref[idx, :, cols] * alpha + _dot(vt, p)
    m_ref[idx, :, cols] = m_new


def _init_stats(m_ref, acc_ref):
    m_ref[...] = jnp.full(m_ref.shape, NEG_INF, f32)
    acc_ref[...] = jnp.zeros(acc_ref.shape, f32)


def _normalized(acc_ref, idx, cols):
    return acc_ref[idx, 0:HEAD_V, cols] / acc_ref[idx, HEAD_V:HEAD_V + 1, cols]


def _mla_kernel(*refs, tq, tk, has_meta):
    if has_meta:
        qt_ref, k_ref, vt_ref, mk_ref, mvt_ref, o_ref, m_ref, acc_ref, s_ref = refs
    else:
        qt_ref, k_ref, vt_ref, o_ref, m_ref, acc_ref, s_ref = refs
    qi = pl.program_id(2)
    ratio = tq // tk
    _init_stats(m_ref, acc_ref)

    def head(hh):
        return slice(hh * LANES, (hh + 1) * LANES)

    if has_meta:
        valid = _row((BLK, tq)) < N_META
        for hh in range(2):
            s = _dot(mk_ref[0, :, head(hh)], qt_ref[0, head(hh), :])
            _flash_update(jnp.where(valid, s, NEG_INF), mvt_ref[0, head(hh), :], m_ref, acc_ref, hh)

    def keys(j):
        return pl.ds(pl.multiple_of(j * tk, tk), tk)

    def stage_a(j, hh, cols):
        s_ref[hh, :, cols] = _dot(k_ref[0, keys(j), head(hh)], qt_ref[0, head(hh), cols])

    def stage_b(j, hh, cols, mask):
        s = s_ref[hh, :, cols]
        if mask is not None:
            s = jnp.where(mask, s, NEG_INF)
        _flash_update(s, vt_ref[0, head(hh), keys(j)], m_ref, acc_ref, hh, cols)

    def blocks(cols):
        return [slice(c, c + tk) for c in range(cols.start, cols.stop, tk)]

    def tile_step(j, cols, mask, next_cols):
        chains = [(hh, cb) for cb in blocks(cols) for hh in range(2)]
        for idx, (hh, cb) in enumerate(chains):
            if idx + 1 < len(chains):
                stage_a(j, *chains[idx + 1])
            elif next_cols is not None:
                stage_a(j + 1, 0, blocks(next_cols)[0])
            chain_mask = None if mask is None else mask[:, cb.start - cols.start:cb.stop - cols.start]
            stage_b(j, hh, cb, chain_mask)

    full = slice(0, tq)
    stage_a(0, 0, blocks(full)[0])
    n_plain = qi * ratio

    def body(i, carry):
        tile_step(2 * i, full, None, full)
        tile_step(2 * i + 1, full, None, full)
        return carry

    lax.fori_loop(0, n_plain // 2, body, 0)
    if ratio % 2 == 1:
        @pl.when(n_plain % 2 == 1)
        def _():
            tile_step(n_plain - 1, full, None, full)

    for d in range(ratio):
        width = tq - d * tk
        key = _row((tk, width))
        qry = _lane((tk, width))
        if has_meta:
            mask = key <= qry
        else:
            mask = key <= jnp.minimum(qry, N_META - 1)
        next_cols = slice((d + 1) * tk, tq) if d + 1 < ratio else None
        tile_step(n_plain + d, slice(d * tk, tq), mask, next_cols)

    ot = jnp.concatenate([_normalized(acc_ref, 0, full), _normalized(acc_ref, 1, full)], axis=0)
    o_ref[0] = ot.T.astype(bf16)


def _mla_attention(mqt, mk, mvt, meta_kv, tq, tk):
    bsz, tlen, _ = mk.shape
    has_meta = meta_kv is not None
    npair = MLA_HEADS // 2
    in_specs = [
        pl.BlockSpec((1, 2 * LANES, tq), lambda b, p, i: (b, p, i)),
        pl.BlockSpec((1, tlen, 2 * LANES), lambda b, p, i: (b, 0, p)),
        pl.BlockSpec((1, 2 * LANES, tlen), lambda b, p, i: (b, p, 0)),
    ]
    args = [mqt, mk, mvt]
    if has_meta:
        in_specs += [
            pl.BlockSpec((1, BLK, 2 * LANES), lambda b, p, i: (0, 0, p)),
            pl.BlockSpec((1, 2 * LANES, BLK), lambda b, p, i: (0, p, 0)),
        ]
        args += list(meta_kv)
    return pl.pallas_call(
        functools.partial(_mla_kernel, tq=tq, tk=tk, has_meta=has_meta),
        name="mla_real" if has_meta else "mla_meta",
        out_shape=jax.ShapeDtypeStruct((bsz, tlen, npair * LANES), bf16),
        grid=(bsz, npair, tlen // tq),
        in_specs=in_specs,
        out_specs=pl.BlockSpec((1, tq, LANES), lambda b, p, i: (b, i, p)),
        scratch_shapes=[
            pltpu.VMEM((2, 1, tq), f32),
            pltpu.VMEM((2, LANES, tq), f32),
            pltpu.VMEM((2, tk, tq), f32),
        ],
        compiler_params=pltpu.CompilerParams(
            dimension_semantics=("parallel", "parallel", "arbitrary"), vmem_limit_bytes=VMEM_LIMIT),
    )(*args)


def _diff_kernel(*refs, tq, tk, has_meta, lam_init):
    if has_meta:
        (qt_ref, k_ref, vt_ref, mk_ref, mvt_ref, ddiag_ref, dprev_ref, dmeta_ref, lam_ref, subg_ref,
         o_ref, qs_ref, m_ref, acc_ref, s_ref) = refs
    else:
        (qt_ref, k_ref, vt_ref, ddiag_ref, lam_ref, subg_ref, o_ref, qs_ref, m_ref, acc_ref, s_ref) = refs
    qi = pl.program_id(2)
    ratio = tq // tk
    seg = 2 * tk
    _init_stats(m_ref, acc_ref)

    def head(hh):
        return slice(hh * LANES, (hh + 1) * LANES)

    def half_cols(h):
        return slice(h * seg, (h + 1) * seg)

    def map_cols(h, c):
        return slice(h * seg + c * tk, h * seg + (c + 1) * tk)

    group = _row((LANES, tk)) // DIFF_QK
    for h in range(ratio):
        qt = qt_ref[0, :, h * tk:(h + 1) * tk]
        for hh in range(2):
            for c in range(2):
                qs_ref[hh, :, map_cols(h, c)] = jnp.where(group == 2 * hh + c, qt, jnp.zeros_like(qt))

    def keys(j):
        return pl.ds(pl.multiple_of(j * tk, tk), tk)

    def stage_a(j, hh, h):
        s_ref[hh, :, half_cols(h)] = _dot(k_ref[0, keys(j), :], qs_ref[hh, :, half_cols(h)])

    def stage_b(j, hh, h, bias_ref):
        s = s_ref[hh, :, half_cols(h)]
        if bias_ref is not None:
            b = bias_ref[hh]
            s = s + jnp.concatenate([b, b], axis=1)
        _flash_update(s, vt_ref[0, head(hh), keys(j)], m_ref, acc_ref, hh, half_cols(h))

    def tile_step(j, first_half, biases, next_first_half):
        chains = [(hh, h) for h in range(first_half, ratio) for hh in range(2)]
        for idx, (hh, h) in enumerate(chains):
            if idx + 1 < len(chains):
                stage_a(j, *chains[idx + 1])
            elif next_first_half is not None:
                stage_a(j + 1, 0, next_first_half)
            stage_b(j, hh, h, biases.get(h))

    if has_meta:
        for h in range(ratio):
            for hh in range(2):
                b = dmeta_ref[jnp.minimum(qi, 1), hh] if h == 0 else dmeta_ref[1, hh]
                s = _dot(mk_ref[0], qs_ref[hh, :, half_cols(h)]) + jnp.concatenate([b, b], axis=1)
                _flash_update(s, mvt_ref[0, head(hh), :], m_ref, acc_ref, hh, half_cols(h))

    stage_a(0, 0, 0)
    first_diag = qi * ratio
    if has_meta:
        def body(i, carry):
            tile_step(2 * i, 0, {}, 0)
            tile_step(2 * i + 1, 0, {}, 0)
            return carry

        lax.fori_loop(0, jnp.maximum(first_diag - 2, 0) // 2, body, 0)

        @pl.when(qi >= 1)
        def _():
            tile_step(first_diag - 2, 0, {}, 0)
            tile_step(first_diag - 1, 0, {0: dprev_ref}, 0)

    for d in range(ratio):
        biases = {d: ddiag_ref}
        if d + 1 < ratio:
            biases[d + 1] = dprev_ref
        tile_step(first_diag + d, d, biases, d + 1 if d + 1 < ratio else None)

    lp = lam_ref[...]
    lam = (jnp.exp(jnp.sum(lp[0:1] * lp[1:2], axis=-1, keepdims=True))
           - jnp.exp(jnp.sum(lp[2:3] * lp[3:4], axis=-1, keepdims=True)) + lam_init)
    outs = []
    for hh in range(2):
        halves = [_normalized(acc_ref, hh, map_cols(h, 0)) - lam * _normalized(acc_ref, hh, map_cols(h, 1))
                  for h in range(ratio)]
        o = halves[0] if ratio == 1 else jnp.concatenate(halves, axis=1)
        ms = jnp.mean(o * o, axis=0, keepdims=True)
        outs.append(o * lax.rsqrt(ms + EPS))
    o_nat = jnp.concatenate(outs, axis=0).T
    o_ref[0] = ((o_nat * subg_ref[...]) * (1.0 - lam_init)).astype(bf16)


def _diff_attention(dqt, dk, dvt, meta_kv, tiles, lam_p, sub_g, lam_init, tq, tk):
    bsz, tlen, _ = dk.shape
    has_meta = meta_kv is not None
    assert tq % tk == 0 and (not has_meta or (tq // tk) % 2 == 0)
    npair = DIFF_HEADS // 2
    in_specs = [
        pl.BlockSpec((1, LANES, tq), lambda b, p, i: (b, p, i)),
        pl.BlockSpec((1, tlen, LANES), lambda b, p, i: (b, 0, p)),
        pl.BlockSpec((1, 2 * LANES, tlen), lambda b, p, i: (b, p, 0)),
    ]
    args = [dqt, dk, dvt]
    if has_meta:
        in_specs += [
            pl.BlockSpec((1, BLK, LANES), lambda b, p, i: (0, 0, p)),
            pl.BlockSpec((1, 2 * LANES, BLK), lambda b, p, i: (0, p, 0)),
            pl.BlockSpec((2, tk, tk), lambda b, p, i: (p, 0, 0)),
            pl.BlockSpec((2, tk, tk), lambda b, p, i: (p, 0, 0)),
            pl.BlockSpec((2, 2, BLK, tk), lambda b, p, i: (0, p, 0, 0)),
        ]
        args += list(meta_kv) + [tiles["dd_diag"], tiles["dd_prev"], tiles["dd_meta"]]
    else:
        in_specs += [pl.BlockSpec((2, tk, tk), lambda b, p, i: (p, 0, 0))]
        args += [tiles["dd_self"]]
    in_specs += [_const_spec((4, DIFF_QK)), _const_spec((1, LANES))]
    args += [lam_p, sub_g]
    return pl.pallas_call(
        functools.partial(_diff_kernel, tq=tq, tk=tk, has_meta=has_meta, lam_init=lam_init),
        name="diff_real" if has_meta else "diff_meta",
        out_shape=jax.ShapeDtypeStruct((bsz, tlen, npair * LANES), bf16),
        grid=(bsz, npair, tlen // tq),
        in_specs=in_specs,
        out_specs=pl.BlockSpec((1, tq, LANES), lambda b, p, i: (b, i, p)),
        scratch_shapes=[
            pltpu.VMEM((2, LANES, 2 * tq), bf16),
            pltpu.VMEM((2, 1, 2 * tq), f32),
            pltpu.VMEM((2, LANES, 2 * tq), f32),
            pltpu.VMEM((2, tk, 2 * tq), f32),
        ],
        compiler_params=pltpu.CompilerParams(
            dimension_semantics=("parallel", "parallel", "arbitrary"), vmem_limit_bytes=VMEM_LIMIT),
    )(*args)


def _swa_kernel(*refs, tq, has_band):
    if has_band:
        qt_ref, k_ref, vt_ref, mk_ref, mvt_ref, dband_ref, dmeta_ref, sink_ref, o_ref, sm_ref, sb_ref = refs
    else:
        qt_ref, mk_ref, mvt_ref, dmeta_ref, sink_ref, o_ref, sm_ref = refs
    qi = pl.program_id(1)
    feat = _row((LANES, BLK))
    first_head = _lane((1, 2 * BLK)) < BLK
    n_meta_var = dmeta_ref.shape[0]

    def block(sub):
        blk = qi * (tq // BLK) + sub
        c0 = sub * BLK
        mvar = jnp.minimum(blk, n_meta_var - 1)
        if has_band:
            bvar = jnp.minimum(blk, 1)
            koff = pl.multiple_of(jnp.maximum(blk - 1, 0) * BLK, BLK)
            kband = k_ref[0, pl.ds(koff, 2 * BLK), :]
            vtband = vt_ref[0, :, pl.ds(koff, 2 * BLK)]

        def stage_a(r):
            q2t = qt_ref[0, r * LANES:(r + 1) * LANES, c0:c0 + BLK]
            zero = jnp.zeros_like(q2t)
            qs = jnp.concatenate([jnp.where(feat < SWA_HD, q2t, zero), jnp.where(feat >= SWA_HD, q2t, zero)],
                                 axis=1)
            sm_ref[r] = _dot(mk_ref[0], qs)
            if has_band:
                sb_ref[r] = _dot(kband, qs)

        def stage_b(r):
            h0, h1 = r, SWA_REP + r
            sink = jnp.where(first_head, sink_ref[h0], sink_ref[h1]) * LOG2E
            s_meta = sm_ref[r] + jnp.concatenate([dmeta_ref[mvar, h0], dmeta_ref[mvar, h1]], axis=1)
            m = jnp.maximum(jnp.max(s_meta, axis=0, keepdims=True), sink)
            if has_band:
                s_band = sb_ref[r] + jnp.concatenate([dband_ref[bvar, h0], dband_ref[bvar, h1]], axis=1)
                m = jnp.maximum(m, jnp.max(s_band, axis=0, keepdims=True))
            p_meta = jnp.exp2(s_meta - m)
            denom = jnp.sum(p_meta, axis=0, keepdims=True) + jnp.exp2(sink - m)
            acc = _dot(mvt_ref[0], p_meta.astype(bf16))
            if has_band:
                p_band = jnp.exp2(s_band - m)
                denom = denom + jnp.sum(p_band, axis=0, keepdims=True)
                acc = acc + _dot(vtband, p_band.astype(bf16))
            ot = acc / denom
            pair = jnp.where(feat < SWA_HD, ot[:, 0:BLK], ot[:, BLK:2 * BLK])
            o_ref[0, c0:c0 + BLK, r * LANES:(r + 1) * LANES] = pair.T.astype(bf16)

        return stage_a, stage_b

    stages = [block(sub) for sub in range(tq // BLK)]
    chains = [(sub, r) for sub in range(tq // BLK) for r in range(SWA_REP)]
    stages[0][0](0)
    stages[0][0](1)
    for i, (sub, r) in enumerate(chains):
        if i + 2 < len(chains):
            nsub, nr = chains[i + 2]
            stages[nsub][0](nr)
        stages[sub][1](r)


def _swa_attention(sqt, sk, svt, meta_kv, tiles, sinks, tq):
    bsz, width, tlen = sqt.shape
    has_band = meta_kv is not None
    smem = pl.BlockSpec(memory_space=pltpu.SMEM)
    if has_band:
        in_specs = [
            pl.BlockSpec((1, width, tq), lambda b, i: (b, 0, i)),
            pl.BlockSpec((1, tlen, LANES), lambda b, i: (b, 0, 0)),
            pl.BlockSpec((1, LANES, tlen), lambda b, i: (b, 0, 0)),
            _const_spec((1, BLK, LANES)),
            _const_spec((1, LANES, BLK)),
            _const_spec(tiles["ds_band"].shape),
            _const_spec(tiles["ds_meta"].shape),
            smem,
        ]
        args = [sqt, sk, svt, meta_kv[0], meta_kv[1], tiles["ds_band"], tiles["ds_meta"], sinks]
    else:
        in_specs = [
            pl.BlockSpec((1, width, tq), lambda b, i: (b, 0, i)),
            _const_spec((1, BLK, LANES)),
            _const_spec((1, LANES, BLK)),
            _const_spec(tiles["ds_self"].shape),
            smem,
        ]
        args = [sqt, sk, svt, tiles["ds_self"], sinks]
    return pl.pallas_call(
        functools.partial(_swa_kernel, tq=tq, has_band=has_band),
        name="swa_real" if has_band else "swa_meta",
        out_shape=jax.ShapeDtypeStruct((bsz, tlen, width), bf16),
        grid=(bsz, tlen // tq),
        in_specs=in_specs,
        out_specs=pl.BlockSpec((1, tq, width), lambda b, i: (b, i, 0)),
        scratch_shapes=[pltpu.VMEM((SWA_REP, BLK, 2 * BLK), f32)]
        + ([pltpu.VMEM((SWA_REP, 2 * BLK, 2 * BLK), f32)] if has_band else []),
        compiler_params=pltpu.CompilerParams(
            dimension_semantics=("parallel", "parallel"), vmem_limit_bytes=VMEM_LIMIT),
    )(*args)


def _out_ffn_kernel(h_ref, ya_ref, yb_ref, yc_ref, woa_ref, wob_ref, woc_ref, g_ref, wg_ref, wu_ref, wd_ref,
                    fg_ref, o_ref, *, final):
    h1 = (h_ref[0] + _dot(ya_ref[0], woa_ref[...]) + _dot(yb_ref[0], wob_ref[...])
          + _dot(yc_ref[0], woc_ref[...]))
    hn = _rms(h1, g_ref[...]).astype(bf16)
    gate = _dot(hn, wg_ref[...])
    up = _dot(hn, wu_ref[...])
    act = ((gate * jax.nn.sigmoid(gate)) * up).astype(bf16)
    h2 = h1 + _dot(act, wd_ref[...])
    if final:
        h2 = _rms(h2, fg_ref[...])
    o_ref[0] = h2


def _single(shape):
    nd = len(shape)
    return pl.BlockSpec(shape, lambda *_: (0,) * nd, pipeline_mode=pl.Buffered(1))


def _out_ffn(h, ya, yb, yc, lw, final_g, final, tm):
    bsz, t, _ = h.shape

    def row(w):
        return pl.BlockSpec((1, tm, w), lambda b, i: (b, i, 0))

    in_specs = [
        row(D_MODEL), row(384), row(256), row(384),
        _single((384, D_MODEL)), _single((256, D_MODEL)), _single((384, D_MODEL)),
        _single((1, D_MODEL)),
        _single((D_MODEL, D_FF)), _single((D_MODEL, D_FF)), _single((D_FF, D_MODEL)),
        _single((1, D_MODEL)),
    ]
    return pl.pallas_call(
        functools.partial(_out_ffn_kernel, final=final),
        name="out_ffn_real" if bsz > 1 else "out_ffn_meta",
        out_shape=jax.ShapeDtypeStruct((bsz, t, D_MODEL), f32),
        grid=(bsz, t // tm),
        in_specs=in_specs,
        out_specs=row(D_MODEL),
        compiler_params=pltpu.CompilerParams(
            dimension_semantics=("parallel", "parallel"), vmem_limit_bytes=VMEM_LIMIT),
    )(h, ya, yb, yc, lw["wo_a"], lw["wo_b"], lw["wo_c"], lw["ffn_norm"], lw["w_gate"], lw["w_up"],
      lw["w_down"], final_g)


def _rot_half_cols(w):
    half = MLA_ROPE // 2
    return jnp.concatenate([-w[:, half:], w[:, :half]], axis=1)


def _layer_weights(l, attn_norm, w_in, mla_q_norm, mla_w_qb, mla_kv_norm, mla_w_kvb, w_out, ffn_norm,
                   w_gate, w_up, w_down):
    w = w_in[l]
    o = 0
    parts = {}
    for name, width in (("c_q", 256), ("c_kv", 128), ("k_rope", 32), ("dq", 256), ("dk", 256), ("dv", 256),
                        ("sq", 384), ("sk", 128), ("sv", 128)):
        parts[name] = w[:, o:o + width]
        o += width
    kr = parts["k_rope"]
    ksw = _rot_half_cols(kr)
    z64 = jnp.zeros((D_MODEL, 64), f32)
    swa_order = [g * SWA_REP + r for r in range(SWA_REP) for g in range(SWA_KV_HEADS)]
    sq = jnp.concatenate([parts["sq"][:, h * SWA_HD:(h + 1) * SWA_HD] for h in swa_order], axis=1)
    w_nat = jnp.concatenate([parts["c_q"], parts["c_kv"], z64, kr, kr, z64, ksw, ksw, parts["dk"],
                             parts["sk"]], axis=1)
    assert w_nat.shape == (D_MODEL, D_NAT)
    w_tr = jnp.concatenate([parts["dq"], parts["dv"], sq, parts["sv"]], axis=1).T
    assert w_tr.shape == (D_TR, D_MODEL)

    wq = mla_w_qb[l]
    q_cols = []
    for h in range(MLA_HEADS):
        base = h * (MLA_NOPE + MLA_ROPE)
        rope = wq[:, base + MLA_NOPE:base + MLA_NOPE + MLA_ROPE]
        q_cols += [wq[:, base:base + MLA_NOPE], rope, _rot_half_cols(rope)]
    w_qbt = jnp.concatenate(q_cols, axis=1).T

    wkv = mla_w_kvb[l]
    zk = jnp.zeros((MLA_KV_RANK, LANES - MLA_NOPE), f32)
    k_cols, v_cols = [], []
    for h in range(MLA_HEADS):
        base = h * (MLA_NOPE + MLA_V)
        k_cols += [wkv[:, base:base + MLA_NOPE], zk]
        v_cols.append(wkv[:, base + MLA_NOPE:base + MLA_NOPE + MLA_V])
    w_kb = jnp.concatenate(k_cols, axis=1)
    w_vt = jnp.concatenate(v_cols, axis=1).T

    wo = w_out[l]
    na = MLA_HEADS * MLA_V
    nb = DIFF_HEADS * DIFF_V
    wo_c = wo[na + nb:]
    wo_c = jnp.concatenate([wo_c[h * SWA_HD:(h + 1) * SWA_HD] for h in swa_order], axis=0)
    return {
        "attn_norm": attn_norm[l][None], "w_nat": w_nat.astype(bf16), "w_tr": w_tr.astype(bf16),
        "q_norm": mla_q_norm[l][None], "w_qbt": w_qbt.astype(bf16),
        "kv_norm": mla_kv_norm[l][None], "w_kb": w_kb.astype(bf16), "w_vt": w_vt.astype(bf16),
        "wo_a": wo[:na].astype(bf16), "wo_b": wo[na:na + nb].astype(bf16), "wo_c": wo_c.astype(bf16),
        "ffn_norm": ffn_norm[l][None],
        "w_gate": w_gate[l].astype(bf16), "w_up": w_up[l].astype(bf16), "w_down": w_down[l].astype(bf16),
    }


def _rope_tables(pos):
    inv_freq = ROPE_THETA ** (-jnp.arange(0, MLA_ROPE, 2, dtype=f32) / MLA_ROPE)
    ang = pos.astype(f32)[:, None] * inv_freq[None, :]
    cos, sin = jnp.cos(ang), jnp.sin(ang)
    n = pos.shape[0]
    cc = jnp.concatenate([cos, cos], axis=1)
    ss = jnp.concatenate([sin, sin], axis=1)
    z = jnp.zeros((n, 64), f32)
    return {
        "csqt": (jnp.concatenate([jnp.ones((n, 64), f32), cc, ss], axis=1) * (MLA_SCALE * LOG2E)).T,
        "cka": jnp.concatenate([z, cc, cc], axis=1),
        "ckb": jnp.concatenate([z, ss, ss], axis=1),
    }


def _mixers(proj, meta_proj, tiles, lam_p, sub_g, lam_init, sinks, real):
    mqt, mk, mvt, dqt, dk, dvt, sq, sk, sv = proj
    if real:
        m_mk, m_mvt, m_dk, m_dvt, m_sk, m_sv = (meta_proj[i] for i in (1, 2, 4, 5, 7, 8))
        ya = _mla_attention(mqt, mk, mvt, (m_mk, m_mvt), MLA_QTILE, MLA_KTILE)
        yb = _diff_attention(dqt, dk, dvt, (m_dk, m_dvt), tiles, lam_p, sub_g, lam_init, DIFF_QTILE, DIFF_TILE)
        yc = _swa_attention(sq, sk, sv, (m_sk, m_sv), tiles, sinks, SWA_QTILE)
    else:
        ya = _mla_attention(mqt, mk, mvt, None, BLK, BLK)
        yb = _diff_attention(dqt, dk, dvt, None, tiles, lam_p, sub_g, lam_init, BLK, BLK)
        yc = _swa_attention(sq, sk, sv, None, tiles, sinks, BLK)
    return ya, yb, yc


def kernel(x, meta_tokens, rel_bias, attn_norm, w_in, mla_q_norm, mla_w_qb, mla_kv_norm, mla_w_kvb, diff_lambda,
           diff_subln, swa_sinks, w_out, ffn_norm, w_gate, w_up, w_down, final_norm):
    bsz, seq, _ = x.shape
    assert seq % MLA_QTILE == 0 and MLA_QTILE % MLA_KTILE == 0 and seq % DIFF_QTILE == 0 and seq % SWA_QTILE == 0 and seq % ROW_TILE == 0

    h = x
    h_meta = jnp.concatenate([meta_tokens.astype(f32), jnp.zeros((BLK - N_META, D_MODEL), f32)], axis=0)[None]
    tabs = _rope_tables(N_META + jnp.arange(seq))
    tabs_meta = _rope_tables(jnp.minimum(jnp.arange(BLK), N_META - 1))
    tiles = dict(zip(("dd_diag", "dd_prev", "dd_meta", "dd_self", "ds_band", "ds_meta", "ds_self"),
                     _bias_tiles(rel_bias)))
    final_g = final_norm[None]

    for l in range(DEPTH):
        lw = _layer_weights(l, attn_norm, w_in, mla_q_norm, mla_w_qb, mla_kv_norm, mla_w_kvb, w_out, ffn_norm,
                            w_gate, w_up, w_down)
        lam_init = 0.8 - 0.6 * math.exp(-0.3 * l)
        sub_g = jnp.concatenate([diff_subln[l], diff_subln[l]])[None]
        last = l == DEPTH - 1

        meta_proj = _project(h_meta, lw, tabs_meta, BLK)
        proj = _project(h, lw, tabs, ROW_TILE)
        if not last:
            y_meta = _mixers(meta_proj, None, tiles, diff_lambda[l], sub_g, lam_init, swa_sinks[l], real=False)
            h_meta = _out_ffn(h_meta, *y_meta, lw, final_g, False, BLK)
        y = _mixers(proj, meta_proj, tiles, diff_lambda[l], sub_g, lam_init, swa_sinks[l], real=True)
        h = _out_ffn(h, *y, lw, final_g, last, ROW_TILE)
    return h
```

```python
import functools
import math

import numpy as np
import jax
import jax.numpy as jnp
from jax import lax
from jax.experimental import pallas as pl
from jax.experimental.pallas import tpu as pltpu

D_MODEL = 1024
DEPTH = 2
N_META = 16
BLK = 128

MLA_HEADS = 6
MLA_Q_RANK = 256
MLA_KV_RANK = 128
MLA_NOPE = 64
MLA_ROPE = 32
MLA_V = 64
ROPE_THETA = 10000.0

DIFF_HEADS = 4
DIFF_QK = 32
DIFF_V = 64

SWA_HEADS = 6
SWA_KV_HEADS = 2
SWA_REP = SWA_HEADS // SWA_KV_HEADS
SWA_HD = 64
WINDOW = 128

REL_BUCKETS = 32
REL_MAX_DIST = 128
D_FF = 2816
NEG_INF = -1e30
EPS = 1e-6
LOG2E = math.log2(math.e)

MLA_SCALE = (MLA_NOPE + MLA_ROPE) ** -0.5
DIFF_SCALE = DIFF_QK ** -0.5
SWA_SCALE = SWA_HD ** -0.5

LANES = 128
HEAD_V = 64
D_NAT = 1024
D_TR = 1024
VMEM_LIMIT = 56 * 1024 * 1024

MLA_QTILE = 2048
MLA_KTILE = 512
DIFF_QTILE = 1024
DIFF_TILE = 512
SWA_QTILE = 512
PROJ_TILE = 1024
ROW_TILE = 512

f32 = jnp.float32
bf16 = jnp.bfloat16


def _bucket_thresholds():
    n = np.arange(0, 4 * REL_MAX_DIST)
    max_exact = REL_BUCKETS // 2
    nf = np.maximum(n, max_exact).astype(np.float32)
    large = max_exact + (np.log(nf / np.float32(max_exact)) / np.float32(math.log(REL_MAX_DIST / max_exact))
                         * np.float32(REL_BUCKETS - max_exact)).astype(np.int32)
    bucket = np.where(n < max_exact, n, np.minimum(large, REL_BUCKETS - 1))
    assert np.all(np.diff(bucket) >= 0) and bucket[-1] == REL_BUCKETS - 1
    return [int(np.argmax(bucket >= k)) for k in range(1, REL_BUCKETS)]


BUCKET_THR = _bucket_thresholds()
FAR_DIST = BUCKET_THR[-1]
assert FAR_DIST <= BLK - N_META + 1


def _dot(a, b):
    return jnp.dot(a, b, preferred_element_type=f32)


def _dot_nt(a, b):
    return lax.dot_general(a, b, (((1,), (1,)), ((), ())), preferred_element_type=f32)


def _rms(x, g):
    return (x * lax.rsqrt(jnp.mean(x * x, axis=-1, keepdims=True) + EPS)) * g


def _lane(shape):
    return lax.broadcasted_iota(jnp.int32, shape, len(shape) - 1)


def _row(shape):
    return lax.broadcasted_iota(jnp.int32, shape, 0)


def _bias_lookup(n, tab_ref, h):
    acc = jnp.full(n.shape, tab_ref[0, h], f32)
    for k, thr in enumerate(BUCKET_THR, start=1):
        acc = jnp.where(n >= thr, tab_ref[k, h], acc)
    return acc


def _bias_kernel(tab_ref, dd_diag, dd_prev, dd_meta, dd_self, ds_band, ds_meta, ds_self, *, td):
    rc = 32

    def rows_cols(r0, cols):
        a = r0 + lax.broadcasted_iota(jnp.int32, (rc, cols), 0)
        b = lax.broadcasted_iota(jnp.int32, (rc, cols), 1)
        return a, b

    def diff_body(i, carry, near_prev):
        r0 = pl.multiple_of(i * rc, rc)
        k, q = rows_cols(r0, td)
        for h in range(DIFF_HEADS):
            c = tab_ref[REL_BUCKETS - 1, h]
            dd_diag[h, pl.ds(r0, rc), :] = jnp.where(
                k <= q, (_bias_lookup(jnp.maximum(q - k, 0), tab_ref, h) - c) * LOG2E, NEG_INF)
            if near_prev:
                dd_prev[h, pl.ds(r0, rc), :] = (_bias_lookup(q + td - k, tab_ref, h) - c) * LOG2E
            else:
                dd_prev[h, pl.ds(r0, rc), :] = jnp.zeros((rc, td), f32)
        return carry

    n_far = (td - BLK) // rc
    lax.fori_loop(0, n_far, functools.partial(diff_body, near_prev=False), 0)
    lax.fori_loop(n_far, td // rc, functools.partial(diff_body, near_prev=True), 0)

    def blk_body(i, carry):
        r0 = pl.multiple_of(i * rc, rc)
        j, q = rows_cols(r0, td)
        j2, q2 = rows_cols(r0, BLK)
        self_ok = j2 <= jnp.minimum(q2, N_META - 1)
        for h in range(DIFF_HEADS):
            c = tab_ref[REL_BUCKETS - 1, h]
            dd_meta[0, h, pl.ds(r0, rc), :] = jnp.where(
                j < N_META, (_bias_lookup(N_META + q - j, tab_ref, h) - c) * LOG2E, NEG_INF)
            dd_meta[1, h, pl.ds(r0, rc), :] = jnp.where(j < N_META, 0.0, NEG_INF)
            dd_self[h, pl.ds(r0, rc), :] = jnp.where(
                self_ok, (_bias_lookup(jnp.maximum(q2 - j2, 0), tab_ref, h) - c) * LOG2E, NEG_INF)
        k, a = j2, q2
        for h in range(SWA_HEADS):
            hb = DIFF_HEADS + h
            near = _bias_lookup(jnp.maximum(a - k, 0), tab_ref, hb) * LOG2E
            cur = jnp.where(k <= a, near, NEG_INF)
            prev = jnp.where(k > a, _bias_lookup(a + BLK - k, tab_ref, hb) * LOG2E, NEG_INF)
            ds_band[0, h, pl.ds(r0, rc), :] = cur
            ds_band[0, h, pl.ds(BLK + r0, rc), :] = jnp.full((rc, BLK), NEG_INF, f32)
            ds_band[1, h, pl.ds(r0, rc), :] = prev
            ds_band[1, h, pl.ds(BLK + r0, rc), :] = cur
            ds_meta[0, h, pl.ds(r0, rc), :] = jnp.where(
                k < N_META, _bias_lookup(N_META + a - k, tab_ref, hb) * LOG2E, NEG_INF)
            far = jnp.full((rc, BLK), tab_ref[REL_BUCKETS - 1, hb], f32) * LOG2E
            ds_meta[1, h, pl.ds(r0, rc), :] = jnp.where(k < N_META, far, NEG_INF)
            ds_self[0, h, pl.ds(r0, rc), :] = jnp.where(self_ok, near, NEG_INF)
        return carry

    lax.fori_loop(0, BLK // rc, blk_body, 0)


def _bias_tiles(rel_bias):
    td = DIFF_TILE
    outs = (
        jax.ShapeDtypeStruct((DIFF_HEADS, td, td), f32),
        jax.ShapeDtypeStruct((DIFF_HEADS, td, td), f32),
        jax.ShapeDtypeStruct((2, DIFF_HEADS, BLK, td), f32),
        jax.ShapeDtypeStruct((DIFF_HEADS, BLK, BLK), f32),
        jax.ShapeDtypeStruct((2, SWA_HEADS, 2 * BLK, BLK), f32),
        jax.ShapeDtypeStruct((2, SWA_HEADS, BLK, LANES), f32),
        jax.ShapeDtypeStruct((1, SWA_HEADS, BLK, LANES), f32),
    )
    return pl.pallas_call(
        functools.partial(_bias_kernel, td=td),
        name="bias_tiles",
        out_shape=outs,
        in_specs=[pl.BlockSpec(memory_space=pltpu.SMEM)],
        compiler_params=pltpu.CompilerParams(vmem_limit_bytes=VMEM_LIMIT),
    )(rel_bias)


def _store_vt(out_ref, vt, heads):
    tm = vt.shape[1]
    ones_blk = jnp.where(_row((HEAD_V, tm)) == 0, 1.0, 0.0).astype(bf16)
    for hd in range(heads):
        out_ref[0, hd * LANES:hd * LANES + HEAD_V, :] = vt[hd * HEAD_V:(hd + 1) * HEAD_V].astype(bf16)
        out_ref[0, hd * LANES + HEAD_V:(hd + 1) * LANES, :] = ones_blk


def _proj_kernel(h_ref, g_ref, wnat_ref, wtr_ref, qn_ref, wqbt_ref, kvn_ref, wkb_ref, wvt_ref,
                 csqt_ref, cka_ref, ckb_ref,
                 mqt_ref, mk_ref, mvt_ref, dqt_ref, dk_ref, dvt_ref, sqt_ref, sk_ref, svt_ref):
    hn = _rms(h_ref[0], g_ref[...]).astype(bf16)
    proj = _dot(hn, wnat_ref[...])
    tr = _dot_nt(wtr_ref[...], hn)
    c_q = proj[:, 0:256]
    c_kv = proj[:, 256:384]
    rope_a = proj[:, 384:512]
    rope_b = proj[:, 512:640]
    dk_ref[0] = proj[:, 640:896].astype(bf16)
    sk_ref[0] = proj[:, 896:1024].astype(bf16)
    dqt_ref[0] = (tr[0:256] * (DIFF_SCALE * LOG2E)).astype(bf16)
    _store_vt(dvt_ref, tr[256:512], DIFF_HEADS)
    sqt_ref[0] = (tr[512:896] * (SWA_SCALE * LOG2E)).astype(bf16)
    svt_ref[0] = tr[896:1024].astype(bf16)

    cqn = _rms(c_q, qn_ref[...]).astype(bf16)
    ckvn = _rms(c_kv, kvn_ref[...]).astype(bf16)
    qt = _dot_nt(wqbt_ref[...], cqn)
    kk = _dot(ckvn, wkb_ref[...])
    vt = _dot_nt(wvt_ref[...], ckvn)
    csqt = csqt_ref[...]
    k_rot = rope_a * cka_ref[...] + rope_b * ckb_ref[...]
    for hd in range(MLA_HEADS):
        sl = slice(hd * LANES, (hd + 1) * LANES)
        mqt_ref[0, sl, :] = (qt[sl] * csqt).astype(bf16)
        mk_ref[0, :, sl] = (kk[:, sl] + k_rot).astype(bf16)
    _store_vt(mvt_ref, vt, MLA_HEADS)


def _const_spec(shape):
    nd = len(shape)
    return pl.BlockSpec(shape, lambda *_: (0,) * nd)


def _layer_spec(shape, l, single=False):
    nd = len(shape)
    return pl.BlockSpec((None,) + tuple(shape), lambda *_: (l,) + (0,) * nd,
                        pipeline_mode=pl.Buffered(1) if single else None)


def _project(h, l, lw, tabs, tm):
    bsz, t, _ = h.shape
    outs = ((768, True), (768, False), (768, True), (256, True), (256, False), (512, True),
            (384, True), (128, False), (128, True))
    out_shape = tuple(jax.ShapeDtypeStruct((bsz, w, t) if tr else (bsz, t, w), bf16) for w, tr in outs)
    out_specs = tuple(pl.BlockSpec((1, w, tm), lambda b, i: (b, 0, i)) if tr
                      else pl.BlockSpec((1, tm, w), lambda b, i: (b, i, 0)) for w, tr in outs)
    tab_spec = pl.BlockSpec((tm, LANES), lambda b, i: (i, 0))
    in_specs = [
        pl.BlockSpec((1, tm, D_MODEL), lambda b, i: (b, i, 0)),
        _layer_spec((1, D_MODEL), l),
        _layer_spec((D_MODEL, D_NAT), l),
        _layer_spec((D_TR, D_MODEL), l),
        _layer_spec((1, MLA_Q_RANK), l),
        _layer_spec((768, MLA_Q_RANK), l),
        _layer_spec((1, MLA_KV_RANK), l),
        _layer_spec((MLA_KV_RANK, 768), l),
        _layer_spec((384, MLA_KV_RANK), l),
        pl.BlockSpec((LANES, tm), lambda b, i: (0, i)),
        tab_spec, tab_spec,
    ]
    return pl.pallas_call(
        _proj_kernel,
        name="proj_real" if bsz > 1 else "proj_meta",
        out_shape=out_shape,
        grid=(bsz, t // tm),
        in_specs=in_specs,
        out_specs=out_specs,
        compiler_params=pltpu.CompilerParams(
            dimension_semantics=("parallel", "parallel"), vmem_limit_bytes=VMEM_LIMIT),
    )(h, lw["attn_norm"], lw["w_nat"], lw["w_tr"], lw["q_norm"], lw["w_qbt"], lw["kv_norm"], lw["w_kb"],
      lw["w_vt"], tabs["csqt"], tabs["cka"], tabs["ckb"])


def _flash_update(s, vt, m_ref, acc_ref, idx, cols=slice(None)):
    m_prev = m_ref[idx, :, cols]
    m_new = jnp.maximum(m_prev, jnp.max(s, axis=0, keepdims=True))
    alpha = jnp.exp2(m_prev - m_new)
    p = jnp.exp2(s - m_new).astype(bf16)
    acc_ref[idx, :, cols] = acc_ref[idx, :, cols] * alpha + _dot(vt, p)
    m_ref[idx, :, cols] = m_new


def _init_stats(m_ref, acc_ref):
    m_ref[...] = jnp.full(m_ref.shape, NEG_INF, f32)
    acc_ref[...] = jnp.zeros(acc_ref.shape, f32)


def _normalized(acc_ref, idx, cols):
    return acc_ref[idx, 0:HEAD_V, cols] / acc_ref[idx, HEAD_V:HEAD_V + 1, cols]


def _mla_kernel(*refs, tq, tk, has_meta):
    if has_meta:
        qt_ref, k_ref, vt_ref, mk_ref, mvt_ref, o_ref, m_ref, acc_ref, s_ref = refs
    else:
        qt_ref, k_ref, vt_ref, o_ref, m_ref, acc_ref, s_ref = refs
    qi = pl.program_id(2)
    ratio = tq // tk
    _init_stats(m_ref, acc_ref)

    def head(hh):
        return slice(hh * LANES, (hh + 1) * LANES)

    if has_meta:
        valid = _row((BLK, tq)) < N_META
        for hh in range(2):
            s = _dot(mk_ref[0, :, head(hh)], qt_ref[0, head(hh), :])
            _flash_update(jnp.where(valid, s, NEG_INF), mvt_ref[0, head(hh), :], m_ref, acc_ref, hh)

    def keys(j):
        return pl.ds(pl.multiple_of(j * tk, tk), tk)

    def stage_a(j, hh, cols):
        s_ref[hh, :, cols] = _dot(k_ref[0, keys(j), head(hh)], qt_ref[0, head(hh), cols])

    def stage_b(j, hh, cols, mask):
        s = s_ref[hh, :, cols]
        if mask is not None:
            s = jnp.where(mask, s, NEG_INF)
        _flash_update(s, vt_ref[0, head(hh), keys(j)], m_ref, acc_ref, hh, cols)

    def blocks(cols):
        return [slice(c, c + tk) for c in range(cols.start, cols.stop, tk)]

    def tile_step(j, cols, mask, next_cols):
        chains = [(hh, cb) for cb in blocks(cols) for hh in range(2)]
        for idx, (hh, cb) in enumerate(chains):
            if idx + 1 < len(chains):
                stage_a(j, *chains[idx + 1])
            elif next_cols is not None:
                stage_a(j + 1, 0, blocks(next_cols)[0])
            chain_mask = None if mask is None else mask[:, cb.start - cols.start:cb.stop - cols.start]
            stage_b(j, hh, cb, chain_mask)

    full = slice(0, tq)
    stage_a(0, 0, blocks(full)[0])
    n_plain = qi * ratio

    def body(i, carry):
        tile_step(2 * i, full, None, full)
        tile_step(2 * i + 1, full, None, full)
        return carry

    lax.fori_loop(0, n_plain // 2, body, 0)
    if ratio % 2 == 1:
        @pl.when(n_plain % 2 == 1)
        def _():
            tile_step(n_plain - 1, full, None, full)

    for d in range(ratio):
        width = tq - d * tk
        key = _row((tk, width))
        qry = _lane((tk, width))
        if has_meta:
            mask = key <= qry
        else:
            mask = key <= jnp.minimum(qry, N_META - 1)
        next_cols = slice((d + 1) * tk, tq) if d + 1 < ratio else None
        tile_step(n_plain + d, slice(d * tk, tq), mask, next_cols)

    ot = jnp.concatenate([_normalized(acc_ref, 0, full), _normalized(acc_ref, 1, full)], axis=0)
    o_ref[0] = ot.T.astype(bf16)


def _mla_attention(mqt, mk, mvt, meta_kv, tq, tk):
    bsz, tlen, _ = mk.shape
    has_meta = meta_kv is not None
    npair = MLA_HEADS // 2
    in_specs = [
        pl.BlockSpec((1, 2 * LANES, tq), lambda b, p, i: (b, p, i)),
        pl.BlockSpec((1, tlen, 2 * LANES), lambda b, p, i: (b, 0, p)),
        pl.BlockSpec((1, 2 * LANES, tlen), lambda b, p, i: (b, p, 0)),
    ]
    args = [mqt, mk, mvt]
    if has_meta:
        in_specs += [
            pl.BlockSpec((1, BLK, 2 * LANES), lambda b, p, i: (0, 0, p)),
            pl.BlockSpec((1, 2 * LANES, BLK), lambda b, p, i: (0, p, 0)),
        ]
        args += list(meta_kv)
    return pl.pallas_call(
        functools.partial(_mla_kernel, tq=tq, tk=tk, has_meta=has_meta),
        name="mla_real" if has_meta else "mla_meta",
        out_shape=jax.ShapeDtypeStruct((bsz, tlen, npair * LANES), bf16),
        grid=(bsz, npair, tlen // tq),
        in_specs=in_specs,
        out_specs=pl.BlockSpec((1, tq, LANES), lambda b, p, i: (b, i, p)),
        scratch_shapes=[
            pltpu.VMEM((2, 1, tq), f32),
            pltpu.VMEM((2, LANES, tq), f32),
            pltpu.VMEM((2, tk, tq), f32),
        ],
        compiler_params=pltpu.CompilerParams(
            dimension_semantics=("parallel", "parallel", "arbitrary"), vmem_limit_bytes=VMEM_LIMIT),
    )(*args)


def _diff_kernel(*refs, tq, tk, has_meta, lam_init):
    if has_meta:
        (qt_ref, k_ref, vt_ref, mk_ref, mvt_ref, ddiag_ref, dprev_ref, dmeta_ref, lam_ref, subg_ref,
         o_ref, qs_ref, m_ref, acc_ref, s_ref) = refs
    else:
        (qt_ref, k_ref, vt_ref, ddiag_ref, lam_ref, subg_ref, o_ref, qs_ref, m_ref, acc_ref, s_ref) = refs
    qi = pl.program_id(2)
    ratio = tq // tk
    seg = 2 * tk
    _init_stats(m_ref, acc_ref)

    def head(hh):
        return slice(hh * LANES, (hh + 1) * LANES)

    def half_cols(h):
        return slice(h * seg, (h + 1) * seg)

    def map_cols(h, c):
        return slice(h * seg + c * tk, h * seg + (c + 1) * tk)

    group = _row((LANES, tk)) // DIFF_QK
    for h in range(ratio):
        qt = qt_ref[0, :, h * tk:(h + 1) * tk]
        for hh in range(2):
            for c in range(2):
                qs_ref[hh, :, map_cols(h, c)] = jnp.where(group == 2 * hh + c, qt, jnp.zeros_like(qt))

    def keys(j):
        return pl.ds(pl.multiple_of(j * tk, tk), tk)

    def stage_a(j, hh, h):
        s_ref[hh, :, half_cols(h)] = _dot(k_ref[0, keys(j), :], qs_ref[hh, :, half_cols(h)])

    def stage_b(j, hh, h, bias_ref):
        s = s_ref[hh, :, half_cols(h)]
        if bias_ref is not None:
            b = bias_ref[hh]
            s = s + jnp.concatenate([b, b], axis=1)
        _flash_update(s, vt_ref[0, head(hh), keys(j)], m_ref, acc_ref, hh, half_cols(h))

    def tile_step(j, first_half, biases, next_first_half):
        chains = [(hh, h) for h in range(first_half, ratio) for hh in range(2)]
        for idx, (hh, h) in enumerate(chains):
            if idx + 1 < len(chains):
                stage_a(j, *chains[idx + 1])
            elif next_first_half is not None:
                stage_a(j + 1, 0, next_first_half)
            stage_b(j, hh, h, biases.get(h))

    if has_meta:
        for h in range(ratio):
            for hh in range(2):
                b = dmeta_ref[jnp.minimum(qi, 1), hh] if h == 0 else dmeta_ref[1, hh]
                s = _dot(mk_ref[0], qs_ref[hh, :, half_cols(h)]) + jnp.concatenate([b, b], axis=1)
                _flash_update(s, mvt_ref[0, head(hh), :], m_ref, acc_ref, hh, half_cols(h))

    stage_a(0, 0, 0)
    first_diag = qi * ratio
    if has_meta:
        def body(i, carry):
            tile_step(2 * i, 0, {}, 0)
            tile_step(2 * i + 1, 0, {}, 0)
            return carry

        lax.fori_loop(0, jnp.maximum(first_diag - 2, 0) // 2, body, 0)

        @pl.when(qi >= 1)
        def _():
            tile_step(first_diag - 2, 0, {}, 0)
            tile_step(first_diag - 1, 0, {0: dprev_ref}, 0)

    for d in range(ratio):
        biases = {d: ddiag_ref}
        if d + 1 < ratio:
            biases[d + 1] = dprev_ref
        tile_step(first_diag + d, d, biases, d + 1 if d + 1 < ratio else None)

    lp = lam_ref[...]
    lam = (jnp.exp(jnp.sum(lp[0:1] * lp[1:2], axis=-1, keepdims=True))
           - jnp.exp(jnp.sum(lp[2:3] * lp[3:4], axis=-1, keepdims=True)) + lam_init)
    outs = []
    for hh in range(2):
        halves = [_normalized(acc_ref, hh, map_cols(h, 0)) - lam * _normalized(acc_ref, hh, map_cols(h, 1))
                  for h in range(ratio)]
        o = halves[0] if ratio == 1 else jnp.concatenate(halves, axis=1)
        ms = jnp.mean(o * o, axis=0, keepdims=True)
        outs.append(o * lax.rsqrt(ms + EPS))
    o_nat = jnp.concatenate(outs, axis=0).T
    o_ref[0] = ((o_nat * subg_ref[...]) * (1.0 - lam_init)).astype(bf16)


def _diff_attention(dqt, dk, dvt, meta_kv, tiles, l, lam_p, sub_g, lam_init, tq, tk):
    bsz, tlen, _ = dk.shape
    has_meta = meta_kv is not None
    assert tq % tk == 0 and (not has_meta or (tq // tk) % 2 == 0)
    npair = DIFF_HEADS // 2
    in_specs = [
        pl.BlockSpec((1, LANES, tq), lambda b, p, i: (b, p, i)),
        pl.BlockSpec((1, tlen, LANES), lambda b, p, i: (b, 0, p)),
        pl.BlockSpec((1, 2 * LANES, tlen), lambda b, p, i: (b, p, 0)),
    ]
    args = [dqt, dk, dvt]
    if has_meta:
        in_specs += [
            pl.BlockSpec((1, BLK, LANES), lambda b, p, i: (0, 0, p)),
            pl.BlockSpec((1, 2 * LANES, BLK), lambda b, p, i: (0, p, 0)),
            pl.BlockSpec((2, tk, tk), lambda b, p, i: (p, 0, 0)),
            pl.BlockSpec((2, tk, tk), lambda b, p, i: (p, 0, 0)),
            pl.BlockSpec((2, 2, BLK, tk), lambda b, p, i: (0, p, 0, 0)),
        ]
        args += list(meta_kv) + [tiles["dd_diag"], tiles["dd_prev"], tiles["dd_meta"]]
    else:
        in_specs += [pl.BlockSpec((2, tk, tk), lambda b, p, i: (p, 0, 0))]
        args += [tiles["dd_self"]]
    in_specs += [_layer_spec((4, DIFF_QK), l), _layer_spec((1, LANES), l)]
    args += [lam_p, sub_g]
    return pl.pallas_call(
        functools.partial(_diff_kernel, tq=tq, tk=tk, has_meta=has_meta, lam_init=lam_init),
        name="diff_real" if has_meta else "diff_meta",
        out_shape=jax.ShapeDtypeStruct((bsz, tlen, npair * LANES), bf16),
        grid=(bsz, npair, tlen // tq),
        in_specs=in_specs,
        out_specs=pl.BlockSpec((1, tq, LANES), lambda b, p, i: (b, i, p)),
        scratch_shapes=[
            pltpu.VMEM((2, LANES, 2 * tq), bf16),
            pltpu.VMEM((2, 1, 2 * tq), f32),
            pltpu.VMEM((2, LANES, 2 * tq), f32),
            pltpu.VMEM((2, tk, 2 * tq), f32),
        ],
        compiler_params=pltpu.CompilerParams(
            dimension_semantics=("parallel", "parallel", "arbitrary"), vmem_limit_bytes=VMEM_LIMIT),
    )(*args)


def _swa_kernel(*refs, tq, has_band, layer):
    if has_band:
        qt_ref, k_ref, vt_ref, mk_ref, mvt_ref, dband_ref, dmeta_ref, sink_ref, o_ref, sm_ref, sb_ref = refs
    else:
        qt_ref, mk_ref, mvt_ref, dmeta_ref, sink_ref, o_ref, sm_ref = refs
    qi = pl.program_id(1)
    feat = _row((LANES, BLK))
    first_head = _lane((1, 2 * BLK)) < BLK
    n_meta_var = dmeta_ref.shape[0]

    def block(sub):
        blk = qi * (tq // BLK) + sub
        c0 = sub * BLK
        mvar = jnp.minimum(blk, n_meta_var - 1)
        if has_band:
            bvar = jnp.minimum(blk, 1)
            koff = pl.multiple_of(jnp.maximum(blk - 1, 0) * BLK, BLK)
            kband = k_ref[0, pl.ds(koff, 2 * BLK), :]
            vtband = vt_ref[0, :, pl.ds(koff, 2 * BLK)]

        def stage_a(r):
            q2t = qt_ref[0, r * LANES:(r + 1) * LANES, c0:c0 + BLK]
            zero = jnp.zeros_like(q2t)
            qs = jnp.concatenate([jnp.where(feat < SWA_HD, q2t, zero), jnp.where(feat >= SWA_HD, q2t, zero)],
                                 axis=1)
            sm_ref[r] = _dot(mk_ref[0], qs)
            if has_band:
                sb_ref[r] = _dot(kband, qs)

        def stage_b(r):
            h0, h1 = r, SWA_REP + r
            sink = jnp.where(first_head, sink_ref[layer, h0], sink_ref[layer, h1]) * LOG2E
            s_meta = sm_ref[r] + jnp.concatenate([dmeta_ref[mvar, h0], dmeta_ref[mvar, h1]], axis=1)
            m = jnp.maximum(jnp.max(s_meta, axis=0, keepdims=True), sink)
            if has_band:
                s_band = sb_ref[r] + jnp.concatenate([dband_ref[bvar, h0], dband_ref[bvar, h1]], axis=1)
                m = jnp.maximum(m, jnp.max(s_band, axis=0, keepdims=True))
            p_meta = jnp.exp2(s_meta - m)
            denom = jnp.sum(p_meta, axis=0, keepdims=True) + jnp.exp2(sink - m)
            acc = _dot(mvt_ref[0], p_meta.astype(bf16))
            if has_band:
                p_band = jnp.exp2(s_band - m)
                denom = denom + jnp.sum(p_band, axis=0, keepdims=True)
                acc = acc + _dot(vtband, p_band.astype(bf16))
            ot = acc / denom
            pair = jnp.where(feat < SWA_HD, ot[:, 0:BLK], ot[:, BLK:2 * BLK])
            o_ref[0, c0:c0 + BLK, r * LANES:(r + 1) * LANES] = pair.T.astype(bf16)

        return stage_a, stage_b

    stages = [block(sub) for sub in range(tq // BLK)]
    chains = [(sub, r) for sub in range(tq // BLK) for r in range(SWA_REP)]
    stages[0][0](0)
    stages[0][0](1)
    for i, (sub, r) in enumerate(chains):
        if i + 2 < len(chains):
            nsub, nr = chains[i + 2]
            stages[nsub][0](nr)
        stages[sub][1](r)


def _swa_attention(sqt, sk, svt, meta_kv, tiles, l, sinks, tq):
    bsz, width, tlen = sqt.shape
    has_band = meta_kv is not None
    smem = pl.BlockSpec(memory_space=pltpu.SMEM)
    if has_band:
        in_specs = [
            pl.BlockSpec((1, width, tq), lambda b, i: (b, 0, i)),
            pl.BlockSpec((1, tlen, LANES), lambda b, i: (b, 0, 0)),
            pl.BlockSpec((1, LANES, tlen), lambda b, i: (b, 0, 0)),
            _const_spec((1, BLK, LANES)),
            _const_spec((1, LANES, BLK)),
            _const_spec(tiles["ds_band"].shape),
            _const_spec(tiles["ds_meta"].shape),
            smem,
        ]
        args = [sqt, sk, svt, meta_kv[0], meta_kv[1], tiles["ds_band"], tiles["ds_meta"], sinks]
    else:
        in_specs = [
            pl.BlockSpec((1, width, tq), lambda b, i: (b, 0, i)),
            _const_spec((1, BLK, LANES)),
            _const_spec((1, LANES, BLK)),
            _const_spec(tiles["ds_self"].shape),
            smem,
        ]
        args = [sqt, sk, svt, tiles["ds_self"], sinks]
    return pl.pallas_call(
        functools.partial(_swa_kernel, tq=tq, has_band=has_band, layer=l),
        name="swa_real" if has_band else "swa_meta",
        out_shape=jax.ShapeDtypeStruct((bsz, tlen, width), bf16),
        grid=(bsz, tlen // tq),
        in_specs=in_specs,
        out_specs=pl.BlockSpec((1, tq, width), lambda b, i: (b, i, 0)),
        scratch_shapes=[pltpu.VMEM((SWA_REP, BLK, 2 * BLK), f32)]
        + ([pltpu.VMEM((SWA_REP, 2 * BLK, 2 * BLK), f32)] if has_band else []),
        compiler_params=pltpu.CompilerParams(
            dimension_semantics=("parallel", "parallel"), vmem_limit_bytes=VMEM_LIMIT),
    )(*args)


def _out_ffn_kernel(h_ref, ya_ref, yb_ref, yc_ref, wo_ref, g_ref, wg_ref, wu_ref, wd_ref, fg_ref, o_ref, *, final):
    y = jnp.concatenate([ya_ref[0], yb_ref[0], yc_ref[0]], axis=1)
    h1 = h_ref[0] + _dot(y, wo_ref[...])
    hn = _rms(h1, g_ref[...]).astype(bf16)
    gate = _dot(hn, wg_ref[...])
    up = _dot(hn, wu_ref[...])
    act = ((gate * jax.nn.sigmoid(gate)) * up).astype(bf16)
    h2 = h1 + _dot(act, wd_ref[...])
    if final:
        h2 = _rms(h2, fg_ref[...])
    o_ref[0] = h2


def _out_ffn(h, ya, yb, yc, l, lw, final_g, final, tm):
    bsz, t, _ = h.shape

    def row(w):
        return pl.BlockSpec((1, tm, w), lambda b, i: (b, i, 0))

    in_specs = [
        row(D_MODEL), row(384), row(256), row(384),
        _layer_spec((D_MODEL, D_MODEL), l, single=True),
        _layer_spec((1, D_MODEL), l),
        _layer_spec((D_MODEL, D_FF), l, single=True),
        _layer_spec((D_MODEL, D_FF), l, single=True),
        _layer_spec((D_FF, D_MODEL), l, single=True),
        _const_spec((1, D_MODEL)),
    ]
    return pl.pallas_call(
        functools.partial(_out_ffn_kernel, final=final),
        name="out_ffn_real" if bsz > 1 else "out_ffn_meta",
        out_shape=jax.ShapeDtypeStruct((bsz, t, D_MODEL), f32),
        grid=(bsz, t // tm),
        in_specs=in_specs,
        out_specs=row(D_MODEL),
        compiler_params=pltpu.CompilerParams(
            dimension_semantics=("parallel", "parallel"), vmem_limit_bytes=VMEM_LIMIT),
    )(h, ya, yb, yc, lw["w_o"], lw["ffn_norm"], lw["w_gate"], lw["w_up"], lw["w_down"], final_g)


def _rot_half(w):
    half = MLA_ROPE // 2
    return jnp.concatenate([-w[..., half:], w[..., :half]], axis=-1)


def _stacked_weights(attn_norm, w_in, mla_q_norm, mla_w_qb, mla_kv_norm, mla_w_kvb, diff_lambda, diff_subln, w_out,
                     ffn_norm, w_gate, w_up, w_down):
    depth = w_in.shape[0]
    parts, o = {}, 0
    for name, width in (("c_q", 256), ("c_kv", 128), ("k_rope", 32), ("dq", 256), ("dk", 256), ("dv", 256),
                        ("sq", 384), ("sk", 128), ("sv", 128)):
        parts[name] = w_in[:, :, o:o + width]
        o += width
    kr = parts["k_rope"]
    ksw = _rot_half(kr)
    z64 = jnp.zeros((depth, D_MODEL, 64), f32)
    sq = parts["sq"].reshape(depth, D_MODEL, SWA_KV_HEADS, SWA_REP, SWA_HD)
    sq = sq.transpose(0, 1, 3, 2, 4).reshape(depth, D_MODEL, SWA_HEADS * SWA_HD)
    w_nat = jnp.concatenate([parts["c_q"], parts["c_kv"], z64, kr, kr, z64, ksw, ksw, parts["dk"], parts["sk"]],
                            axis=-1)
    w_tr = jnp.swapaxes(jnp.concatenate([parts["dq"], parts["dv"], sq, parts["sv"]], axis=-1), 1, 2)
    assert w_nat.shape == (depth, D_MODEL, D_NAT) and w_tr.shape == (depth, D_TR, D_MODEL)

    wq = mla_w_qb.reshape(depth, MLA_Q_RANK, MLA_HEADS, MLA_NOPE + MLA_ROPE)
    rope = wq[..., MLA_NOPE:]
    w_qbt = jnp.concatenate([wq[..., :MLA_NOPE], rope, _rot_half(rope)], axis=-1)
    w_qbt = jnp.swapaxes(w_qbt.reshape(depth, MLA_Q_RANK, MLA_HEADS * LANES), 1, 2)

    wkv = mla_w_kvb.reshape(depth, MLA_KV_RANK, MLA_HEADS, MLA_NOPE + MLA_V)
    w_kb = jnp.concatenate([wkv[..., :MLA_NOPE], jnp.zeros(wkv.shape[:3] + (LANES - MLA_NOPE,), f32)], axis=-1)
    w_kb = w_kb.reshape(depth, MLA_KV_RANK, MLA_HEADS * LANES)
    w_vt = jnp.swapaxes(wkv[..., MLA_NOPE:].reshape(depth, MLA_KV_RANK, MLA_HEADS * MLA_V), 1, 2)

    n_ab = MLA_HEADS * MLA_V + DIFF_HEADS * DIFF_V
    wo_c = w_out[:, n_ab:].reshape(depth, SWA_KV_HEADS, SWA_REP, SWA_HD, D_MODEL)
    wo_c = wo_c.transpose(0, 2, 1, 3, 4).reshape(depth, SWA_HEADS * SWA_HD, D_MODEL)
    w_o = jnp.concatenate([w_out[:, :n_ab], wo_c], axis=1)
    return {
        "attn_norm": attn_norm[:, None], "w_nat": w_nat.astype(bf16), "w_tr": w_tr.astype(bf16),
        "q_norm": mla_q_norm[:, None], "w_qbt": w_qbt.astype(bf16),
        "kv_norm": mla_kv_norm[:, None], "w_kb": w_kb.astype(bf16), "w_vt": w_vt.astype(bf16),
        "diff_lambda": diff_lambda, "sub_g": jnp.concatenate([diff_subln, diff_subln], axis=-1)[:, None],
        "w_o": w_o.astype(bf16), "ffn_norm": ffn_norm[:, None],
        "w_gate": w_gate.astype(bf16), "w_up": w_up.astype(bf16), "w_down": w_down.astype(bf16),
    }


def _rope_tables(pos):
    inv_freq = ROPE_THETA ** (-jnp.arange(0, MLA_ROPE, 2, dtype=f32) / MLA_ROPE)
    ang = pos.astype(f32)[:, None] * inv_freq[None, :]
    cos, sin = jnp.cos(ang), jnp.sin(ang)
    n = pos.shape[0]
    cc = jnp.concatenate([cos, cos], axis=1)
    ss = jnp.concatenate([sin, sin], axis=1)
    z = jnp.zeros((n, 64), f32)
    return {
        "csqt": (jnp.concatenate([jnp.ones((n, 64), f32), cc, ss], axis=1) * (MLA_SCALE * LOG2E)).T,
        "cka": jnp.concatenate([z, cc, cc], axis=1),
        "ckb": jnp.concatenate([z, ss, ss], axis=1),
    }


def _mixers(proj, meta_proj, tiles, l, lw, lam_init, sinks, real):
    mqt, mk, mvt, dqt, dk, dvt, sq, sk, sv = proj
    lam_p, sub_g = lw["diff_lambda"], lw["sub_g"]
    if real:
        m_mk, m_mvt, m_dk, m_dvt, m_sk, m_sv = (meta_proj[i] for i in (1, 2, 4, 5, 7, 8))
        ya = _mla_attention(mqt, mk, mvt, (m_mk, m_mvt), MLA_QTILE, MLA_KTILE)
        yb = _diff_attention(dqt, dk, dvt, (m_dk, m_dvt), tiles, l, lam_p, sub_g, lam_init, DIFF_QTILE, DIFF_TILE)
        yc = _swa_attention(sq, sk, sv, (m_sk, m_sv), tiles, l, sinks, SWA_QTILE)
    else:
        ya = _mla_attention(mqt, mk, mvt, None, BLK, BLK)
        yb = _diff_attention(dqt, dk, dvt, None, tiles, l, lam_p, sub_g, lam_init, BLK, BLK)
        yc = _swa_attention(sq, sk, sv, None, tiles, l, sinks, BLK)
    return ya, yb, yc


def kernel(x, meta_tokens, rel_bias, attn_norm, w_in, mla_q_norm, mla_w_qb, mla_kv_norm, mla_w_kvb, diff_lambda,
           diff_subln, swa_sinks, w_out, ffn_norm, w_gate, w_up, w_down, final_norm):
    bsz, seq, _ = x.shape
    assert seq % MLA_QTILE == 0 and MLA_QTILE % MLA_KTILE == 0 and seq % DIFF_QTILE == 0
    assert seq % SWA_QTILE == 0 and seq % PROJ_TILE == 0 and seq % ROW_TILE == 0

    h = x
    h_meta = jnp.concatenate([meta_tokens.astype(f32), jnp.zeros((BLK - N_META, D_MODEL), f32)], axis=0)[None]
    tabs = _rope_tables(N_META + jnp.arange(seq))
    tabs_meta = _rope_tables(jnp.minimum(jnp.arange(BLK), N_META - 1))
    tiles = dict(zip(("dd_diag", "dd_prev", "dd_meta", "dd_self", "ds_band", "ds_meta", "ds_self"),
                     _bias_tiles(rel_bias)))
    final_g = final_norm[None]
    lw = _stacked_weights(attn_norm, w_in, mla_q_norm, mla_w_qb, mla_kv_norm, mla_w_kvb, diff_lambda, diff_subln,
                          w_out, ffn_norm, w_gate, w_up, w_down)

    for l in range(DEPTH):
        lam_init = 0.8 - 0.6 * math.exp(-0.3 * l)
        last = l == DEPTH - 1

        meta_proj = _project(h_meta, l, lw, tabs_meta, BLK)
        proj = _project(h, l, lw, tabs, PROJ_TILE)
        if not last:
            y_meta = _mixers(meta_proj, None, tiles, l, lw, lam_init, swa_sinks, real=False)
            h_meta = _out_ffn(h_meta, *y_meta, l, lw, final_g, False, BLK)
        y = _mixers(proj, meta_proj, tiles, l, lw, lam_init, swa_sinks, real=True)
        h = _out_ffn(h, *y, l, lw, final_g, last, ROW_TILE)
    return h
```

```python
import functools
import math

import numpy as np
import jax
import jax.numpy as jnp
from jax import lax
from jax.experimental import pallas as pl
from jax.experimental.pallas import tpu as pltpu

D_MODEL = 1024
DEPTH = 2
N_META = 16
BLK = 128

MLA_HEADS = 6
MLA_Q_RANK = 256
MLA_KV_RANK = 128
MLA_NOPE = 64
MLA_ROPE = 32
MLA_V = 64
ROPE_THETA = 10000.0

DIFF_HEADS = 4
DIFF_QK = 32
DIFF_V = 64

SWA_HEADS = 6
SWA_KV_HEADS = 2
SWA_REP = SWA_HEADS // SWA_KV_HEADS
SWA_HD = 64
WINDOW = 128

REL_BUCKETS = 32
REL_MAX_DIST = 128
D_FF = 2816
NEG_INF = -1e30
EPS = 1e-6
LOG2E = math.log2(math.e)

MLA_SCALE = (MLA_NOPE + MLA_ROPE) ** -0.5
DIFF_SCALE = DIFF_QK ** -0.5
SWA_SCALE = SWA_HD ** -0.5

LANES = 128
HEAD_V = 64


def _spans(widths):
    spans, o = {}, 0
    for name, w in widths:
        spans[name] = slice(o, o + w)
        o += w
    return spans, o


NAT, D_NAT = _spans((("c_q", MLA_Q_RANK), ("c_kv", MLA_KV_RANK), ("rope_a", LANES), ("rope_b", LANES),
                     ("dk", DIFF_HEADS * 2 * DIFF_QK), ("sk", SWA_KV_HEADS * SWA_HD)))
TR, D_TR = _spans((("dq", DIFF_HEADS * 2 * DIFF_QK), ("dv", DIFF_HEADS * DIFF_V), ("sq", SWA_HEADS * SWA_HD),
                   ("sv", SWA_KV_HEADS * SWA_HD)))
MLA_QK_W = MLA_HEADS * LANES
MLA_V_W = MLA_HEADS * MLA_V
VMEM_LIMIT = 56 * 1024 * 1024

MLA_QTILE = 2048
MLA_KTILE = 512
DIFF_QTILE = 1024
DIFF_TILE = 512
SWA_QTILE = 1024
PROJ_TILE = 1024
ROW_TILE = 512

f32 = jnp.float32
bf16 = jnp.bfloat16


def _bucket_thresholds():
    n = np.arange(0, 4 * REL_MAX_DIST)
    max_exact = REL_BUCKETS // 2
    nf = np.maximum(n, max_exact).astype(np.float32)
    large = max_exact + (np.log(nf / np.float32(max_exact)) / np.float32(math.log(REL_MAX_DIST / max_exact))
                         * np.float32(REL_BUCKETS - max_exact)).astype(np.int32)
    bucket = np.where(n < max_exact, n, np.minimum(large, REL_BUCKETS - 1))
    assert np.all(np.diff(bucket) >= 0) and bucket[-1] == REL_BUCKETS - 1
    return [int(np.argmax(bucket >= k)) for k in range(1, REL_BUCKETS)]


BUCKET_THR = _bucket_thresholds()
FAR_DIST = BUCKET_THR[-1]
assert FAR_DIST <= BLK - N_META + 1


def _dot(a, b):
    return jnp.dot(a, b, preferred_element_type=f32)


def _dot_nt(a, b):
    return lax.dot_general(a, b, (((1,), (1,)), ((), ())), preferred_element_type=f32)


def _rms(x, g):
    return (x * lax.rsqrt(jnp.mean(x * x, axis=-1, keepdims=True) + EPS)) * g


def _lane(shape):
    return lax.broadcasted_iota(jnp.int32, shape, len(shape) - 1)


def _row(shape):
    return lax.broadcasted_iota(jnp.int32, shape, 0)


def _bias_lookup(n, tab_ref, h):
    acc = jnp.full(n.shape, tab_ref[0, h], f32)
    for k, thr in enumerate(BUCKET_THR, start=1):
        acc = jnp.where(n >= thr, tab_ref[k, h], acc)
    return acc


def _bias_kernel(tab_ref, dd_diag, dd_prev, dd_meta, dd_self, ds_band, ds_meta, ds_self, *, td):
    rc = 32

    def rows_cols(r0, cols):
        a = r0 + lax.broadcasted_iota(jnp.int32, (rc, cols), 0)
        b = lax.broadcasted_iota(jnp.int32, (rc, cols), 1)
        return a, b

    def diff_body(i, carry, near_prev):
        r0 = pl.multiple_of(i * rc, rc)
        k, q = rows_cols(r0, td)
        for h in range(DIFF_HEADS):
            c = tab_ref[REL_BUCKETS - 1, h]
            dd_diag[h, pl.ds(r0, rc), :] = jnp.where(
                k <= q, (_bias_lookup(jnp.maximum(q - k, 0), tab_ref, h) - c) * LOG2E, NEG_INF)
            if near_prev:
                dd_prev[h, pl.ds(r0, rc), :] = (_bias_lookup(q + td - k, tab_ref, h) - c) * LOG2E
            else:
                dd_prev[h, pl.ds(r0, rc), :] = jnp.zeros((rc, td), f32)
        return carry

    n_far = (td - BLK) // rc
    lax.fori_loop(0, n_far, functools.partial(diff_body, near_prev=False), 0)
    lax.fori_loop(n_far, td // rc, functools.partial(diff_body, near_prev=True), 0)

    def blk_body(i, carry):
        r0 = pl.multiple_of(i * rc, rc)
        j, q = rows_cols(r0, td)
        j2, q2 = rows_cols(r0, BLK)
        self_ok = j2 <= jnp.minimum(q2, N_META - 1)
        for h in range(DIFF_HEADS):
            c = tab_ref[REL_BUCKETS - 1, h]
            dd_meta[0, h, pl.ds(r0, rc), :] = jnp.where(
                j < N_META, (_bias_lookup(N_META + q - j, tab_ref, h) - c) * LOG2E, NEG_INF)
            dd_meta[1, h, pl.ds(r0, rc), :] = jnp.where(j < N_META, 0.0, NEG_INF)
            dd_self[h, pl.ds(r0, rc), :] = jnp.where(
                self_ok, (_bias_lookup(jnp.maximum(q2 - j2, 0), tab_ref, h) - c) * LOG2E, NEG_INF)
        k, a = j2, q2
        for h in range(SWA_HEADS):
            hb = DIFF_HEADS + h
            near = _bias_lookup(jnp.maximum(a - k, 0), tab_ref, hb) * LOG2E
            cur = jnp.where(k <= a, near, NEG_INF)
            prev = jnp.where(k > a, _bias_lookup(a + BLK - k, tab_ref, hb) * LOG2E, NEG_INF)
            ds_band[0, h, pl.ds(r0, rc), :] = cur
            ds_band[0, h, pl.ds(BLK + r0, rc), :] = jnp.full((rc, BLK), NEG_INF, f32)
            ds_band[1, h, pl.ds(r0, rc), :] = prev
            ds_band[1, h, pl.ds(BLK + r0, rc), :] = cur
            ds_meta[0, h, pl.ds(r0, rc), :] = jnp.where(
                k < N_META, _bias_lookup(N_META + a - k, tab_ref, hb) * LOG2E, NEG_INF)
            far = jnp.full((rc, BLK), tab_ref[REL_BUCKETS - 1, hb], f32) * LOG2E
            ds_meta[1, h, pl.ds(r0, rc), :] = jnp.where(k < N_META, far, NEG_INF)
            ds_self[0, h, pl.ds(r0, rc), :] = jnp.where(self_ok, near, NEG_INF)
        return carry

    lax.fori_loop(0, BLK // rc, blk_body, 0)


def _bias_tiles(rel_bias):
    td = DIFF_TILE
    outs = (
        jax.ShapeDtypeStruct((DIFF_HEADS, td, td), f32),
        jax.ShapeDtypeStruct((DIFF_HEADS, td, td), f32),
        jax.ShapeDtypeStruct((2, DIFF_HEADS, BLK, td), f32),
        jax.ShapeDtypeStruct((DIFF_HEADS, BLK, BLK), f32),
        jax.ShapeDtypeStruct((2, SWA_HEADS, 2 * BLK, BLK), f32),
        jax.ShapeDtypeStruct((2, SWA_HEADS, BLK, LANES), f32),
        jax.ShapeDtypeStruct((1, SWA_HEADS, BLK, LANES), f32),
    )
    return pl.pallas_call(
        functools.partial(_bias_kernel, td=td),
        name="bias_tiles",
        out_shape=outs,
        in_specs=[pl.BlockSpec(memory_space=pltpu.SMEM)],
        compiler_params=pltpu.CompilerParams(vmem_limit_bytes=VMEM_LIMIT),
    )(rel_bias)


def _store_vt(out_ref, vt, heads):
    tm = vt.shape[1]
    ones_blk = jnp.where(_row((HEAD_V, tm)) == 0, 1.0, 0.0).astype(bf16)
    for hd in range(heads):
        out_ref[0, hd * LANES:hd * LANES + HEAD_V, :] = vt[hd * HEAD_V:(hd + 1) * HEAD_V].astype(bf16)
        out_ref[0, hd * LANES + HEAD_V:(hd + 1) * LANES, :] = ones_blk


def _proj_kernel(h_ref, g_ref, wnat_ref, wtr_ref, qn_ref, wqbt_ref, kvn_ref, wkb_ref, wvt_ref,
                 csqt_ref, cka_ref, ckb_ref,
                 mqt_ref, mk_ref, mvt_ref, dqt_ref, dk_ref, dvt_ref, sqt_ref, sk_ref, svt_ref):
    hn = _rms(h_ref[0], g_ref[...]).astype(bf16)
    proj = _dot(hn, wnat_ref[...])
    tr = _dot_nt(wtr_ref[...], hn)
    c_q = proj[:, NAT["c_q"]]
    c_kv = proj[:, NAT["c_kv"]]
    rope_a = proj[:, NAT["rope_a"]]
    rope_b = proj[:, NAT["rope_b"]]
    dk_ref[0] = proj[:, NAT["dk"]].astype(bf16)
    sk_ref[0] = proj[:, NAT["sk"]].astype(bf16)
    dqt_ref[0] = (tr[TR["dq"]] * (DIFF_SCALE * LOG2E)).astype(bf16)
    _store_vt(dvt_ref, tr[TR["dv"]], DIFF_HEADS)
    sqt_ref[0] = (tr[TR["sq"]] * (SWA_SCALE * LOG2E)).astype(bf16)
    svt_ref[0] = tr[TR["sv"]].astype(bf16)

    cqn = _rms(c_q, qn_ref[...]).astype(bf16)
    ckvn = _rms(c_kv, kvn_ref[...]).astype(bf16)
    qt = _dot_nt(wqbt_ref[...], cqn)
    kk = _dot(ckvn, wkb_ref[...])
    vt = _dot_nt(wvt_ref[...], ckvn)
    csqt = csqt_ref[...]
    k_rot = rope_a * cka_ref[...] + rope_b * ckb_ref[...]
    for hd in range(MLA_HEADS):
        sl = slice(hd * LANES, (hd + 1) * LANES)
        mqt_ref[0, sl, :] = (qt[sl] * csqt).astype(bf16)
        mk_ref[0, :, sl] = (kk[:, sl] + k_rot).astype(bf16)
    _store_vt(mvt_ref, vt, MLA_HEADS)


def _const_spec(shape):
    nd = len(shape)
    return pl.BlockSpec(shape, lambda *_: (0,) * nd)


def _layer_spec(shape, l, single=False):
    nd = len(shape)
    return pl.BlockSpec((None,) + tuple(shape), lambda *_: (l,) + (0,) * nd,
                        pipeline_mode=pl.Buffered(1) if single else None)


def _project(h, l, lw, tabs, tm):
    bsz, t, _ = h.shape
    width = {name: sp.stop - sp.start for name, sp in {**NAT, **TR}.items()}
    outs = ((MLA_QK_W, True), (MLA_QK_W, False), (MLA_HEADS * LANES, True), (width["dq"], True),
            (width["dk"], False), (DIFF_HEADS * LANES, True), (width["sq"], True), (width["sk"], False),
            (width["sv"], True))
    out_shape = tuple(jax.ShapeDtypeStruct((bsz, w, t) if tr else (bsz, t, w), bf16) for w, tr in outs)
    out_specs = tuple(pl.BlockSpec((1, w, tm), lambda b, i: (b, 0, i)) if tr
                      else pl.BlockSpec((1, tm, w), lambda b, i: (b, i, 0)) for w, tr in outs)
    tab_spec = pl.BlockSpec((tm, LANES), lambda b, i: (i, 0))
    in_specs = [
        pl.BlockSpec((1, tm, D_MODEL), lambda b, i: (b, i, 0)),
        _layer_spec((1, D_MODEL), l),
        _layer_spec((D_MODEL, D_NAT), l),
        _layer_spec((D_TR, D_MODEL), l),
        _layer_spec((1, MLA_Q_RANK), l),
        _layer_spec((MLA_QK_W, MLA_Q_RANK), l),
        _layer_spec((1, MLA_KV_RANK), l),
        _layer_spec((MLA_KV_RANK, MLA_QK_W), l),
        _layer_spec((MLA_V_W, MLA_KV_RANK), l),
        pl.BlockSpec((LANES, tm), lambda b, i: (0, i)),
        tab_spec, tab_spec,
    ]
    return pl.pallas_call(
        _proj_kernel,
        name="proj_real" if bsz > 1 else "proj_meta",
        out_shape=out_shape,
        grid=(bsz, t // tm),
        in_specs=in_specs,
        out_specs=out_specs,
        compiler_params=pltpu.CompilerParams(
            dimension_semantics=("parallel", "parallel"), vmem_limit_bytes=VMEM_LIMIT),
    )(h, lw["attn_norm"], lw["w_nat"], lw["w_tr"], lw["q_norm"], lw["w_qbt"], lw["kv_norm"], lw["w_kb"],
      lw["w_vt"], tabs["csqt"], tabs["cka"], tabs["ckb"])


def _flash_update(s, vt, m_ref, acc_ref, idx, cols=slice(None)):
    m_prev = m_ref[idx, :, cols]
    m_new = jnp.maximum(m_prev, jnp.max(s, axis=0, keepdims=True))
    alpha = jnp.exp2(m_prev - m_new)
    p = jnp.exp2(s - m_new).astype(bf16)
    acc_ref[idx, :, cols] = acc_ref[idx, :, cols] * alpha + _dot(vt, p)
    m_ref[idx, :, cols] = m_new


def _init_stats(m_ref, acc_ref):
    m_ref[...] = jnp.full(m_ref.shape, NEG_INF, f32)
    acc_ref[...] = jnp.zeros(acc_ref.shape, f32)


def _normalized(acc_ref, idx, cols):
    return acc_ref[idx, 0:HEAD_V, cols] / acc_ref[idx, HEAD_V:HEAD_V + 1, cols]


def _mla_kernel(*refs, tq, tk, has_meta):
    if has_meta:
        qt_ref, k_ref, vt_ref, mk_ref, mvt_ref, o_ref, m_ref, acc_ref, s_ref = refs
    else:
        qt_ref, k_ref, vt_ref, o_ref, m_ref, acc_ref, s_ref = refs
    qi = pl.program_id(2)
    ratio = tq // tk
    _init_stats(m_ref, acc_ref)

    def head(hh):
        return slice(hh * LANES, (hh + 1) * LANES)

    if has_meta:
        valid = _row((BLK, tq)) < N_META
        for hh in range(2):
            s = _dot(mk_ref[0, :, head(hh)], qt_ref[0, head(hh), :])
            _flash_update(jnp.where(valid, s, NEG_INF), mvt_ref[0, head(hh), :], m_ref, acc_ref, hh)

    def keys(j):
        return pl.ds(pl.multiple_of(j * tk, tk), tk)

    def stage_a(j, hh, cols):
        s_ref[hh, :, cols] = _dot(k_ref[0, keys(j), head(hh)], qt_ref[0, head(hh), cols])

    def stage_b(j, hh, cols, mask):
        s = s_ref[hh, :, cols]
        if mask is not None:
            s = jnp.where(mask, s, NEG_INF)
        _flash_update(s, vt_ref[0, head(hh), keys(j)], m_ref, acc_ref, hh, cols)

    def blocks(cols):
        return [slice(c, c + tk) for c in range(cols.start, cols.stop, tk)]

    def tile_step(j, cols, mask, next_cols):
        chains = [(hh, cb) for cb in blocks(cols) for hh in range(2)]
        for idx, (hh, cb) in enumerate(chains):
            if idx + 1 < len(chains):
                stage_a(j, *chains[idx + 1])
            elif next_cols is not None:
                stage_a(j + 1, 0, blocks(next_cols)[0])
            chain_mask = None if mask is None else mask[:, cb.start - cols.start:cb.stop - cols.start]
            stage_b(j, hh, cb, chain_mask)

    full = slice(0, tq)
    stage_a(0, 0, blocks(full)[0])
    n_plain = qi * ratio

    def body(i, carry):
        tile_step(2 * i, full, None, full)
        tile_step(2 * i + 1, full, None, full)
        return carry

    lax.fori_loop(0, n_plain // 2, body, 0)
    if ratio % 2 == 1:
        @pl.when(n_plain % 2 == 1)
        def _():
            tile_step(n_plain - 1, full, None, full)

    for d in range(ratio):
        width = tq - d * tk
        key = _row((tk, width))
        qry = _lane((tk, width))
        if has_meta:
            mask = key <= qry
        else:
            mask = key <= jnp.minimum(qry, N_META - 1)
        next_cols = slice((d + 1) * tk, tq) if d + 1 < ratio else None
        tile_step(n_plain + d, slice(d * tk, tq), mask, next_cols)

    ot = jnp.concatenate([_normalized(acc_ref, 0, full), _normalized(acc_ref, 1, full)], axis=0)
    o_ref[0] = ot.T.astype(bf16)


def _mla_attention(mqt, mk, mvt, meta_kv, tq, tk):
    bsz, tlen, _ = mk.shape
    has_meta = meta_kv is not None
    npair = MLA_HEADS // 2
    in_specs = [
        pl.BlockSpec((1, 2 * LANES, tq), lambda b, p, i: (b, p, i)),
        pl.BlockSpec((1, tlen, 2 * LANES), lambda b, p, i: (b, 0, p)),
        pl.BlockSpec((1, 2 * LANES, tlen), lambda b, p, i: (b, p, 0)),
    ]
    args = [mqt, mk, mvt]
    if has_meta:
        in_specs += [
            pl.BlockSpec((1, BLK, 2 * LANES), lambda b, p, i: (0, 0, p)),
            pl.BlockSpec((1, 2 * LANES, BLK), lambda b, p, i: (0, p, 0)),
        ]
        args += list(meta_kv)
    return pl.pallas_call(
        functools.partial(_mla_kernel, tq=tq, tk=tk, has_meta=has_meta),
        name="mla_real" if has_meta else "mla_meta",
        out_shape=jax.ShapeDtypeStruct((bsz, tlen, npair * LANES), bf16),
        grid=(bsz, npair, tlen // tq),
        in_specs=in_specs,
        out_specs=pl.BlockSpec((1, tq, LANES), lambda b, p, i: (b, i, p)),
        scratch_shapes=[
            pltpu.VMEM((2, 1, tq), f32),
            pltpu.VMEM((2, LANES, tq), f32),
            pltpu.VMEM((2, tk, tq), f32),
        ],
        compiler_params=pltpu.CompilerParams(
            dimension_semantics=("parallel", "parallel", "arbitrary"), vmem_limit_bytes=VMEM_LIMIT),
    )(*args)


def _diff_kernel(*refs, tq, tk, has_meta, lam_init):
    if has_meta:
        (qt_ref, k_ref, vt_ref, mk_ref, mvt_ref, ddiag_ref, dprev_ref, dmeta_ref, lam_ref, subg_ref,
         o_ref, qs_ref, m_ref, acc_ref, s_ref) = refs
    else:
        (qt_ref, k_ref, vt_ref, ddiag_ref, lam_ref, subg_ref, o_ref, qs_ref, m_ref, acc_ref, s_ref) = refs
    qi = pl.program_id(2)
    ratio = tq // tk
    seg = 2 * tk
    _init_stats(m_ref, acc_ref)

    def head(hh):
        return slice(hh * LANES, (hh + 1) * LANES)

    def half_cols(h):
        return slice(h * seg, (h + 1) * seg)

    def map_cols(h, c):
        return slice(h * seg + c * tk, h * seg + (c + 1) * tk)

    group = _row((LANES, tk)) // DIFF_QK
    for h in range(ratio):
        qt = qt_ref[0, :, h * tk:(h + 1) * tk]
        for hh in range(2):
            for c in range(2):
                qs_ref[hh, :, map_cols(h, c)] = jnp.where(group == 2 * hh + c, qt, jnp.zeros_like(qt))

    def keys(j):
        return pl.ds(pl.multiple_of(j * tk, tk), tk)

    def stage_a(j, hh, h):
        s_ref[hh, :, half_cols(h)] = _dot(k_ref[0, keys(j), :], qs_ref[hh, :, half_cols(h)])

    def stage_b(j, hh, h, bias_ref):
        s = s_ref[hh, :, half_cols(h)]
        if bias_ref is not None:
            b = bias_ref[hh]
            s = s + jnp.concatenate([b, b], axis=1)
        _flash_update(s, vt_ref[0, head(hh), keys(j)], m_ref, acc_ref, hh, half_cols(h))

    def tile_step(j, first_half, biases, next_first_half):
        chains = [(hh, h) for h in range(first_half, ratio) for hh in range(2)]
        for idx, (hh, h) in enumerate(chains):
            if idx + 1 < len(chains):
                stage_a(j, *chains[idx + 1])
            elif next_first_half is not None:
                stage_a(j + 1, 0, next_first_half)
            stage_b(j, hh, h, biases.get(h))

    if has_meta:
        for h in range(ratio):
            for hh in range(2):
                b = dmeta_ref[jnp.minimum(qi, 1), hh] if h == 0 else dmeta_ref[1, hh]
                s = _dot(mk_ref[0], qs_ref[hh, :, half_cols(h)]) + jnp.concatenate([b, b], axis=1)
                _flash_update(s, mvt_ref[0, head(hh), :], m_ref, acc_ref, hh, half_cols(h))

    stage_a(0, 0, 0)
    first_diag = qi * ratio
    if has_meta:
        def body(i, carry):
            tile_step(2 * i, 0, {}, 0)
            tile_step(2 * i + 1, 0, {}, 0)
            return carry

        lax.fori_loop(0, jnp.maximum(first_diag - 2, 0) // 2, body, 0)

        @pl.when(qi >= 1)
        def _():
            tile_step(first_diag - 2, 0, {}, 0)
            tile_step(first_diag - 1, 0, {0: dprev_ref}, 0)

    for d in range(ratio):
        biases = {d: ddiag_ref}
        if d + 1 < ratio:
            biases[d + 1] = dprev_ref
        tile_step(first_diag + d, d, biases, d + 1 if d + 1 < ratio else None)

    lp = lam_ref[...]
    lam = (jnp.exp(jnp.sum(lp[0:1] * lp[1:2], axis=-1, keepdims=True))
           - jnp.exp(jnp.sum(lp[2:3] * lp[3:4], axis=-1, keepdims=True)) + lam_init)
    outs = []
    for hh in range(2):
        halves = [_normalized(acc_ref, hh, map_cols(h, 0)) - lam * _normalized(acc_ref, hh, map_cols(h, 1))
                  for h in range(ratio)]
        o = halves[0] if ratio == 1 else jnp.concatenate(halves, axis=1)
        ms = jnp.mean(o * o, axis=0, keepdims=True)
        outs.append(o * lax.rsqrt(ms + EPS))
    o_nat = jnp.concatenate(outs, axis=0).T
    o_ref[0] = ((o_nat * subg_ref[...]) * (1.0 - lam_init)).astype(bf16)


def _diff_attention(dqt, dk, dvt, meta_kv, tiles, l, lam_p, sub_g, lam_init, tq, tk):
    bsz, tlen, _ = dk.shape
    has_meta = meta_kv is not None
    assert tq % tk == 0 and (not has_meta or (tq // tk) % 2 == 0)
    npair = DIFF_HEADS // 2
    in_specs = [
        pl.BlockSpec((1, LANES, tq), lambda b, p, i: (b, p, i)),
        pl.BlockSpec((1, tlen, LANES), lambda b, p, i: (b, 0, p)),
        pl.BlockSpec((1, 2 * LANES, tlen), lambda b, p, i: (b, p, 0)),
    ]
    args = [dqt, dk, dvt]
    if has_meta:
        in_specs += [
            pl.BlockSpec((1, BLK, LANES), lambda b, p, i: (0, 0, p)),
            pl.BlockSpec((1, 2 * LANES, BLK), lambda b, p, i: (0, p, 0)),
            pl.BlockSpec((2, tk, tk), lambda b, p, i: (p, 0, 0)),
            pl.BlockSpec((2, tk, tk), lambda b, p, i: (p, 0, 0)),
            pl.BlockSpec((2, 2, BLK, tk), lambda b, p, i: (0, p, 0, 0)),
        ]
        args += list(meta_kv) + [tiles["dd_diag"], tiles["dd_prev"], tiles["dd_meta"]]
    else:
        in_specs += [pl.BlockSpec((2, tk, tk), lambda b, p, i: (p, 0, 0))]
        args += [tiles["dd_self"]]
    in_specs += [_layer_spec((4, DIFF_QK), l), _layer_spec((1, LANES), l)]
    args += [lam_p, sub_g]
    return pl.pallas_call(
        functools.partial(_diff_kernel, tq=tq, tk=tk, has_meta=has_meta, lam_init=lam_init),
        name="diff_real" if has_meta else "diff_meta",
        out_shape=jax.ShapeDtypeStruct((bsz, tlen, npair * LANES), bf16),
        grid=(bsz, npair, tlen // tq),
        in_specs=in_specs,
        out_specs=pl.BlockSpec((1, tq, LANES), lambda b, p, i: (b, i, p)),
        scratch_shapes=[
            pltpu.VMEM((2, LANES, 2 * tq), bf16),
            pltpu.VMEM((2, 1, 2 * tq), f32),
            pltpu.VMEM((2, LANES, 2 * tq), f32),
            pltpu.VMEM((2, tk, 2 * tq), f32),
        ],
        compiler_params=pltpu.CompilerParams(
            dimension_semantics=("parallel", "parallel", "arbitrary"), vmem_limit_bytes=VMEM_LIMIT),
    )(*args)


def _swa_kernel(*refs, tq, has_band, layer):
    if has_band:
        qt_ref, k_ref, vt_ref, mk_ref, mvt_ref, dband_ref, dmeta_ref, sink_ref, o_ref, sm_ref, sb_ref = refs
    else:
        qt_ref, mk_ref, mvt_ref, dmeta_ref, sink_ref, o_ref, sm_ref = refs
    qi = pl.program_id(1)
    feat = _row((LANES, BLK))
    first_head = _lane((1, 2 * BLK)) < BLK
    n_meta_var = dmeta_ref.shape[0]

    def block(sub):
        blk = qi * (tq // BLK) + sub
        c0 = sub * BLK
        mvar = jnp.minimum(blk, n_meta_var - 1)
        if has_band:
            bvar = jnp.minimum(blk, 1)
            koff = pl.multiple_of(jnp.maximum(blk - 1, 0) * BLK, BLK)
            kband = k_ref[0, pl.ds(koff, 2 * BLK), :]
            vtband = vt_ref[0, :, pl.ds(koff, 2 * BLK)]

        def stage_a(r):
            q2t = qt_ref[0, r * LANES:(r + 1) * LANES, c0:c0 + BLK]
            zero = jnp.zeros_like(q2t)
            qs = jnp.concatenate([jnp.where(feat < SWA_HD, q2t, zero), jnp.where(feat >= SWA_HD, q2t, zero)],
                                 axis=1)
            sm_ref[r] = _dot(mk_ref[0], qs)
            if has_band:
                sb_ref[r] = _dot(kband, qs)

        def stage_b(r):
            h0, h1 = r, SWA_REP + r
            sink = jnp.where(first_head, sink_ref[layer, h0], sink_ref[layer, h1]) * LOG2E
            s_meta = sm_ref[r] + jnp.concatenate([dmeta_ref[mvar, h0], dmeta_ref[mvar, h1]], axis=1)
            m = jnp.maximum(jnp.max(s_meta, axis=0, keepdims=True), sink)
            if has_band:
                s_band = sb_ref[r] + jnp.concatenate([dband_ref[bvar, h0], dband_ref[bvar, h1]], axis=1)
                m = jnp.maximum(m, jnp.max(s_band, axis=0, keepdims=True))
            p_meta = jnp.exp2(s_meta - m)
            denom = jnp.sum(p_meta, axis=0, keepdims=True) + jnp.exp2(sink - m)
            acc = _dot(mvt_ref[0], p_meta.astype(bf16))
            if has_band:
                p_band = jnp.exp2(s_band - m)
                denom = denom + jnp.sum(p_band, axis=0, keepdims=True)
                acc = acc + _dot(vtband, p_band.astype(bf16))
            ot = acc / denom
            pair = jnp.where(feat < SWA_HD, ot[:, 0:BLK], ot[:, BLK:2 * BLK])
            o_ref[0, c0:c0 + BLK, r * LANES:(r + 1) * LANES] = pair.T.astype(bf16)

        return stage_a, stage_b

    stages = [block(sub) for sub in range(tq // BLK)]
    chains = [(sub, r) for sub in range(tq // BLK) for r in range(SWA_REP)]
    stages[0][0](0)
    stages[0][0](1)
    for i, (sub, r) in enumerate(chains):
        if i + 2 < len(chains):
            nsub, nr = chains[i + 2]
            stages[nsub][0](nr)
        stages[sub][1](r)


def _swa_attention(sqt, sk, svt, meta_kv, tiles, l, sinks, tq):
    bsz, width, tlen = sqt.shape
    has_band = meta_kv is not None
    smem = pl.BlockSpec(memory_space=pltpu.SMEM)
    if has_band:
        in_specs = [
            pl.BlockSpec((1, width, tq), lambda b, i: (b, 0, i)),
            pl.BlockSpec((1, tlen, LANES), lambda b, i: (b, 0, 0)),
            pl.BlockSpec((1, LANES, tlen), lambda b, i: (b, 0, 0)),
            _const_spec((1, BLK, LANES)),
            _const_spec((1, LANES, BLK)),
            _const_spec(tiles["ds_band"].shape),
            _const_spec(tiles["ds_meta"].shape),
            smem,
        ]
        args = [sqt, sk, svt, meta_kv[0], meta_kv[1], tiles["ds_band"], tiles["ds_meta"], sinks]
    else:
        in_specs = [
            pl.BlockSpec((1, width, tq), lambda b, i: (b, 0, i)),
            _const_spec((1, BLK, LANES)),
            _const_spec((1, LANES, BLK)),
            _const_spec(tiles["ds_self"].shape),
            smem,
        ]
        args = [sqt, sk, svt, tiles["ds_self"], sinks]
    return pl.pallas_call(
        functools.partial(_swa_kernel, tq=tq, has_band=has_band, layer=l),
        name="swa_real" if has_band else "swa_meta",
        out_shape=jax.ShapeDtypeStruct((bsz, tlen, width), bf16),
        grid=(bsz, tlen // tq),
        in_specs=in_specs,
        out_specs=pl.BlockSpec((1, tq, width), lambda b, i: (b, i, 0)),
        scratch_shapes=[pltpu.VMEM((SWA_REP, BLK, 2 * BLK), f32)]
        + ([pltpu.VMEM((SWA_REP, 2 * BLK, 2 * BLK), f32)] if has_band else []),
        compiler_params=pltpu.CompilerParams(
            dimension_semantics=("parallel", "parallel"), vmem_limit_bytes=VMEM_LIMIT),
    )(*args)


def _out_ffn_kernel(h_ref, ya_ref, yb_ref, yc_ref, wo_ref, g_ref, wg_ref, wu_ref, wd_ref, fg_ref, o_ref, *, final):
    y = jnp.concatenate([ya_ref[0], yb_ref[0], yc_ref[0]], axis=1)
    h1 = h_ref[0] + _dot(y, wo_ref[...])
    hn = _rms(h1, g_ref[...]).astype(bf16)
    gate = _dot(hn, wg_ref[...])
    up = _dot(hn, wu_ref[...])
    act = ((gate * jax.nn.sigmoid(gate)) * up).astype(bf16)
    h2 = h1 + _dot(act, wd_ref[...])
    if final:
        h2 = _rms(h2, fg_ref[...])
    o_ref[0] = h2


def _out_ffn(h, ya, yb, yc, l, lw, final_g, final, tm):
    bsz, t, _ = h.shape

    def row(w):
        return pl.BlockSpec((1, tm, w), lambda b, i: (b, i, 0))

    in_specs = [
        row(D_MODEL), row(MLA_HEADS * MLA_V), row(DIFF_HEADS * DIFF_V), row(SWA_HEADS * SWA_HD),
        _layer_spec((D_MODEL, D_MODEL), l, single=True),
        _layer_spec((1, D_MODEL), l),
        _layer_spec((D_MODEL, D_FF), l, single=True),
        _layer_spec((D_MODEL, D_FF), l, single=True),
        _layer_spec((D_FF, D_MODEL), l, single=True),
        _const_spec((1, D_MODEL)),
    ]
    return pl.pallas_call(
        functools.partial(_out_ffn_kernel, final=final),
        name="out_ffn_real" if bsz > 1 else "out_ffn_meta",
        out_shape=jax.ShapeDtypeStruct((bsz, t, D_MODEL), f32),
        grid=(bsz, t // tm),
        in_specs=in_specs,
        out_specs=row(D_MODEL),
        compiler_params=pltpu.CompilerParams(
            dimension_semantics=("parallel", "parallel"), vmem_limit_bytes=VMEM_LIMIT),
    )(h, ya, yb, yc, lw["w_o"], lw["ffn_norm"], lw["w_gate"], lw["w_up"], lw["w_down"], final_g)


def _rot_half(w):
    half = MLA_ROPE // 2
    return jnp.concatenate([-w[..., half:], w[..., :half]], axis=-1)


def _stacked_weights(attn_norm, w_in, mla_q_norm, mla_w_qb, mla_kv_norm, mla_w_kvb, diff_lambda, diff_subln, w_out,
                     ffn_norm, w_gate, w_up, w_down):
    depth = w_in.shape[0]
    parts, o = {}, 0
    for name, width in (("c_q", MLA_Q_RANK), ("c_kv", MLA_KV_RANK), ("k_rope", MLA_ROPE),
                        ("dq", DIFF_HEADS * 2 * DIFF_QK), ("dk", DIFF_HEADS * 2 * DIFF_QK),
                        ("dv", DIFF_HEADS * DIFF_V), ("sq", SWA_HEADS * SWA_HD), ("sk", SWA_KV_HEADS * SWA_HD),
                        ("sv", SWA_KV_HEADS * SWA_HD)):
        parts[name] = w_in[:, :, o:o + width]
        o += width
    kr = parts["k_rope"]
    ksw = _rot_half(kr)
    z64 = jnp.zeros((depth, D_MODEL, MLA_NOPE), f32)
    sq = parts["sq"].reshape(depth, D_MODEL, SWA_KV_HEADS, SWA_REP, SWA_HD)
    sq = sq.transpose(0, 1, 3, 2, 4).reshape(depth, D_MODEL, SWA_HEADS * SWA_HD)
    w_nat = jnp.concatenate([parts["c_q"], parts["c_kv"], z64, kr, kr, z64, ksw, ksw, parts["dk"], parts["sk"]],
                            axis=-1)
    w_tr = jnp.swapaxes(jnp.concatenate([parts["dq"], parts["dv"], sq, parts["sv"]], axis=-1), 1, 2)
    assert w_nat.shape == (depth, D_MODEL, D_NAT) and w_tr.shape == (depth, D_TR, D_MODEL)

    wq = mla_w_qb.reshape(depth, MLA_Q_RANK, MLA_HEADS, MLA_NOPE + MLA_ROPE)
    rope = wq[..., MLA_NOPE:]
    w_qbt = jnp.concatenate([wq[..., :MLA_NOPE], rope, _rot_half(rope)], axis=-1)
    w_qbt = jnp.swapaxes(w_qbt.reshape(depth, MLA_Q_RANK, MLA_HEADS * LANES), 1, 2)

    wkv = mla_w_kvb.reshape(depth, MLA_KV_RANK, MLA_HEADS, MLA_NOPE + MLA_V)
    w_kb = jnp.concatenate([wkv[..., :MLA_NOPE], jnp.zeros(wkv.shape[:3] + (LANES - MLA_NOPE,), f32)], axis=-1)
    w_kb = w_kb.reshape(depth, MLA_KV_RANK, MLA_HEADS * LANES)
    w_vt = jnp.swapaxes(wkv[..., MLA_NOPE:].reshape(depth, MLA_KV_RANK, MLA_HEADS * MLA_V), 1, 2)

    n_ab = MLA_HEADS * MLA_V + DIFF_HEADS * DIFF_V
    wo_c = w_out[:, n_ab:].reshape(depth, SWA_KV_HEADS, SWA_REP, SWA_HD, D_MODEL)
    wo_c = wo_c.transpose(0, 2, 1, 3, 4).reshape(depth, SWA_HEADS * SWA_HD, D_MODEL)
    w_o = jnp.concatenate([w_out[:, :n_ab], wo_c], axis=1)
    return {
        "attn_norm": attn_norm[:, None], "w_nat": w_nat.astype(bf16), "w_tr": w_tr.astype(bf16),
        "q_norm": mla_q_norm[:, None], "w_qbt": w_qbt.astype(bf16),
        "kv_norm": mla_kv_norm[:, None], "w_kb": w_kb.astype(bf16), "w_vt": w_vt.astype(bf16),
        "diff_lambda": diff_lambda, "sub_g": jnp.concatenate([diff_subln, diff_subln], axis=-1)[:, None],
        "w_o": w_o.astype(bf16), "ffn_norm": ffn_norm[:, None],
        "w_gate": w_gate.astype(bf16), "w_up": w_up.astype(bf16), "w_down": w_down.astype(bf16),
    }


def _rope_lane_selectors():
    half = MLA_ROPE // 2
    idx = np.arange(half)
    sel = np.zeros((3, 2 * half, LANES), np.float32)
    for rep in range(2):
        sel[0, idx, MLA_NOPE + rep * half + idx] = 1.0
        sel[0, half + idx, MLA_NOPE + MLA_ROPE + rep * half + idx] = 1.0
        for blk in range(2):
            lane0 = MLA_NOPE + (2 * blk + rep) * half
            sel[1, idx, lane0 + idx] = 1.0
            sel[2, half + idx, lane0 + idx] = 1.0
    return sel


def _rope_tables(pos):
    inv_freq = ROPE_THETA ** (-jnp.arange(0, MLA_ROPE, 2, dtype=f32) / MLA_ROPE)
    ang = pos.astype(f32)[:, None] * inv_freq[None, :]
    cos_sin = jnp.concatenate([jnp.cos(ang), jnp.sin(ang)], axis=1)
    placed = jnp.einsum("nk,tkl->tnl", cos_sin, jnp.asarray(_rope_lane_selectors()),
                        precision=lax.Precision.HIGHEST)
    nope_ones = (np.arange(LANES) < MLA_NOPE).astype(np.float32)[None, :]
    return {
        "csqt": ((placed[0] + nope_ones) * (MLA_SCALE * LOG2E)).T,
        "cka": placed[1],
        "ckb": placed[2],
    }


def _mixers(proj, meta_proj, tiles, l, lw, lam_init, sinks, real):
    mqt, mk, mvt, dqt, dk, dvt, sq, sk, sv = proj
    lam_p, sub_g = lw["diff_lambda"], lw["sub_g"]
    if real:
        m_mk, m_mvt, m_dk, m_dvt, m_sk, m_sv = (meta_proj[i] for i in (1, 2, 4, 5, 7, 8))
        ya = _mla_attention(mqt, mk, mvt, (m_mk, m_mvt), MLA_QTILE, MLA_KTILE)
        yb = _diff_attention(dqt, dk, dvt, (m_dk, m_dvt), tiles, l, lam_p, sub_g, lam_init, DIFF_QTILE, DIFF_TILE)
        yc = _swa_attention(sq, sk, sv, (m_sk, m_sv), tiles, l, sinks, SWA_QTILE)
    else:
        ya = _mla_attention(mqt, mk, mvt, None, BLK, BLK)
        yb = _diff_attention(dqt, dk, dvt, None, tiles, l, lam_p, sub_g, lam_init, BLK, BLK)
        yc = _swa_attention(sq, sk, sv, None, tiles, l, sinks, BLK)
    return ya, yb, yc


def kernel(x, meta_tokens, rel_bias, attn_norm, w_in, mla_q_norm, mla_w_qb, mla_kv_norm, mla_w_kvb, diff_lambda,
           diff_subln, swa_sinks, w_out, ffn_norm, w_gate, w_up, w_down, final_norm):
    bsz, seq, _ = x.shape
    assert seq % MLA_QTILE == 0 and MLA_QTILE % MLA_KTILE == 0 and seq % DIFF_QTILE == 0
    assert seq % SWA_QTILE == 0 and seq % PROJ_TILE == 0 and seq % ROW_TILE == 0

    h = x
    h_meta = jnp.concatenate([meta_tokens.astype(f32), jnp.zeros((BLK - N_META, D_MODEL), f32)], axis=0)[None]
    tabs = _rope_tables(N_META + jnp.arange(seq))
    tabs_meta = _rope_tables(jnp.minimum(jnp.arange(BLK), N_META - 1))
    tiles = dict(zip(("dd_diag", "dd_prev", "dd_meta", "dd_self", "ds_band", "ds_meta", "ds_self"),
                     _bias_tiles(rel_bias)))
    final_g = final_norm[None]
    lw = _stacked_weights(attn_norm, w_in, mla_q_norm, mla_w_qb, mla_kv_norm, mla_w_kvb, diff_lambda, diff_subln,
                          w_out, ffn_norm, w_gate, w_up, w_down)

    for l in range(DEPTH):
        lam_init = 0.8 - 0.6 * math.exp(-0.3 * l)
        last = l == DEPTH - 1

        meta_proj = _project(h_meta, l, lw, tabs_meta, BLK)
        proj = _project(h, l, lw, tabs, PROJ_TILE)
        if not last:
            y_meta = _mixers(meta_proj, None, tiles, l, lw, lam_init, swa_sinks, real=False)
            h_meta = _out_ffn(h_meta, *y_meta, l, lw, final_g, False, BLK)
        y = _mixers(proj, meta_proj, tiles, l, lw, lam_init, swa_sinks, real=True)
        h = _out_ffn(h, *y, l, lw, final_g, last, ROW_TILE)
    return h
```

```python
import functools
import math

import numpy as np
import jax
import jax.numpy as jnp
from jax import lax
from jax.experimental import pallas as pl
from jax.experimental.pallas import tpu as pltpu

D_MODEL = 1024
DEPTH = 2
N_META = 16
BLK = 128

MLA_HEADS = 6
MLA_Q_RANK = 256
MLA_KV_RANK = 128
MLA_NOPE = 64
MLA_ROPE = 32
MLA_V = 64
ROPE_THETA = 10000.0

DIFF_HEADS = 4
DIFF_QK = 32
DIFF_V = 64

SWA_HEADS = 6
SWA_KV_HEADS = 2
SWA_REP = SWA_HEADS // SWA_KV_HEADS
SWA_HD = 64
WINDOW = 128

REL_BUCKETS = 32
REL_MAX_DIST = 128
D_FF = 2816
NEG_INF = -1e30
EPS = 1e-6
LOG2E = math.log2(math.e)

MLA_SCALE = (MLA_NOPE + MLA_ROPE) ** -0.5
DIFF_SCALE = DIFF_QK ** -0.5
SWA_SCALE = SWA_HD ** -0.5

LANES = 128
HEAD_V = 64
BF16_ROWS = 16
ACC_ROWS = -(-(HEAD_V + 1) // BF16_ROWS) * BF16_ROWS


def _spans(widths):
    spans, o = {}, 0
    for name, w in widths:
        spans[name] = slice(o, o + w)
        o += w
    return spans, o


NAT, D_NAT = _spans((("c_q", MLA_Q_RANK), ("c_kv", MLA_KV_RANK), ("rope_a", LANES), ("rope_b", LANES),
                     ("dk", DIFF_HEADS * 2 * DIFF_QK), ("sk", SWA_KV_HEADS * SWA_HD)))
TR, D_TR = _spans((("dq", DIFF_HEADS * 2 * DIFF_QK), ("dv", DIFF_HEADS * DIFF_V), ("sq", SWA_HEADS * SWA_HD),
                   ("sv", SWA_KV_HEADS * SWA_HD)))
MLA_QK_W = MLA_HEADS * LANES
MLA_V_W = MLA_HEADS * MLA_V
VMEM_LIMIT = 56 * 1024 * 1024

MLA_QTILE = 2048
MLA_KTILE = 512
DIFF_QTILE = 1024
DIFF_TILE = 512
SWA_QTILE = 1024
PROJ_TILE = 1024
ROW_TILE = 512

f32 = jnp.float32
bf16 = jnp.bfloat16


def _bucket_thresholds():
    n = np.arange(0, 4 * REL_MAX_DIST)
    max_exact = REL_BUCKETS // 2
    nf = np.maximum(n, max_exact).astype(np.float32)
    large = max_exact + (np.log(nf / np.float32(max_exact)) / np.float32(math.log(REL_MAX_DIST / max_exact))
                         * np.float32(REL_BUCKETS - max_exact)).astype(np.int32)
    bucket = np.where(n < max_exact, n, np.minimum(large, REL_BUCKETS - 1))
    assert np.all(np.diff(bucket) >= 0) and bucket[-1] == REL_BUCKETS - 1
    return [int(np.argmax(bucket >= k)) for k in range(1, REL_BUCKETS)]


BUCKET_THR = _bucket_thresholds()
FAR_DIST = BUCKET_THR[-1]
assert FAR_DIST <= BLK - N_META + 1


def _dot(a, b):
    return jnp.dot(a, b, preferred_element_type=f32)


def _dot_nt(a, b):
    return lax.dot_general(a, b, (((1,), (1,)), ((), ())), preferred_element_type=f32)


def _rms(x, g):
    return (x * lax.rsqrt(jnp.mean(x * x, axis=-1, keepdims=True) + EPS)) * g


def _lane(shape):
    return lax.broadcasted_iota(jnp.int32, shape, len(shape) - 1)


def _row(shape):
    return lax.broadcasted_iota(jnp.int32, shape, 0)


def _bias_lookup(n, tab_ref, h):
    acc = jnp.full(n.shape, tab_ref[0, h], f32)
    for k, thr in enumerate(BUCKET_THR, start=1):
        acc = jnp.where(n >= thr, tab_ref[k, h], acc)
    return acc


def _bias_kernel(tab_ref, dd_diag, dd_prev, dd_meta, dd_self, ds_band, ds_meta, ds_self, *, td):
    rc = 32

    def rows_cols(r0, cols):
        a = r0 + lax.broadcasted_iota(jnp.int32, (rc, cols), 0)
        b = lax.broadcasted_iota(jnp.int32, (rc, cols), 1)
        return a, b

    def diff_body(i, carry, near_prev):
        r0 = pl.multiple_of(i * rc, rc)
        k, q = rows_cols(r0, td)
        for h in range(DIFF_HEADS):
            c = tab_ref[REL_BUCKETS - 1, h]
            dd_diag[h, pl.ds(r0, rc), :] = jnp.where(
                k <= q, (_bias_lookup(jnp.maximum(q - k, 0), tab_ref, h) - c) * LOG2E, NEG_INF)
            if near_prev:
                dd_prev[h, pl.ds(r0, rc), :] = (_bias_lookup(q + td - k, tab_ref, h) - c) * LOG2E
            else:
                dd_prev[h, pl.ds(r0, rc), :] = jnp.zeros((rc, td), f32)
        return carry

    n_far = (td - BLK) // rc
    lax.fori_loop(0, n_far, functools.partial(diff_body, near_prev=False), 0)
    lax.fori_loop(n_far, td // rc, functools.partial(diff_body, near_prev=True), 0)

    def blk_body(i, carry):
        r0 = pl.multiple_of(i * rc, rc)
        j, q = rows_cols(r0, td)
        j2, q2 = rows_cols(r0, BLK)
        self_ok = j2 <= jnp.minimum(q2, N_META - 1)
        for h in range(DIFF_HEADS):
            c = tab_ref[REL_BUCKETS - 1, h]
            dd_meta[0, h, pl.ds(r0, rc), :] = jnp.where(
                j < N_META, (_bias_lookup(N_META + q - j, tab_ref, h) - c) * LOG2E, NEG_INF)
            dd_meta[1, h, pl.ds(r0, rc), :] = jnp.where(j < N_META, 0.0, NEG_INF)
            dd_self[h, pl.ds(r0, rc), :] = jnp.where(
                self_ok, (_bias_lookup(jnp.maximum(q2 - j2, 0), tab_ref, h) - c) * LOG2E, NEG_INF)
        k, a = j2, q2
        for h in range(SWA_HEADS):
            hb = DIFF_HEADS + h
            near = _bias_lookup(jnp.maximum(a - k, 0), tab_ref, hb) * LOG2E
            cur = jnp.where(k <= a, near, NEG_INF)
            prev = jnp.where(k > a, _bias_lookup(a + BLK - k, tab_ref, hb) * LOG2E, NEG_INF)
            ds_band[0, h, pl.ds(r0, rc), :] = cur
            ds_band[0, h, pl.ds(BLK + r0, rc), :] = jnp.full((rc, BLK), NEG_INF, f32)
            ds_band[1, h, pl.ds(r0, rc), :] = prev
            ds_band[1, h, pl.ds(BLK + r0, rc), :] = cur
            ds_meta[0, h, pl.ds(r0, rc), :] = jnp.where(
                k < N_META, _bias_lookup(N_META + a - k, tab_ref, hb) * LOG2E, NEG_INF)
            far = jnp.full((rc, BLK), tab_ref[REL_BUCKETS - 1, hb], f32) * LOG2E
            ds_meta[1, h, pl.ds(r0, rc), :] = jnp.where(k < N_META, far, NEG_INF)
            ds_self[0, h, pl.ds(r0, rc), :] = jnp.where(self_ok, near, NEG_INF)
        return carry

    lax.fori_loop(0, BLK // rc, blk_body, 0)


def _bias_tiles(rel_bias):
    td = DIFF_TILE
    outs = (
        jax.ShapeDtypeStruct((DIFF_HEADS, td, td), f32),
        jax.ShapeDtypeStruct((DIFF_HEADS, td, td), f32),
        jax.ShapeDtypeStruct((2, DIFF_HEADS, BLK, td), f32),
        jax.ShapeDtypeStruct((DIFF_HEADS, BLK, BLK), f32),
        jax.ShapeDtypeStruct((2, SWA_HEADS, 2 * BLK, BLK), f32),
        jax.ShapeDtypeStruct((2, SWA_HEADS, BLK, LANES), f32),
        jax.ShapeDtypeStruct((1, SWA_HEADS, BLK, LANES), f32),
    )
    return pl.pallas_call(
        functools.partial(_bias_kernel, td=td),
        name="bias_tiles",
        out_shape=outs,
        in_specs=[pl.BlockSpec(memory_space=pltpu.SMEM)],
        compiler_params=pltpu.CompilerParams(vmem_limit_bytes=VMEM_LIMIT),
    )(rel_bias)


def _store_vt(out_ref, vt, heads):
    tm = vt.shape[1]
    ones_blk = jnp.where(_row((HEAD_V, tm)) == 0, 1.0, 0.0).astype(bf16)
    for hd in range(heads):
        out_ref[0, hd * LANES:hd * LANES + HEAD_V, :] = vt[hd * HEAD_V:(hd + 1) * HEAD_V].astype(bf16)
        out_ref[0, hd * LANES + HEAD_V:(hd + 1) * LANES, :] = ones_blk


def _proj_kernel(h_ref, g_ref, wnat_ref, wtr_ref, qn_ref, wqbt_ref, kvn_ref, wkb_ref, wvt_ref,
                 csqt_ref, cka_ref, ckb_ref,
                 mqt_ref, mk_ref, mvt_ref, dqt_ref, dk_ref, dvt_ref, sqt_ref, sk_ref, svt_ref):
    hn = _rms(h_ref[0], g_ref[...]).astype(bf16)
    proj = _dot(hn, wnat_ref[...])
    tr = _dot_nt(wtr_ref[...], hn)
    c_q = proj[:, NAT["c_q"]]
    c_kv = proj[:, NAT["c_kv"]]
    rope_a = proj[:, NAT["rope_a"]]
    rope_b = proj[:, NAT["rope_b"]]
    dk_ref[0] = proj[:, NAT["dk"]].astype(bf16)
    sk_ref[0] = proj[:, NAT["sk"]].astype(bf16)
    dqt_ref[0] = (tr[TR["dq"]] * (DIFF_SCALE * LOG2E)).astype(bf16)
    _store_vt(dvt_ref, tr[TR["dv"]], DIFF_HEADS)
    sqt_ref[0] = (tr[TR["sq"]] * (SWA_SCALE * LOG2E)).astype(bf16)
    svt_ref[0] = tr[TR["sv"]].astype(bf16)

    cqn = _rms(c_q, qn_ref[...]).astype(bf16)
    ckvn = _rms(c_kv, kvn_ref[...]).astype(bf16)
    qt = _dot_nt(wqbt_ref[...], cqn)
    kk = _dot(ckvn, wkb_ref[...])
    vt = _dot_nt(wvt_ref[...], ckvn)
    csqt = csqt_ref[...]
    k_rot = rope_a * cka_ref[...] + rope_b * ckb_ref[...]
    for hd in range(MLA_HEADS):
        sl = slice(hd * LANES, (hd + 1) * LANES)
        mqt_ref[0, sl, :] = (qt[sl] * csqt).astype(bf16)
        mk_ref[0, :, sl] = (kk[:, sl] + k_rot).astype(bf16)
    _store_vt(mvt_ref, vt, MLA_HEADS)


def _const_spec(shape):
    nd = len(shape)
    return pl.BlockSpec(shape, lambda *_: (0,) * nd)


def _layer_spec(shape, l, single=False):
    nd = len(shape)
    return pl.BlockSpec((None,) + tuple(shape), lambda *_: (l,) + (0,) * nd,
                        pipeline_mode=pl.Buffered(1) if single else None)


def _project(h, l, lw, tabs, tm):
    bsz, t, _ = h.shape
    width = {name: sp.stop - sp.start for name, sp in {**NAT, **TR}.items()}
    outs = ((MLA_QK_W, True), (MLA_QK_W, False), (MLA_HEADS * LANES, True), (width["dq"], True),
            (width["dk"], False), (DIFF_HEADS * LANES, True), (width["sq"], True), (width["sk"], False),
            (width["sv"], True))
    out_shape = tuple(jax.ShapeDtypeStruct((bsz, w, t) if tr else (bsz, t, w), bf16) for w, tr in outs)
    out_specs = tuple(pl.BlockSpec((1, w, tm), lambda b, i: (b, 0, i)) if tr
                      else pl.BlockSpec((1, tm, w), lambda b, i: (b, i, 0)) for w, tr in outs)
    tab_spec = pl.BlockSpec((tm, LANES), lambda b, i: (i, 0))
    in_specs = [
        pl.BlockSpec((1, tm, D_MODEL), lambda b, i: (b, i, 0)),
        _layer_spec((1, D_MODEL), l),
        _layer_spec((D_MODEL, D_NAT), l),
        _layer_spec((D_TR, D_MODEL), l),
        _layer_spec((1, MLA_Q_RANK), l),
        _layer_spec((MLA_QK_W, MLA_Q_RANK), l),
        _layer_spec((1, MLA_KV_RANK), l),
        _layer_spec((MLA_KV_RANK, MLA_QK_W), l),
        _layer_spec((MLA_V_W, MLA_KV_RANK), l),
        pl.BlockSpec((LANES, tm), lambda b, i: (0, i)),
        tab_spec, tab_spec,
    ]
    return pl.pallas_call(
        _proj_kernel,
        name="proj_real" if bsz > 1 else "proj_meta",
        out_shape=out_shape,
        grid=(bsz, t // tm),
        in_specs=in_specs,
        out_specs=out_specs,
        compiler_params=pltpu.CompilerParams(
            dimension_semantics=("parallel", "parallel"), vmem_limit_bytes=VMEM_LIMIT),
    )(h, lw["attn_norm"], lw["w_nat"], lw["w_tr"], lw["q_norm"], lw["w_qbt"], lw["kv_norm"], lw["w_kb"],
      lw["w_vt"], tabs["csqt"], tabs["cka"], tabs["ckb"])


def _flash_update(s, vt, m_ref, acc_ref, idx, cols=slice(None)):
    m_prev = m_ref[idx, :, cols]
    m_new = jnp.maximum(m_prev, jnp.max(s, axis=0, keepdims=True))
    alpha = jnp.exp2(m_prev - m_new)
    p = jnp.exp2(s - m_new).astype(bf16)
    rows = slice(0, ACC_ROWS)
    acc_ref[idx, rows, cols] = acc_ref[idx, rows, cols] * alpha + _dot(vt[rows], p)
    m_ref[idx, :, cols] = m_new


def _init_stats(m_ref, acc_ref):
    m_ref[...] = jnp.full(m_ref.shape, NEG_INF, f32)
    acc_ref[...] = jnp.zeros(acc_ref.shape, f32)


def _normalized(acc_ref, idx, cols):
    return acc_ref[idx, 0:HEAD_V, cols] / acc_ref[idx, HEAD_V:HEAD_V + 1, cols]


def _mla_kernel(*refs, tq, tk, has_meta):
    if has_meta:
        qt_ref, k_ref, vt_ref, mk_ref, mvt_ref, o_ref, m_ref, acc_ref, s_ref = refs
    else:
        qt_ref, k_ref, vt_ref, o_ref, m_ref, acc_ref, s_ref = refs
    qi = pl.program_id(2)
    ratio = tq // tk
    _init_stats(m_ref, acc_ref)

    def head(hh):
        return slice(hh * LANES, (hh + 1) * LANES)

    if has_meta:
        for hh in range(2):
            s = _dot(mk_ref[0, 0:N_META, head(hh)], qt_ref[0, head(hh), :])
            _flash_update(s, mvt_ref[0, head(hh), 0:N_META], m_ref, acc_ref, hh)

    def keys(j):
        return pl.ds(pl.multiple_of(j * tk, tk), tk)

    def stage_a(j, hh, cols):
        s_ref[hh, :, cols] = _dot(k_ref[0, keys(j), head(hh)], qt_ref[0, head(hh), cols])

    def stage_b(j, hh, cols, mask):
        s = s_ref[hh, :, cols]
        if mask is not None:
            s = jnp.where(mask, s, NEG_INF)
        _flash_update(s, vt_ref[0, head(hh), keys(j)], m_ref, acc_ref, hh, cols)

    def blocks(cols):
        return [slice(c, c + tk) for c in range(cols.start, cols.stop, tk)]

    def tile_step(j, cols, mask, next_cols):
        chains = [(hh, cb) for cb in blocks(cols) for hh in range(2)]
        for idx, (hh, cb) in enumerate(chains):
            if idx + 1 < len(chains):
                stage_a(j, *chains[idx + 1])
            elif next_cols is not None:
                stage_a(j + 1, 0, blocks(next_cols)[0])
            chain_mask = None if mask is None else mask[:, cb.start - cols.start:cb.stop - cols.start]
            stage_b(j, hh, cb, chain_mask)

    full = slice(0, tq)
    stage_a(0, 0, blocks(full)[0])
    n_plain = qi * ratio

    def body(i, carry):
        tile_step(2 * i, full, None, full)
        tile_step(2 * i + 1, full, None, full)
        return carry

    lax.fori_loop(0, n_plain // 2, body, 0)
    if ratio % 2 == 1:
        @pl.when(n_plain % 2 == 1)
        def _():
            tile_step(n_plain - 1, full, None, full)

    for d in range(ratio):
        width = tq - d * tk
        key = _row((tk, width))
        qry = _lane((tk, width))
        if has_meta:
            mask = key <= qry
        else:
            mask = key <= jnp.minimum(qry, N_META - 1)
        next_cols = slice((d + 1) * tk, tq) if d + 1 < ratio else None
        tile_step(n_plain + d, slice(d * tk, tq), mask, next_cols)

    ot = jnp.concatenate([_normalized(acc_ref, 0, full), _normalized(acc_ref, 1, full)], axis=0)
    o_ref[0] = ot.T.astype(bf16)


def _mla_attention(mqt, mk, mvt, meta_kv, tq, tk):
    bsz, tlen, _ = mk.shape
    has_meta = meta_kv is not None
    npair = MLA_HEADS // 2
    in_specs = [
        pl.BlockSpec((1, 2 * LANES, tq), lambda b, p, i: (b, p, i)),
        pl.BlockSpec((1, tlen, 2 * LANES), lambda b, p, i: (b, 0, p)),
        pl.BlockSpec((1, 2 * LANES, tlen), lambda b, p, i: (b, p, 0)),
    ]
    args = [mqt, mk, mvt]
    if has_meta:
        in_specs += [
            pl.BlockSpec((1, BLK, 2 * LANES), lambda b, p, i: (0, 0, p)),
            pl.BlockSpec((1, 2 * LANES, BLK), lambda b, p, i: (0, p, 0)),
        ]
        args += list(meta_kv)
    return pl.pallas_call(
        functools.partial(_mla_kernel, tq=tq, tk=tk, has_meta=has_meta),
        name="mla_real" if has_meta else "mla_meta",
        out_shape=jax.ShapeDtypeStruct((bsz, tlen, npair * LANES), bf16),
        grid=(bsz, npair, tlen // tq),
        in_specs=in_specs,
        out_specs=pl.BlockSpec((1, tq, LANES), lambda b, p, i: (b, i, p)),
        scratch_shapes=[
            pltpu.VMEM((2, 1, tq), f32),
            pltpu.VMEM((2, LANES, tq), f32),
            pltpu.VMEM((2, tk, tq), f32),
        ],
        compiler_params=pltpu.CompilerParams(
            dimension_semantics=("parallel", "parallel", "arbitrary"), vmem_limit_bytes=VMEM_LIMIT),
    )(*args)


def _diff_kernel(*refs, tq, tk, has_meta, lam_init):
    if has_meta:
        (qt_ref, k_ref, vt_ref, mk_ref, mvt_ref, ddiag_ref, dprev_ref, dmeta_ref, lam_ref, subg_ref,
         o_ref, qs_ref, m_ref, acc_ref, s_ref) = refs
    else:
        (qt_ref, k_ref, vt_ref, ddiag_ref, lam_ref, subg_ref, o_ref, qs_ref, m_ref, acc_ref, s_ref) = refs
    qi = pl.program_id(2)
    ratio = tq // tk
    seg = 2 * tk
    _init_stats(m_ref, acc_ref)

    def head(hh):
        return slice(hh * LANES, (hh + 1) * LANES)

    def half_cols(h):
        return slice(h * seg, (h + 1) * seg)

    def map_cols(h, c):
        return slice(h * seg + c * tk, h * seg + (c + 1) * tk)

    group = _row((LANES, tk)) // DIFF_QK
    for h in range(ratio):
        qt = qt_ref[0, :, h * tk:(h + 1) * tk]
        for hh in range(2):
            for c in range(2):
                qs_ref[hh, :, map_cols(h, c)] = jnp.where(group == 2 * hh + c, qt, jnp.zeros_like(qt))

    def keys(j):
        return pl.ds(pl.multiple_of(j * tk, tk), tk)

    def stage_a(j, hh, h):
        s_ref[hh, :, half_cols(h)] = _dot(k_ref[0, keys(j), :], qs_ref[hh, :, half_cols(h)])

    def stage_b(j, hh, h, bias_ref):
        s = s_ref[hh, :, half_cols(h)]
        if bias_ref is not None:
            b = bias_ref[hh]
            s = s + jnp.concatenate([b, b], axis=1)
        _flash_update(s, vt_ref[0, head(hh), keys(j)], m_ref, acc_ref, hh, half_cols(h))

    def tile_step(j, first_half, biases, next_first_half):
        chains = [(hh, h) for h in range(first_half, ratio) for hh in range(2)]
        for idx, (hh, h) in enumerate(chains):
            if idx + 1 < len(chains):
                stage_a(j, *chains[idx + 1])
            elif next_first_half is not None:
                stage_a(j + 1, 0, next_first_half)
            stage_b(j, hh, h, biases.get(h))

    if has_meta:
        for h in range(ratio):
            for hh in range(2):
                b = dmeta_ref[jnp.minimum(qi, 1), hh, 0:N_META] if h == 0 else dmeta_ref[1, hh, 0:N_META]
                s = _dot(mk_ref[0, 0:N_META], qs_ref[hh, :, half_cols(h)]) + jnp.concatenate([b, b], axis=1)
                _flash_update(s, mvt_ref[0, head(hh), 0:N_META], m_ref, acc_ref, hh, half_cols(h))

    stage_a(0, 0, 0)
    first_diag = qi * ratio
    if has_meta:
        def body(i, carry):
            tile_step(2 * i, 0, {}, 0)
            tile_step(2 * i + 1, 0, {}, 0)
            return carry

        lax.fori_loop(0, jnp.maximum(first_diag - 2, 0) // 2, body, 0)

        @pl.when(qi >= 1)
        def _():
            tile_step(first_diag - 2, 0, {}, 0)
            tile_step(first_diag - 1, 0, {0: dprev_ref}, 0)

    for d in range(ratio):
        biases = {d: ddiag_ref}
        if d + 1 < ratio:
            biases[d + 1] = dprev_ref
        tile_step(first_diag + d, d, biases, d + 1 if d + 1 < ratio else None)

    lp = lam_ref[...]
    lam = (jnp.exp(jnp.sum(lp[0:1] * lp[1:2], axis=-1, keepdims=True))
           - jnp.exp(jnp.sum(lp[2:3] * lp[3:4], axis=-1, keepdims=True)) + lam_init)
    outs = []
    for hh in range(2):
        halves = [_normalized(acc_ref, hh, map_cols(h, 0)) - lam * _normalized(acc_ref, hh, map_cols(h, 1))
                  for h in range(ratio)]
        o = halves[0] if ratio == 1 else jnp.concatenate(halves, axis=1)
        ms = jnp.mean(o * o, axis=0, keepdims=True)
        outs.append(o * lax.rsqrt(ms + EPS))
    o_nat = jnp.concatenate(outs, axis=0).T
    o_ref[0] = ((o_nat * subg_ref[...]) * (1.0 - lam_init)).astype(bf16)


def _diff_attention(dqt, dk, dvt, meta_kv, tiles, l, lam_p, sub_g, lam_init, tq, tk):
    bsz, tlen, _ = dk.shape
    has_meta = meta_kv is not None
    assert tq % tk == 0 and (not has_meta or (tq // tk) % 2 == 0)
    npair = DIFF_HEADS // 2
    in_specs = [
        pl.BlockSpec((1, LANES, tq), lambda b, p, i: (b, p, i)),
        pl.BlockSpec((1, tlen, LANES), lambda b, p, i: (b, 0, p)),
        pl.BlockSpec((1, 2 * LANES, tlen), lambda b, p, i: (b, p, 0)),
    ]
    args = [dqt, dk, dvt]
    if has_meta:
        in_specs += [
            pl.BlockSpec((1, BLK, LANES), lambda b, p, i: (0, 0, p)),
            pl.BlockSpec((1, 2 * LANES, BLK), lambda b, p, i: (0, p, 0)),
            pl.BlockSpec((2, tk, tk), lambda b, p, i: (p, 0, 0)),
            pl.BlockSpec((2, tk, tk), lambda b, p, i: (p, 0, 0)),
            pl.BlockSpec((2, 2, BLK, tk), lambda b, p, i: (0, p, 0, 0)),
        ]
        args += list(meta_kv) + [tiles["dd_diag"], tiles["dd_prev"], tiles["dd_meta"]]
    else:
        in_specs += [pl.BlockSpec((2, tk, tk), lambda b, p, i: (p, 0, 0))]
        args += [tiles["dd_self"]]
    in_specs += [_layer_spec((4, DIFF_QK), l), _layer_spec((1, LANES), l)]
    args += [lam_p, sub_g]
    return pl.pallas_call(
        functools.partial(_diff_kernel, tq=tq, tk=tk, has_meta=has_meta, lam_init=lam_init),
        name="diff_real" if has_meta else "diff_meta",
        out_shape=jax.ShapeDtypeStruct((bsz, tlen, npair * LANES), bf16),
        grid=(bsz, npair, tlen // tq),
        in_specs=in_specs,
        out_specs=pl.BlockSpec((1, tq, LANES), lambda b, p, i: (b, i, p)),
        scratch_shapes=[
            pltpu.VMEM((2, LANES, 2 * tq), bf16),
            pltpu.VMEM((2, 1, 2 * tq), f32),
            pltpu.VMEM((2, LANES, 2 * tq), f32),
            pltpu.VMEM((2, tk, 2 * tq), f32),
        ],
        compiler_params=pltpu.CompilerParams(
            dimension_semantics=("parallel", "parallel", "arbitrary"), vmem_limit_bytes=VMEM_LIMIT),
    )(*args)


def _swa_kernel(*refs, tq, has_band, layer):
    if has_band:
        qt_ref, k_ref, vt_ref, mk_ref, mvt_ref, dband_ref, dmeta_ref, sink_ref, o_ref, sm_ref, sb_ref = refs
    else:
        qt_ref, mk_ref, mvt_ref, dmeta_ref, sink_ref, o_ref, sm_ref = refs
    qi = pl.program_id(1)
    feat = _row((LANES, BLK))
    first_head = _lane((1, 2 * BLK)) < BLK
    n_meta_var = dmeta_ref.shape[0]

    def block(sub):
        blk = qi * (tq // BLK) + sub
        c0 = sub * BLK
        mvar = jnp.minimum(blk, n_meta_var - 1)
        if has_band:
            bvar = jnp.minimum(blk, 1)
            koff = pl.multiple_of(jnp.maximum(blk - 1, 0) * BLK, BLK)
            kband = k_ref[0, pl.ds(koff, 2 * BLK), :]
            vtband = vt_ref[0, :, pl.ds(koff, 2 * BLK)]

        def stage_a(r):
            q2t = qt_ref[0, r * LANES:(r + 1) * LANES, c0:c0 + BLK]
            zero = jnp.zeros_like(q2t)
            qs = jnp.concatenate([jnp.where(feat < SWA_HD, q2t, zero), jnp.where(feat >= SWA_HD, q2t, zero)],
                                 axis=1)
            sm_ref[r] = _dot(mk_ref[0], qs)
            if has_band:
                sb_ref[r] = _dot(kband, qs)

        def stage_b(r):
            h0, h1 = r, SWA_REP + r
            sink = jnp.where(first_head, sink_ref[layer, h0], sink_ref[layer, h1]) * LOG2E
            s_meta = sm_ref[r] + jnp.concatenate([dmeta_ref[mvar, h0], dmeta_ref[mvar, h1]], axis=1)
            m = jnp.maximum(jnp.max(s_meta, axis=0, keepdims=True), sink)
            if has_band:
                s_band = sb_ref[r] + jnp.concatenate([dband_ref[bvar, h0], dband_ref[bvar, h1]], axis=1)
                m = jnp.maximum(m, jnp.max(s_band, axis=0, keepdims=True))
            p_meta = jnp.exp2(s_meta - m)
            denom = jnp.sum(p_meta, axis=0, keepdims=True) + jnp.exp2(sink - m)
            acc = _dot(mvt_ref[0], p_meta.astype(bf16))
            if has_band:
                p_band = jnp.exp2(s_band - m)
                denom = denom + jnp.sum(p_band, axis=0, keepdims=True)
                acc = acc + _dot(vtband, p_band.astype(bf16))
            ot = acc / denom
            pair = jnp.where(feat < SWA_HD, ot[:, 0:BLK], ot[:, BLK:2 * BLK])
            o_ref[0, c0:c0 + BLK, r * LANES:(r + 1) * LANES] = pair.T.astype(bf16)

        return stage_a, stage_b

    stages = [block(sub) for sub in range(tq // BLK)]
    chains = [(sub, r) for sub in range(tq // BLK) for r in range(SWA_REP)]
    stages[0][0](0)
    stages[0][0](1)
    for i, (sub, r) in enumerate(chains):
        if i + 2 < len(chains):
            nsub, nr = chains[i + 2]
            stages[nsub][0](nr)
        stages[sub][1](r)


def _swa_attention(sqt, sk, svt, meta_kv, tiles, l, sinks, tq):
    bsz, width, tlen = sqt.shape
    has_band = meta_kv is not None
    smem = pl.BlockSpec(memory_space=pltpu.SMEM)
    if has_band:
        in_specs = [
            pl.BlockSpec((1, width, tq), lambda b, i: (b, 0, i)),
            pl.BlockSpec((1, tlen, LANES), lambda b, i: (b, 0, 0)),
            pl.BlockSpec((1, LANES, tlen), lambda b, i: (b, 0, 0)),
            _const_spec((1, BLK, LANES)),
            _const_spec((1, LANES, BLK)),
            _const_spec(tiles["ds_band"].shape),
            _const_spec(tiles["ds_meta"].shape),
            smem,
        ]
        args = [sqt, sk, svt, meta_kv[0], meta_kv[1], tiles["ds_band"], tiles["ds_meta"], sinks]
    else:
        in_specs = [
            pl.BlockSpec((1, width, tq), lambda b, i: (b, 0, i)),
            _const_spec((1, BLK, LANES)),
            _const_spec((1, LANES, BLK)),
            _const_spec(tiles["ds_self"].shape),
            smem,
        ]
        args = [sqt, sk, svt, tiles["ds_self"], sinks]
    return pl.pallas_call(
        functools.partial(_swa_kernel, tq=tq, has_band=has_band, layer=l),
        name="swa_real" if has_band else "swa_meta",
        out_shape=jax.ShapeDtypeStruct((bsz, tlen, width), bf16),
        grid=(bsz, tlen // tq),
        in_specs=in_specs,
        out_specs=pl.BlockSpec((1, tq, width), lambda b, i: (b, i, 0)),
        scratch_shapes=[pltpu.VMEM((SWA_REP, BLK, 2 * BLK), f32)]
        + ([pltpu.VMEM((SWA_REP, 2 * BLK, 2 * BLK), f32)] if has_band else []),
        compiler_params=pltpu.CompilerParams(
            dimension_semantics=("parallel", "parallel"), vmem_limit_bytes=VMEM_LIMIT),
    )(*args)


def _out_ffn_kernel(h_ref, ya_ref, yb_ref, yc_ref, wo_ref, g_ref, wg_ref, wu_ref, wd_ref, fg_ref, o_ref, *, final):
    y = jnp.concatenate([ya_ref[0], yb_ref[0], yc_ref[0]], axis=1)
    h1 = h_ref[0] + _dot(y, wo_ref[...])
    hn = _rms(h1, g_ref[...]).astype(bf16)
    gate = _dot(hn, wg_ref[...])
    up = _dot(hn, wu_ref[...])
    act = ((gate * jax.nn.sigmoid(gate)) * up).astype(bf16)
    h2 = h1 + _dot(act, wd_ref[...])
    if final:
        h2 = _rms(h2, fg_ref[...])
    o_ref[0] = h2


def _out_ffn(h, ya, yb, yc, l, lw, final_g, final, tm):
    bsz, t, _ = h.shape

    def row(w):
        return pl.BlockSpec((1, tm, w), lambda b, i: (b, i, 0))

    in_specs = [
        row(D_MODEL), row(MLA_HEADS * MLA_V), row(DIFF_HEADS * DIFF_V), row(SWA_HEADS * SWA_HD),
        _layer_spec((D_MODEL, D_MODEL), l, single=True),
        _layer_spec((1, D_MODEL), l),
        _layer_spec((D_MODEL, D_FF), l, single=True),
        _layer_spec((D_MODEL, D_FF), l, single=True),
        _layer_spec((D_FF, D_MODEL), l, single=True),
        _const_spec((1, D_MODEL)),
    ]
    return pl.pallas_call(
        functools.partial(_out_ffn_kernel, final=final),
        name="out_ffn_real" if bsz > 1 else "out_ffn_meta",
        out_shape=jax.ShapeDtypeStruct((bsz, t, D_MODEL), f32),
        grid=(bsz, t // tm),
        in_specs=in_specs,
        out_specs=row(D_MODEL),
        compiler_params=pltpu.CompilerParams(
            dimension_semantics=("parallel", "parallel"), vmem_limit_bytes=VMEM_LIMIT),
    )(h, ya, yb, yc, lw["w_o"], lw["ffn_norm"], lw["w_gate"], lw["w_up"], lw["w_down"], final_g)


def _rot_half(w):
    half = MLA_ROPE // 2
    return jnp.concatenate([-w[..., half:], w[..., :half]], axis=-1)


def _stacked_weights(attn_norm, w_in, mla_q_norm, mla_w_qb, mla_kv_norm, mla_w_kvb, diff_lambda, diff_subln, w_out,
                     ffn_norm, w_gate, w_up, w_down):
    depth = w_in.shape[0]
    parts, o = {}, 0
    for name, width in (("c_q", MLA_Q_RANK), ("c_kv", MLA_KV_RANK), ("k_rope", MLA_ROPE),
                        ("dq", DIFF_HEADS * 2 * DIFF_QK), ("dk", DIFF_HEADS * 2 * DIFF_QK),
                        ("dv", DIFF_HEADS * DIFF_V), ("sq", SWA_HEADS * SWA_HD), ("sk", SWA_KV_HEADS * SWA_HD),
                        ("sv", SWA_KV_HEADS * SWA_HD)):
        parts[name] = w_in[:, :, o:o + width]
        o += width
    kr = parts["k_rope"]
    ksw = _rot_half(kr)
    z64 = jnp.zeros((depth, D_MODEL, MLA_NOPE), f32)
    sq = parts["sq"].reshape(depth, D_MODEL, SWA_KV_HEADS, SWA_REP, SWA_HD)
    sq = sq.transpose(0, 1, 3, 2, 4).reshape(depth, D_MODEL, SWA_HEADS * SWA_HD)
    w_nat = jnp.concatenate([parts["c_q"], parts["c_kv"], z64, kr, kr, z64, ksw, ksw, parts["dk"], parts["sk"]],
                            axis=-1)
    w_tr = jnp.swapaxes(jnp.concatenate([parts["dq"], parts["dv"], sq, parts["sv"]], axis=-1), 1, 2)
    assert w_nat.shape == (depth, D_MODEL, D_NAT) and w_tr.shape == (depth, D_TR, D_MODEL)

    wq = mla_w_qb.reshape(depth, MLA_Q_RANK, MLA_HEADS, MLA_NOPE + MLA_ROPE)
    rope = wq[..., MLA_NOPE:]
    w_qbt = jnp.concatenate([wq[..., :MLA_NOPE], rope, _rot_half(rope)], axis=-1)
    w_qbt = jnp.swapaxes(w_qbt.reshape(depth, MLA_Q_RANK, MLA_HEADS * LANES), 1, 2)

    wkv = mla_w_kvb.reshape(depth, MLA_KV_RANK, MLA_HEADS, MLA_NOPE + MLA_V)
    w_kb = jnp.concatenate([wkv[..., :MLA_NOPE], jnp.zeros(wkv.shape[:3] + (LANES - MLA_NOPE,), f32)], axis=-1)
    w_kb = w_kb.reshape(depth, MLA_KV_RANK, MLA_HEADS * LANES)
    w_vt = jnp.swapaxes(wkv[..., MLA_NOPE:].reshape(depth, MLA_KV_RANK, MLA_HEADS * MLA_V), 1, 2)

    n_ab = MLA_HEADS * MLA_V + DIFF_HEADS * DIFF_V
    wo_c = w_out[:, n_ab:].reshape(depth, SWA_KV_HEADS, SWA_REP, SWA_HD, D_MODEL)
    wo_c = wo_c.transpose(0, 2, 1, 3, 4).reshape(depth, SWA_HEADS * SWA_HD, D_MODEL)
    w_o = jnp.concatenate([w_out[:, :n_ab], wo_c], axis=1)
    return {
        "attn_norm": attn_norm[:, None], "w_nat": w_nat.astype(bf16), "w_tr": w_tr.astype(bf16),
        "q_norm": mla_q_norm[:, None], "w_qbt": w_qbt.astype(bf16),
        "kv_norm": mla_kv_norm[:, None], "w_kb": w_kb.astype(bf16), "w_vt": w_vt.astype(bf16),
        "diff_lambda": diff_lambda, "sub_g": jnp.concatenate([diff_subln, diff_subln], axis=-1)[:, None],
        "w_o": w_o.astype(bf16), "ffn_norm": ffn_norm[:, None],
        "w_gate": w_gate.astype(bf16), "w_up": w_up.astype(bf16), "w_down": w_down.astype(bf16),
    }


def _rope_lane_selectors():
    half = MLA_ROPE // 2
    idx = np.arange(half)
    sel = np.zeros((3, 2 * half, LANES), np.float32)
    for rep in range(2):
        sel[0, idx, MLA_NOPE + rep * half + idx] = 1.0
        sel[0, half + idx, MLA_NOPE + MLA_ROPE + rep * half + idx] = 1.0
        for blk in range(2):
            lane0 = MLA_NOPE + (2 * blk + rep) * half
            sel[1, idx, lane0 + idx] = 1.0
            sel[2, half + idx, lane0 + idx] = 1.0
    return sel


def _rope_tables(pos):
    inv_freq = ROPE_THETA ** (-jnp.arange(0, MLA_ROPE, 2, dtype=f32) / MLA_ROPE)
    ang = pos.astype(f32)[:, None] * inv_freq[None, :]
    cos_sin = jnp.concatenate([jnp.cos(ang), jnp.sin(ang)], axis=1)
    placed = jnp.einsum("nk,tkl->tnl", cos_sin, jnp.asarray(_rope_lane_selectors()),
                        precision=lax.Precision.HIGHEST)
    nope_ones = (np.arange(LANES) < MLA_NOPE).astype(np.float32)[None, :]
    return {
        "csqt": ((placed[0] + nope_ones) * (MLA_SCALE * LOG2E)).T,
        "cka": placed[1],
        "ckb": placed[2],
    }


def _mixers(proj, meta_proj, tiles, l, lw, lam_init, sinks, real):
    mqt, mk, mvt, dqt, dk, dvt, sq, sk, sv = proj
    lam_p, sub_g = lw["diff_lambda"], lw["sub_g"]
    if real:
        m_mk, m_mvt, m_dk, m_dvt, m_sk, m_sv = (meta_proj[i] for i in (1, 2, 4, 5, 7, 8))
        ya = _mla_attention(mqt, mk, mvt, (m_mk, m_mvt), MLA_QTILE, MLA_KTILE)
        yb = _diff_attention(dqt, dk, dvt, (m_dk, m_dvt), tiles, l, lam_p, sub_g, lam_init, DIFF_QTILE, DIFF_TILE)
        yc = _swa_attention(sq, sk, sv, (m_sk, m_sv), tiles, l, sinks, SWA_QTILE)
    else:
        ya = _mla_attention(mqt, mk, mvt, None, BLK, BLK)
        yb = _diff_attention(dqt, dk, dvt, None, tiles, l, lam_p, sub_g, lam_init, BLK, BLK)
        yc = _swa_attention(sq, sk, sv, None, tiles, l, sinks, BLK)
    return ya, yb, yc


def kernel(x, meta_tokens, rel_bias, attn_norm, w_in, mla_q_norm, mla_w_qb, mla_kv_norm, mla_w_kvb, diff_lambda,
           diff_subln, swa_sinks, w_out, ffn_norm, w_gate, w_up, w_down, final_norm):
    bsz, seq, _ = x.shape
    assert seq % MLA_QTILE == 0 and MLA_QTILE % MLA_KTILE == 0 and seq % DIFF_QTILE == 0
    assert seq % SWA_QTILE == 0 and seq % PROJ_TILE == 0 and seq % ROW_TILE == 0

    h = x
    h_meta = jnp.concatenate([meta_tokens.astype(f32), jnp.zeros((BLK - N_META, D_MODEL), f32)], axis=0)[None]
    tabs = _rope_tables(N_META + jnp.arange(seq))
    tabs_meta = _rope_tables(jnp.minimum(jnp.arange(BLK), N_META - 1))
    tiles = dict(zip(("dd_diag", "dd_prev", "dd_meta", "dd_self", "ds_band", "ds_meta", "ds_self"),
                     _bias_tiles(rel_bias)))
    final_g = final_norm[None]
    lw = _stacked_weights(attn_norm, w_in, mla_q_norm, mla_w_qb, mla_kv_norm, mla_w_kvb, diff_lambda, diff_subln,
                          w_out, ffn_norm, w_gate, w_up, w_down)

    for l in range(DEPTH):
        lam_init = 0.8 - 0.6 * math.exp(-0.3 * l)
        last = l == DEPTH - 1

        meta_proj = _project(h_meta, l, lw, tabs_meta, BLK)
        proj = _project(h, l, lw, tabs, PROJ_TILE)
        if not last:
            y_meta = _mixers(meta_proj, None, tiles, l, lw, lam_init, swa_sinks, real=False)
            h_meta = _out_ffn(h_meta, *y_meta, l, lw, final_g, False, BLK)
        y = _mixers(proj, meta_proj, tiles, l, lw, lam_init, swa_sinks, real=True)
        h = _out_ffn(h, *y, l, lw, final_g, last, ROW_TILE)
    return h
```

```python
import functools
import math

import numpy as np
import jax
import jax.numpy as jnp
from jax import lax
from jax.experimental import pallas as pl
from jax.experimental.pallas import tpu as pltpu

D_MODEL = 1024
DEPTH = 2
N_META = 16
BLK = 128

MLA_HEADS = 6
MLA_Q_RANK = 256
MLA_KV_RANK = 128
MLA_NOPE = 64
MLA_ROPE = 32
MLA_V = 64
ROPE_THETA = 10000.0

DIFF_HEADS = 4
DIFF_QK = 32
DIFF_V = 64

SWA_HEADS = 6
SWA_KV_HEADS = 2
SWA_REP = SWA_HEADS // SWA_KV_HEADS
SWA_HD = 64
WINDOW = 128

REL_BUCKETS = 32
REL_MAX_DIST = 128
D_FF = 2816
NEG_INF = -1e30
EPS = 1e-6
LOG2E = math.log2(math.e)

MLA_SCALE = (MLA_NOPE + MLA_ROPE) ** -0.5
DIFF_SCALE = DIFF_QK ** -0.5
SWA_SCALE = SWA_HD ** -0.5

LANES = 128
HEAD_V = 64
BF16_ROWS = 16
ACC_ROWS = -(-(HEAD_V + 1) // BF16_ROWS) * BF16_ROWS


def _spans(widths):
    spans, o = {}, 0
    for name, w in widths:
        spans[name] = slice(o, o + w)
        o += w
    return spans, o


NAT, D_NAT = _spans((("c_q", MLA_Q_RANK), ("c_kv", MLA_KV_RANK), ("rope_a", LANES), ("rope_b", LANES),
                     ("dk", DIFF_HEADS * 2 * DIFF_QK), ("sk", SWA_KV_HEADS * SWA_HD)))
TR, D_TR = _spans((("dq", DIFF_HEADS * 2 * DIFF_QK), ("dv", DIFF_HEADS * DIFF_V), ("sq", SWA_HEADS * SWA_HD),
                   ("sv", SWA_KV_HEADS * SWA_HD)))
MLA_QK_W = MLA_HEADS * LANES
MLA_V_W = MLA_HEADS * MLA_V
VMEM_LIMIT = 56 * 1024 * 1024

MLA_QTILE = 2048
MLA_KTILE = 512
DIFF_QTILE = 1024
DIFF_TILE = 512
SWA_QTILE = 1024
PROJ_TILE = 1024
ROW_TILE = 512

f32 = jnp.float32
bf16 = jnp.bfloat16


def _bucket_thresholds():
    n = np.arange(0, 4 * REL_MAX_DIST)
    max_exact = REL_BUCKETS // 2
    nf = np.maximum(n, max_exact).astype(np.float32)
    large = max_exact + (np.log(nf / np.float32(max_exact)) / np.float32(math.log(REL_MAX_DIST / max_exact))
                         * np.float32(REL_BUCKETS - max_exact)).astype(np.int32)
    bucket = np.where(n < max_exact, n, np.minimum(large, REL_BUCKETS - 1))
    assert np.all(np.diff(bucket) >= 0) and bucket[-1] == REL_BUCKETS - 1
    return [int(np.argmax(bucket >= k)) for k in range(1, REL_BUCKETS)]


BUCKET_THR = _bucket_thresholds()
FAR_DIST = BUCKET_THR[-1]
assert FAR_DIST <= BLK - N_META + 1


def _dot(a, b):
    return jnp.dot(a, b, preferred_element_type=f32)


def _dot_nt(a, b):
    return lax.dot_general(a, b, (((1,), (1,)), ((), ())), preferred_element_type=f32)


def _rms(x, g):
    return (x * lax.rsqrt(jnp.mean(x * x, axis=-1, keepdims=True) + EPS)) * g


def _lane(shape):
    return lax.broadcasted_iota(jnp.int32, shape, len(shape) - 1)


def _row(shape):
    return lax.broadcasted_iota(jnp.int32, shape, 0)


def _bias_lookup(n, tab_ref, h):
    acc = jnp.full(n.shape, tab_ref[0, h], f32)
    for k, thr in enumerate(BUCKET_THR, start=1):
        acc = jnp.where(n >= thr, tab_ref[k, h], acc)
    return acc


def _bias_kernel(tab_ref, dd_diag, dd_prev, dd_meta, dd_self, ds_band, ds_meta, ds_self, *, td):
    rc = 32

    def rows_cols(r0, cols):
        a = r0 + lax.broadcasted_iota(jnp.int32, (rc, cols), 0)
        b = lax.broadcasted_iota(jnp.int32, (rc, cols), 1)
        return a, b

    def diff_body(i, carry, near_prev):
        r0 = pl.multiple_of(i * rc, rc)
        k, q = rows_cols(r0, td)
        for h in range(DIFF_HEADS):
            c = tab_ref[REL_BUCKETS - 1, h]
            dd_diag[h, pl.ds(r0, rc), :] = jnp.where(
                k <= q, (_bias_lookup(jnp.maximum(q - k, 0), tab_ref, h) - c) * LOG2E, NEG_INF)
            if near_prev:
                dd_prev[h, pl.ds(r0, rc), :] = (_bias_lookup(q + td - k, tab_ref, h) - c) * LOG2E
            else:
                dd_prev[h, pl.ds(r0, rc), :] = jnp.zeros((rc, td), f32)
        return carry

    n_far = (td - BLK) // rc
    lax.fori_loop(0, n_far, functools.partial(diff_body, near_prev=False), 0)
    lax.fori_loop(n_far, td // rc, functools.partial(diff_body, near_prev=True), 0)

    def blk_body(i, carry):
        r0 = pl.multiple_of(i * rc, rc)
        j, q = rows_cols(r0, td)
        j2, q2 = rows_cols(r0, BLK)
        self_ok = j2 <= jnp.minimum(q2, N_META - 1)
        for h in range(DIFF_HEADS):
            c = tab_ref[REL_BUCKETS - 1, h]
            dd_meta[0, h, pl.ds(r0, rc), :] = jnp.where(
                j < N_META, (_bias_lookup(N_META + q - j, tab_ref, h) - c) * LOG2E, NEG_INF)
            dd_meta[1, h, pl.ds(r0, rc), :] = jnp.where(j < N_META, 0.0, NEG_INF)
            dd_self[h, pl.ds(r0, rc), :] = jnp.where(
                self_ok, (_bias_lookup(jnp.maximum(q2 - j2, 0), tab_ref, h) - c) * LOG2E, NEG_INF)
        k, a = j2, q2
        for h in range(SWA_HEADS):
            hb = DIFF_HEADS + h
            near = _bias_lookup(jnp.maximum(a - k, 0), tab_ref, hb) * LOG2E
            cur = jnp.where(k <= a, near, NEG_INF)
            prev = jnp.where(k > a, _bias_lookup(a + BLK - k, tab_ref, hb) * LOG2E, NEG_INF)
            ds_band[0, h, pl.ds(r0, rc), :] = cur
            ds_band[0, h, pl.ds(BLK + r0, rc), :] = jnp.full((rc, BLK), NEG_INF, f32)
            ds_band[1, h, pl.ds(r0, rc), :] = prev
            ds_band[1, h, pl.ds(BLK + r0, rc), :] = cur
            ds_meta[0, h, pl.ds(r0, rc), :] = jnp.where(
                k < N_META, _bias_lookup(N_META + a - k, tab_ref, hb) * LOG2E, NEG_INF)
            far = jnp.full((rc, BLK), tab_ref[REL_BUCKETS - 1, hb], f32) * LOG2E
            ds_meta[1, h, pl.ds(r0, rc), :] = jnp.where(k < N_META, far, NEG_INF)
            ds_self[0, h, pl.ds(r0, rc), :] = jnp.where(self_ok, near, NEG_INF)
        return carry

    lax.fori_loop(0, BLK // rc, blk_body, 0)


def _bias_tiles(rel_bias):
    td = DIFF_TILE
    outs = (
        jax.ShapeDtypeStruct((DIFF_HEADS, td, td), f32),
        jax.ShapeDtypeStruct((DIFF_HEADS, td, td), f32),
        jax.ShapeDtypeStruct((2, DIFF_HEADS, BLK, td), f32),
        jax.ShapeDtypeStruct((DIFF_HEADS, BLK, BLK), f32),
        jax.ShapeDtypeStruct((2, SWA_HEADS, 2 * BLK, BLK), f32),
        jax.ShapeDtypeStruct((2, SWA_HEADS, BLK, LANES), f32),
        jax.ShapeDtypeStruct((1, SWA_HEADS, BLK, LANES), f32),
    )
    return pl.pallas_call(
        functools.partial(_bias_kernel, td=td),
        name="bias_tiles",
        out_shape=outs,
        in_specs=[pl.BlockSpec(memory_space=pltpu.SMEM)],
        compiler_params=pltpu.CompilerParams(vmem_limit_bytes=VMEM_LIMIT),
    )(rel_bias)


def _store_vt(out_ref, vt, heads):
    tm = vt.shape[1]
    ones_blk = jnp.where(_row((HEAD_V, tm)) == 0, 1.0, 0.0).astype(bf16)
    for hd in range(heads):
        out_ref[0, hd * LANES:hd * LANES + HEAD_V, :] = vt[hd * HEAD_V:(hd + 1) * HEAD_V].astype(bf16)
        out_ref[0, hd * LANES + HEAD_V:(hd + 1) * LANES, :] = ones_blk


def _proj_kernel(h_ref, g_ref, wnat_ref, wtr_ref, qn_ref, wqbt_ref, kvn_ref, wkb_ref, wvt_ref,
                 csqt_ref, cka_ref, ckb_ref,
                 mqt_ref, mk_ref, mvt_ref, dqt_ref, dk_ref, dvt_ref, sqt_ref, sk_ref, svt_ref):
    hn = _rms(h_ref[0], g_ref[...]).astype(bf16)
    proj = _dot(hn, wnat_ref[...])
    tr = _dot_nt(wtr_ref[...], hn)
    c_q = proj[:, NAT["c_q"]]
    c_kv = proj[:, NAT["c_kv"]]
    rope_a = proj[:, NAT["rope_a"]]
    rope_b = proj[:, NAT["rope_b"]]
    dk_ref[0] = proj[:, NAT["dk"]].astype(bf16)
    sk_ref[0] = proj[:, NAT["sk"]].astype(bf16)
    dqt_ref[0] = (tr[TR["dq"]] * (DIFF_SCALE * LOG2E)).astype(bf16)
    _store_vt(dvt_ref, tr[TR["dv"]], DIFF_HEADS)
    sqt_ref[0] = (tr[TR["sq"]] * (SWA_SCALE * LOG2E)).astype(bf16)
    svt_ref[0] = tr[TR["sv"]].astype(bf16)

    cqn = _rms(c_q, qn_ref[...]).astype(bf16)
    ckvn = _rms(c_kv, kvn_ref[...]).astype(bf16)
    qt = _dot_nt(wqbt_ref[...], cqn)
    kk = _dot(ckvn, wkb_ref[...])
    vt = _dot_nt(wvt_ref[...], ckvn)
    csqt = csqt_ref[...]
    k_rot = rope_a * cka_ref[...] + rope_b * ckb_ref[...]
    for hd in range(MLA_HEADS):
        sl = slice(hd * LANES, (hd + 1) * LANES)
        mqt_ref[0, sl, :] = (qt[sl] * csqt).astype(bf16)
        mk_ref[0, :, sl] = (kk[:, sl] + k_rot).astype(bf16)
    _store_vt(mvt_ref, vt, MLA_HEADS)


def _const_spec(shape):
    nd = len(shape)
    return pl.BlockSpec(shape, lambda *_: (0,) * nd)


def _layer_spec(shape, l, single=False):
    nd = len(shape)
    return pl.BlockSpec((None,) + tuple(shape), lambda *_: (l,) + (0,) * nd,
                        pipeline_mode=pl.Buffered(1) if single else None)


def _project(h, l, lw, tabs, tm):
    bsz, t, _ = h.shape
    width = {name: sp.stop - sp.start for name, sp in {**NAT, **TR}.items()}
    outs = ((MLA_QK_W, True), (MLA_QK_W, False), (MLA_HEADS * LANES, True), (width["dq"], True),
            (width["dk"], False), (DIFF_HEADS * LANES, True), (width["sq"], True), (width["sk"], False),
            (width["sv"], True))
    out_shape = tuple(jax.ShapeDtypeStruct((bsz, w, t) if tr else (bsz, t, w), bf16) for w, tr in outs)
    out_specs = tuple(pl.BlockSpec((1, w, tm), lambda b, i: (b, 0, i)) if tr
                      else pl.BlockSpec((1, tm, w), lambda b, i: (b, i, 0)) for w, tr in outs)
    tab_spec = pl.BlockSpec((tm, LANES), lambda b, i: (i, 0))
    in_specs = [
        pl.BlockSpec((1, tm, D_MODEL), lambda b, i: (b, i, 0)),
        _layer_spec((1, D_MODEL), l),
        _layer_spec((D_MODEL, D_NAT), l),
        _layer_spec((D_TR, D_MODEL), l),
        _layer_spec((1, MLA_Q_RANK), l),
        _layer_spec((MLA_QK_W, MLA_Q_RANK), l),
        _layer_spec((1, MLA_KV_RANK), l),
        _layer_spec((MLA_KV_RANK, MLA_QK_W), l),
        _layer_spec((MLA_V_W, MLA_KV_RANK), l),
        pl.BlockSpec((LANES, tm), lambda b, i: (0, i)),
        tab_spec, tab_spec,
    ]
    return pl.pallas_call(
        _proj_kernel,
        name="proj_real" if bsz > 1 else "proj_meta",
        out_shape=out_shape,
        grid=(bsz, t // tm),
        in_specs=in_specs,
        out_specs=out_specs,
        compiler_params=pltpu.CompilerParams(
            dimension_semantics=("parallel", "parallel"), vmem_limit_bytes=VMEM_LIMIT),
    )(h, lw["attn_norm"], lw["w_nat"], lw["w_tr"], lw["q_norm"], lw["w_qbt"], lw["kv_norm"], lw["w_kb"],
      lw["w_vt"], tabs["csqt"], tabs["cka"], tabs["ckb"])


def _flash_update(s, vt, m_ref, acc_ref, idx, cols=slice(None), n_rows=ACC_ROWS):
    m_prev = m_ref[idx, :, cols]
    m_new = jnp.maximum(m_prev, jnp.max(s, axis=0, keepdims=True))
    alpha = jnp.exp2(m_prev - m_new)
    p = jnp.exp2(s - m_new).astype(bf16)
    rows = slice(0, n_rows)
    acc_ref[idx, rows, cols] = acc_ref[idx, rows, cols] * alpha + _dot(vt[rows], p)
    m_ref[idx, :, cols] = m_new


def _init_stats(m_ref, acc_ref):
    m_ref[...] = jnp.full(m_ref.shape, NEG_INF, f32)
    acc_ref[...] = jnp.zeros(acc_ref.shape, f32)


def _normalized(acc_ref, idx, cols):
    return acc_ref[idx, 0:HEAD_V, cols] / acc_ref[idx, HEAD_V:HEAD_V + 1, cols]


def _mla_kernel(*refs, tq, tk, has_meta):
    if has_meta:
        qt_ref, k_ref, vt_ref, mk_ref, mvt_ref, o_ref, m_ref, acc_ref, s_ref = refs
    else:
        qt_ref, k_ref, vt_ref, o_ref, m_ref, acc_ref, s_ref = refs
    qi = pl.program_id(2)
    ratio = tq // tk
    _init_stats(m_ref, acc_ref)

    def head(hh):
        return slice(hh * LANES, (hh + 1) * LANES)

    if has_meta:
        for hh in range(2):
            s = _dot(mk_ref[0, 0:N_META, head(hh)], qt_ref[0, head(hh), :])
            _flash_update(s, mvt_ref[0, head(hh), 0:N_META], m_ref, acc_ref, hh, n_rows=LANES)

    def keys(j):
        return pl.ds(pl.multiple_of(j * tk, tk), tk)

    def stage_a(j, hh, cols):
        s_ref[hh, :, cols] = _dot(k_ref[0, keys(j), head(hh)], qt_ref[0, head(hh), cols])

    def stage_b(j, hh, cols, mask):
        s = s_ref[hh, :, cols]
        if mask is not None:
            s = jnp.where(mask, s, NEG_INF)
        _flash_update(s, vt_ref[0, head(hh), keys(j)], m_ref, acc_ref, hh, cols, n_rows=LANES)

    def blocks(cols):
        return [slice(c, c + tk) for c in range(cols.start, cols.stop, tk)]

    def tile_step(j, cols, mask, next_cols):
        chains = [(hh, cb) for cb in blocks(cols) for hh in range(2)]
        for idx, (hh, cb) in enumerate(chains):
            if idx + 1 < len(chains):
                stage_a(j, *chains[idx + 1])
            elif next_cols is not None:
                stage_a(j + 1, 0, blocks(next_cols)[0])
            chain_mask = None if mask is None else mask[:, cb.start - cols.start:cb.stop - cols.start]
            stage_b(j, hh, cb, chain_mask)

    full = slice(0, tq)
    stage_a(0, 0, blocks(full)[0])
    n_plain = qi * ratio

    def body(i, carry):
        tile_step(2 * i, full, None, full)
        tile_step(2 * i + 1, full, None, full)
        return carry

    lax.fori_loop(0, n_plain // 2, body, 0)
    if ratio % 2 == 1:
        @pl.when(n_plain % 2 == 1)
        def _():
            tile_step(n_plain - 1, full, None, full)

    for d in range(ratio):
        width = tq - d * tk
        key = _row((tk, width))
        qry = _lane((tk, width))
        if has_meta:
            mask = key <= qry
        else:
            mask = key <= jnp.minimum(qry, N_META - 1)
        next_cols = slice((d + 1) * tk, tq) if d + 1 < ratio else None
        tile_step(n_plain + d, slice(d * tk, tq), mask, next_cols)

    ot = jnp.concatenate([_normalized(acc_ref, 0, full), _normalized(acc_ref, 1, full)], axis=0)
    o_ref[0] = ot.T.astype(bf16)


def _mla_attention(mqt, mk, mvt, meta_kv, tq, tk):
    bsz, tlen, _ = mk.shape
    has_meta = meta_kv is not None
    npair = MLA_HEADS // 2
    in_specs = [
        pl.BlockSpec((1, 2 * LANES, tq), lambda b, p, i: (b, p, i)),
        pl.BlockSpec((1, tlen, 2 * LANES), lambda b, p, i: (b, 0, p)),
        pl.BlockSpec((1, 2 * LANES, tlen), lambda b, p, i: (b, p, 0)),
    ]
    args = [mqt, mk, mvt]
    if has_meta:
        in_specs += [
            pl.BlockSpec((1, BLK, 2 * LANES), lambda b, p, i: (0, 0, p)),
            pl.BlockSpec((1, 2 * LANES, BLK), lambda b, p, i: (0, p, 0)),
        ]
        args += list(meta_kv)
    return pl.pallas_call(
        functools.partial(_mla_kernel, tq=tq, tk=tk, has_meta=has_meta),
        name="mla_real" if has_meta else "mla_meta",
        out_shape=jax.ShapeDtypeStruct((bsz, tlen, npair * LANES), bf16),
        grid=(bsz, npair, tlen // tq),
        in_specs=in_specs,
        out_specs=pl.BlockSpec((1, tq, LANES), lambda b, p, i: (b, i, p)),
        scratch_shapes=[
            pltpu.VMEM((2, 1, tq), f32),
            pltpu.VMEM((2, LANES, tq), f32),
            pltpu.VMEM((2, tk, tq), f32),
        ],
        compiler_params=pltpu.CompilerParams(
            dimension_semantics=("parallel", "parallel", "arbitrary"), vmem_limit_bytes=VMEM_LIMIT),
    )(*args)


def _diff_kernel(*refs, tq, tk, has_meta, lam_init):
    if has_meta:
        (qt_ref, k_ref, vt_ref, mk_ref, mvt_ref, ddiag_ref, dprev_ref, dmeta_ref, lam_ref, subg_ref,
         o_ref, qs_ref, m_ref, acc_ref, s_ref) = refs
    else:
        (qt_ref, k_ref, vt_ref, ddiag_ref, lam_ref, subg_ref, o_ref, qs_ref, m_ref, acc_ref, s_ref) = refs
    qi = pl.program_id(2)
    ratio = tq // tk
    seg = 2 * tk
    _init_stats(m_ref, acc_ref)

    def head(hh):
        return slice(hh * LANES, (hh + 1) * LANES)

    def half_cols(h):
        return slice(h * seg, (h + 1) * seg)

    def map_cols(h, c):
        return slice(h * seg + c * tk, h * seg + (c + 1) * tk)

    group = _row((LANES, tk)) // DIFF_QK
    for h in range(ratio):
        qt = qt_ref[0, :, h * tk:(h + 1) * tk]
        for hh in range(2):
            for c in range(2):
                qs_ref[hh, :, map_cols(h, c)] = jnp.where(group == 2 * hh + c, qt, jnp.zeros_like(qt))

    def keys(j):
        return pl.ds(pl.multiple_of(j * tk, tk), tk)

    def stage_a(j, hh, h):
        s_ref[hh, :, half_cols(h)] = _dot(k_ref[0, keys(j), :], qs_ref[hh, :, half_cols(h)])

    def stage_b(j, hh, h, bias_ref):
        s = s_ref[hh, :, half_cols(h)]
        if bias_ref is not None:
            b = bias_ref[hh]
            s = s + jnp.concatenate([b, b], axis=1)
        _flash_update(s, vt_ref[0, head(hh), keys(j)], m_ref, acc_ref, hh, half_cols(h))

    def tile_step(j, first_half, biases, next_first_half):
        chains = [(hh, h) for h in range(first_half, ratio) for hh in range(2)]
        for idx, (hh, h) in enumerate(chains):
            if idx + 1 < len(chains):
                stage_a(j, *chains[idx + 1])
            elif next_first_half is not None:
                stage_a(j + 1, 0, next_first_half)
            stage_b(j, hh, h, biases.get(h))

    if has_meta:
        for h in range(ratio):
            for hh in range(2):
                b = dmeta_ref[jnp.minimum(qi, 1), hh, 0:N_META] if h == 0 else dmeta_ref[1, hh, 0:N_META]
                s = _dot(mk_ref[0, 0:N_META], qs_ref[hh, :, half_cols(h)]) + jnp.concatenate([b, b], axis=1)
                _flash_update(s, mvt_ref[0, head(hh), 0:N_META], m_ref, acc_ref, hh, half_cols(h))

    stage_a(0, 0, 0)
    first_diag = qi * ratio
    if has_meta:
        def body(i, carry):
            tile_step(2 * i, 0, {}, 0)
            tile_step(2 * i + 1, 0, {}, 0)
            return carry

        lax.fori_loop(0, jnp.maximum(first_diag - 2, 0) // 2, body, 0)

        @pl.when(qi >= 1)
        def _():
            tile_step(first_diag - 2, 0, {}, 0)
            tile_step(first_diag - 1, 0, {0: dprev_ref}, 0)

    for d in range(ratio):
        biases = {d: ddiag_ref}
        if d + 1 < ratio:
            biases[d + 1] = dprev_ref
        tile_step(first_diag + d, d, biases, d + 1 if d + 1 < ratio else None)

    lp = lam_ref[...]
    lam = (jnp.exp(jnp.sum(lp[0:1] * lp[1:2], axis=-1, keepdims=True))
           - jnp.exp(jnp.sum(lp[2:3] * lp[3:4], axis=-1, keepdims=True)) + lam_init)
    outs = []
    for hh in range(2):
        halves = [_normalized(acc_ref, hh, map_cols(h, 0)) - lam * _normalized(acc_ref, hh, map_cols(h, 1))
                  for h in range(ratio)]
        o = halves[0] if ratio == 1 else jnp.concatenate(halves, axis=1)
        ms = jnp.mean(o * o, axis=0, keepdims=True)
        outs.append(o * lax.rsqrt(ms + EPS))
    o_nat = jnp.concatenate(outs, axis=0).T
    o_ref[0] = ((o_nat * subg_ref[...]) * (1.0 - lam_init)).astype(bf16)


def _diff_attention(dqt, dk, dvt, meta_kv, tiles, l, lam_p, sub_g, lam_init, tq, tk):
    bsz, tlen, _ = dk.shape
    has_meta = meta_kv is not None
    assert tq % tk == 0 and (not has_meta or (tq // tk) % 2 == 0)
    npair = DIFF_HEADS // 2
    in_specs = [
        pl.BlockSpec((1, LANES, tq), lambda b, p, i: (b, p, i)),
        pl.BlockSpec((1, tlen, LANES), lambda b, p, i: (b, 0, p)),
        pl.BlockSpec((1, 2 * LANES, tlen), lambda b, p, i: (b, p, 0)),
    ]
    args = [dqt, dk, dvt]
    if has_meta:
        in_specs += [
            pl.BlockSpec((1, BLK, LANES), lambda b, p, i: (0, 0, p)),
            pl.BlockSpec((1, 2 * LANES, BLK), lambda b, p, i: (0, p, 0)),
            pl.BlockSpec((2, tk, tk), lambda b, p, i: (p, 0, 0)),
            pl.BlockSpec((2, tk, tk), lambda b, p, i: (p, 0, 0)),
            pl.BlockSpec((2, 2, BLK, tk), lambda b, p, i: (0, p, 0, 0)),
        ]
        args += list(meta_kv) + [tiles["dd_diag"], tiles["dd_prev"], tiles["dd_meta"]]
    else:
        in_specs += [pl.BlockSpec((2, tk, tk), lambda b, p, i: (p, 0, 0))]
        args += [tiles["dd_self"]]
    in_specs += [_layer_spec((4, DIFF_QK), l), _layer_spec((1, LANES), l)]
    args += [lam_p, sub_g]
    return pl.pallas_call(
        functools.partial(_diff_kernel, tq=tq, tk=tk, has_meta=has_meta, lam_init=lam_init),
        name="diff_real" if has_meta else "diff_meta",
        out_shape=jax.ShapeDtypeStruct((bsz, tlen, npair * LANES), bf16),
        grid=(bsz, npair, tlen // tq),
        in_specs=in_specs,
        out_specs=pl.BlockSpec((1, tq, LANES), lambda b, p, i: (b, i, p)),
        scratch_shapes=[
            pltpu.VMEM((2, LANES, 2 * tq), bf16),
            pltpu.VMEM((2, 1, 2 * tq), f32),
            pltpu.VMEM((2, LANES, 2 * tq), f32),
            pltpu.VMEM((2, tk, 2 * tq), f32),
        ],
        compiler_params=pltpu.CompilerParams(
            dimension_semantics=("parallel", "parallel", "arbitrary"), vmem_limit_bytes=VMEM_LIMIT),
    )(*args)


def _swa_kernel(*refs, tq, has_band, layer):
    if has_band:
        qt_ref, k_ref, vt_ref, mk_ref, mvt_ref, dband_ref, dmeta_ref, sink_ref, o_ref, sm_ref, sb_ref = refs
    else:
        qt_ref, mk_ref, mvt_ref, dmeta_ref, sink_ref, o_ref, sm_ref = refs
    qi = pl.program_id(1)
    feat = _row((LANES, BLK))
    first_head = _lane((1, 2 * BLK)) < BLK
    n_meta_var = dmeta_ref.shape[0]

    def block(sub):
        blk = qi * (tq // BLK) + sub
        c0 = sub * BLK
        mvar = jnp.minimum(blk, n_meta_var - 1)
        if has_band:
            bvar = jnp.minimum(blk, 1)
            koff = pl.multiple_of(jnp.maximum(blk - 1, 0) * BLK, BLK)
            kband = k_ref[0, pl.ds(koff, 2 * BLK), :]
            vtband = vt_ref[0, :, pl.ds(koff, 2 * BLK)]

        def stage_a(r):
            q2t = qt_ref[0, r * LANES:(r + 1) * LANES, c0:c0 + BLK]
            zero = jnp.zeros_like(q2t)
            qs = jnp.concatenate([jnp.where(feat < SWA_HD, q2t, zero), jnp.where(feat >= SWA_HD, q2t, zero)],
                                 axis=1)
            sm_ref[r] = _dot(mk_ref[0], qs)
            if has_band:
                sb_ref[r] = _dot(kband, qs)

        def stage_b(r):
            h0, h1 = r, SWA_REP + r
            sink = jnp.where(first_head, sink_ref[layer, h0], sink_ref[layer, h1]) * LOG2E
            s_meta = sm_ref[r] + jnp.concatenate([dmeta_ref[mvar, h0], dmeta_ref[mvar, h1]], axis=1)
            m = jnp.maximum(jnp.max(s_meta, axis=0, keepdims=True), sink)
            if has_band:
                s_band = sb_ref[r] + jnp.concatenate([dband_ref[bvar, h0], dband_ref[bvar, h1]], axis=1)
                m = jnp.maximum(m, jnp.max(s_band, axis=0, keepdims=True))
            p_meta = jnp.exp2(s_meta - m)
            denom = jnp.sum(p_meta, axis=0, keepdims=True) + jnp.exp2(sink - m)
            acc = _dot(mvt_ref[0], p_meta.astype(bf16))
            if has_band:
                p_band = jnp.exp2(s_band - m)
                denom = denom + jnp.sum(p_band, axis=0, keepdims=True)
                acc = acc + _dot(vtband, p_band.astype(bf16))
            ot = acc / denom
            pair = jnp.where(feat < SWA_HD, ot[:, 0:BLK], ot[:, BLK:2 * BLK])
            o_ref[0, c0:c0 + BLK, r * LANES:(r + 1) * LANES] = pair.T.astype(bf16)

        return stage_a, stage_b

    stages = [block(sub) for sub in range(tq // BLK)]
    chains = [(sub, r) for sub in range(tq // BLK) for r in range(SWA_REP)]
    stages[0][0](0)
    stages[0][0](1)
    for i, (sub, r) in enumerate(chains):
        if i + 2 < len(chains):
            nsub, nr = chains[i + 2]
            stages[nsub][0](nr)
        stages[sub][1](r)


def _swa_attention(sqt, sk, svt, meta_kv, tiles, l, sinks, tq):
    bsz, width, tlen = sqt.shape
    has_band = meta_kv is not None
    smem = pl.BlockSpec(memory_space=pltpu.SMEM)
    if has_band:
        in_specs = [
            pl.BlockSpec((1, width, tq), lambda b, i: (b, 0, i)),
            pl.BlockSpec((1, tlen, LANES), lambda b, i: (b, 0, 0)),
            pl.BlockSpec((1, LANES, tlen), lambda b, i: (b, 0, 0)),
            _const_spec((1, BLK, LANES)),
            _const_spec((1, LANES, BLK)),
            _const_spec(tiles["ds_band"].shape),
            _const_spec(tiles["ds_meta"].shape),
            smem,
        ]
        args = [sqt, sk, svt, meta_kv[0], meta_kv[1], tiles["ds_band"], tiles["ds_meta"], sinks]
    else:
        in_specs = [
            pl.BlockSpec((1, width, tq), lambda b, i: (b, 0, i)),
            _const_spec((1, BLK, LANES)),
            _const_spec((1, LANES, BLK)),
            _const_spec(tiles["ds_self"].shape),
            smem,
        ]
        args = [sqt, sk, svt, tiles["ds_self"], sinks]
    return pl.pallas_call(
        functools.partial(_swa_kernel, tq=tq, has_band=has_band, layer=l),
        name="swa_real" if has_band else "swa_meta",
        out_shape=jax.ShapeDtypeStruct((bsz, tlen, width), bf16),
        grid=(bsz, tlen // tq),
        in_specs=in_specs,
        out_specs=pl.BlockSpec((1, tq, width), lambda b, i: (b, i, 0)),
        scratch_shapes=[pltpu.VMEM((SWA_REP, BLK, 2 * BLK), f32)]
        + ([pltpu.VMEM((SWA_REP, 2 * BLK, 2 * BLK), f32)] if has_band else []),
        compiler_params=pltpu.CompilerParams(
            dimension_semantics=("parallel", "parallel"), vmem_limit_bytes=VMEM_LIMIT),
    )(*args)


def _out_ffn_kernel(h_ref, ya_ref, yb_ref, yc_ref, wo_ref, g_ref, wg_ref, wu_ref, wd_ref, fg_ref, o_ref, *, final):
    y = jnp.concatenate([ya_ref[0], yb_ref[0], yc_ref[0]], axis=1)
    h1 = h_ref[0] + _dot(y, wo_ref[...])
    hn = _rms(h1, g_ref[...]).astype(bf16)
    gate = _dot(hn, wg_ref[...])
    up = _dot(hn, wu_ref[...])
    act = ((gate * jax.nn.sigmoid(gate)) * up).astype(bf16)
    h2 = h1 + _dot(act, wd_ref[...])
    if final:
        h2 = _rms(h2, fg_ref[...])
    o_ref[0] = h2


def _out_ffn(h, ya, yb, yc, l, lw, final_g, final, tm):
    bsz, t, _ = h.shape

    def row(w):
        return pl.BlockSpec((1, tm, w), lambda b, i: (b, i, 0))

    in_specs = [
        row(D_MODEL), row(MLA_HEADS * MLA_V), row(DIFF_HEADS * DIFF_V), row(SWA_HEADS * SWA_HD),
        _layer_spec((D_MODEL, D_MODEL), l, single=True),
        _layer_spec((1, D_MODEL), l),
        _layer_spec((D_MODEL, D_FF), l, single=True),
        _layer_spec((D_MODEL, D_FF), l, single=True),
        _layer_spec((D_FF, D_MODEL), l, single=True),
        _const_spec((1, D_MODEL)),
    ]
    return pl.pallas_call(
        functools.partial(_out_ffn_kernel, final=final),
        name="out_ffn_real" if bsz > 1 else "out_ffn_meta",
        out_shape=jax.ShapeDtypeStruct((bsz, t, D_MODEL), f32),
        grid=(bsz, t // tm),
        in_specs=in_specs,
        out_specs=row(D_MODEL),
        compiler_params=pltpu.CompilerParams(
            dimension_semantics=("parallel", "parallel"), vmem_limit_bytes=VMEM_LIMIT),
    )(h, ya, yb, yc, lw["w_o"], lw["ffn_norm"], lw["w_gate"], lw["w_up"], lw["w_down"], final_g)


def _rot_half(w):
    half = MLA_ROPE // 2
    return jnp.concatenate([-w[..., half:], w[..., :half]], axis=-1)


def _stacked_weights(attn_norm, w_in, mla_q_norm, mla_w_qb, mla_kv_norm, mla_w_kvb, diff_lambda, diff_subln, w_out,
                     ffn_norm, w_gate, w_up, w_down):
    depth = w_in.shape[0]
    parts, o = {}, 0
    for name, width in (("c_q", MLA_Q_RANK), ("c_kv", MLA_KV_RANK), ("k_rope", MLA_ROPE),
                        ("dq", DIFF_HEADS * 2 * DIFF_QK), ("dk", DIFF_HEADS * 2 * DIFF_QK),
                        ("dv", DIFF_HEADS * DIFF_V), ("sq", SWA_HEADS * SWA_HD), ("sk", SWA_KV_HEADS * SWA_HD),
                        ("sv", SWA_KV_HEADS * SWA_HD)):
        parts[name] = w_in[:, :, o:o + width]
        o += width
    kr = parts["k_rope"]
    ksw = _rot_half(kr)
    z64 = jnp.zeros((depth, D_MODEL, MLA_NOPE), f32)
    sq = parts["sq"].reshape(depth, D_MODEL, SWA_KV_HEADS, SWA_REP, SWA_HD)
    sq = sq.transpose(0, 1, 3, 2, 4).reshape(depth, D_MODEL, SWA_HEADS * SWA_HD)
    w_nat = jnp.concatenate([parts["c_q"], parts["c_kv"], z64, kr, kr, z64, ksw, ksw, parts["dk"], parts["sk"]],
                            axis=-1)
    w_tr = jnp.swapaxes(jnp.concatenate([parts["dq"], parts["dv"], sq, parts["sv"]], axis=-1), 1, 2)
    assert w_nat.shape == (depth, D_MODEL, D_NAT) and w_tr.shape == (depth, D_TR, D_MODEL)

    wq = mla_w_qb.reshape(depth, MLA_Q_RANK, MLA_HEADS, MLA_NOPE + MLA_ROPE)
    rope = wq[..., MLA_NOPE:]
    w_qbt = jnp.concatenate([wq[..., :MLA_NOPE], rope, _rot_half(rope)], axis=-1)
    w_qbt = jnp.swapaxes(w_qbt.reshape(depth, MLA_Q_RANK, MLA_HEADS * LANES), 1, 2)

    wkv = mla_w_kvb.reshape(depth, MLA_KV_RANK, MLA_HEADS, MLA_NOPE + MLA_V)
    w_kb = jnp.concatenate([wkv[..., :MLA_NOPE], jnp.zeros(wkv.shape[:3] + (LANES - MLA_NOPE,), f32)], axis=-1)
    w_kb = w_kb.reshape(depth, MLA_KV_RANK, MLA_HEADS * LANES)
    w_vt = jnp.swapaxes(wkv[..., MLA_NOPE:].reshape(depth, MLA_KV_RANK, MLA_HEADS * MLA_V), 1, 2)

    n_ab = MLA_HEADS * MLA_V + DIFF_HEADS * DIFF_V
    wo_c = w_out[:, n_ab:].reshape(depth, SWA_KV_HEADS, SWA_REP, SWA_HD, D_MODEL)
    wo_c = wo_c.transpose(0, 2, 1, 3, 4).reshape(depth, SWA_HEADS * SWA_HD, D_MODEL)
    w_o = jnp.concatenate([w_out[:, :n_ab], wo_c], axis=1)
    return {
        "attn_norm": attn_norm[:, None], "w_nat": w_nat.astype(bf16), "w_tr": w_tr.astype(bf16),
        "q_norm": mla_q_norm[:, None], "w_qbt": w_qbt.astype(bf16),
        "kv_norm": mla_kv_norm[:, None], "w_kb": w_kb.astype(bf16), "w_vt": w_vt.astype(bf16),
        "diff_lambda": diff_lambda, "sub_g": jnp.concatenate([diff_subln, diff_subln], axis=-1)[:, None],
        "w_o": w_o.astype(bf16), "ffn_norm": ffn_norm[:, None],
        "w_gate": w_gate.astype(bf16), "w_up": w_up.astype(bf16), "w_down": w_down.astype(bf16),
    }


def _rope_lane_selectors():
    half = MLA_ROPE // 2
    idx = np.arange(half)
    sel = np.zeros((3, 2 * half, LANES), np.float32)
    for rep in range(2):
        sel[0, idx, MLA_NOPE + rep * half + idx] = 1.0
        sel[0, half + idx, MLA_NOPE + MLA_ROPE + rep * half + idx] = 1.0
        for blk in range(2):
            lane0 = MLA_NOPE + (2 * blk + rep) * half
            sel[1, idx, lane0 + idx] = 1.0
            sel[2, half + idx, lane0 + idx] = 1.0
    return sel


def _rope_tables(pos):
    inv_freq = ROPE_THETA ** (-jnp.arange(0, MLA_ROPE, 2, dtype=f32) / MLA_ROPE)
    ang = pos.astype(f32)[:, None] * inv_freq[None, :]
    cos_sin = jnp.concatenate([jnp.cos(ang), jnp.sin(ang)], axis=1)
    placed = jnp.einsum("nk,tkl->tnl", cos_sin, jnp.asarray(_rope_lane_selectors()),
                        precision=lax.Precision.HIGHEST)
    nope_ones = (np.arange(LANES) < MLA_NOPE).astype(np.float32)[None, :]
    return {
        "csqt": ((placed[0] + nope_ones) * (MLA_SCALE * LOG2E)).T,
        "cka": placed[1],
        "ckb": placed[2],
    }


def _mixers(proj, meta_proj, tiles, l, lw, lam_init, sinks, real):
    mqt, mk, mvt, dqt, dk, dvt, sq, sk, sv = proj
    lam_p, sub_g = lw["diff_lambda"], lw["sub_g"]
    if real:
        m_mk, m_mvt, m_dk, m_dvt, m_sk, m_sv = (meta_proj[i] for i in (1, 2, 4, 5, 7, 8))
        ya = _mla_attention(mqt, mk, mvt, (m_mk, m_mvt), MLA_QTILE, MLA_KTILE)
        yb = _diff_attention(dqt, dk, dvt, (m_dk, m_dvt), tiles, l, lam_p, sub_g, lam_init, DIFF_QTILE, DIFF_TILE)
        yc = _swa_attention(sq, sk, sv, (m_sk, m_sv), tiles, l, sinks, SWA_QTILE)
    else:
        ya = _mla_attention(mqt, mk, mvt, None, BLK, BLK)
        yb = _diff_attention(dqt, dk, dvt, None, tiles, l, lam_p, sub_g, lam_init, BLK, BLK)
        yc = _swa_attention(sq, sk, sv, None, tiles, l, sinks, BLK)
    return ya, yb, yc


def kernel(x, meta_tokens, rel_bias, attn_norm, w_in, mla_q_norm, mla_w_qb, mla_kv_norm, mla_w_kvb, diff_lambda,
           diff_subln, swa_sinks, w_out, ffn_norm, w_gate, w_up, w_down, final_norm):
    bsz, seq, _ = x.shape
    assert seq % MLA_QTILE == 0 and MLA_QTILE % MLA_KTILE == 0 and seq % DIFF_QTILE == 0
    assert seq % SWA_QTILE == 0 and seq % PROJ_TILE == 0 and seq % ROW_TILE == 0

    h = x
    h_meta = jnp.concatenate([meta_tokens.astype(f32), jnp.zeros((BLK - N_META, D_MODEL), f32)], axis=0)[None]
    tabs = _rope_tables(N_META + jnp.arange(seq))
    tabs_meta = _rope_tables(jnp.minimum(jnp.arange(BLK), N_META - 1))
    tiles = dict(zip(("dd_diag", "dd_prev", "dd_meta", "dd_self", "ds_band", "ds_meta", "ds_self"),
                     _bias_tiles(rel_bias)))
    final_g = final_norm[None]
    lw = _stacked_weights(attn_norm, w_in, mla_q_norm, mla_w_qb, mla_kv_norm, mla_w_kvb, diff_lambda, diff_subln,
                          w_out, ffn_norm, w_gate, w_up, w_down)

    for l in range(DEPTH):
        lam_init = 0.8 - 0.6 * math.exp(-0.3 * l)
        last = l == DEPTH - 1

        meta_proj = _project(h_meta, l, lw, tabs_meta, BLK)
        proj = _project(h, l, lw, tabs, PROJ_TILE)
        if not last:
            y_meta = _mixers(meta_proj, None, tiles, l, lw, lam_init, swa_sinks, real=False)
            h_meta = _out_ffn(h_meta, *y_meta, l, lw, final_g, False, BLK)
        y = _mixers(proj, meta_proj, tiles, l, lw, lam_init, swa_sinks, real=True)
        h = _out_ffn(h, *y, l, lw, final_g, last, ROW_TILE)
    return h
```

```python
import functools
import math

import numpy as np
import jax
import jax.numpy as jnp
from jax import lax
from jax.experimental import pallas as pl
from jax.experimental.pallas import tpu as pltpu

D_MODEL = 1024
DEPTH = 2
N_META = 16
BLK = 128

MLA_HEADS = 6
MLA_Q_RANK = 256
MLA_KV_RANK = 128
MLA_NOPE = 64
MLA_ROPE = 32
MLA_V = 64
ROPE_THETA = 10000.0

DIFF_HEADS = 4
DIFF_QK = 32
DIFF_V = 64

SWA_HEADS = 6
SWA_KV_HEADS = 2
SWA_REP = SWA_HEADS // SWA_KV_HEADS
SWA_HD = 64
WINDOW = 128

REL_BUCKETS = 32
REL_MAX_DIST = 128
D_FF = 2816
NEG_INF = -1e30
EPS = 1e-6
LOG2E = math.log2(math.e)

MLA_SCALE = (MLA_NOPE + MLA_ROPE) ** -0.5
DIFF_SCALE = DIFF_QK ** -0.5
SWA_SCALE = SWA_HD ** -0.5

LANES = 128
HEAD_V = 64
BF16_ROWS = 16
ACC_ROWS = -(-(HEAD_V + 1) // BF16_ROWS) * BF16_ROWS


def _spans(widths):
    spans, o = {}, 0
    for name, w in widths:
        spans[name] = slice(o, o + w)
        o += w
    return spans, o


NAT, D_NAT = _spans((("c_q", MLA_Q_RANK), ("c_kv", MLA_KV_RANK), ("rope_a", LANES), ("rope_b", LANES),
                     ("dk", DIFF_HEADS * 2 * DIFF_QK), ("sk", SWA_KV_HEADS * SWA_HD)))
TR, D_TR = _spans((("dq", DIFF_HEADS * 2 * DIFF_QK), ("dv", DIFF_HEADS * DIFF_V), ("sq", SWA_HEADS * SWA_HD),
                   ("sv", SWA_KV_HEADS * SWA_HD)))
MLA_QK_W = MLA_HEADS * LANES
MLA_V_W = MLA_HEADS * MLA_V
VMEM_LIMIT = 56 * 1024 * 1024

MLA_QTILE = 2048
MLA_KTILE = 512
DIFF_QTILE = 1024
DIFF_TILE = 512
SWA_QTILE = 1024
PROJ_TILE = 1024
ROW_TILE = 512

f32 = jnp.float32
bf16 = jnp.bfloat16


def _bucket_thresholds():
    n = np.arange(0, 4 * REL_MAX_DIST)
    max_exact = REL_BUCKETS // 2
    nf = np.maximum(n, max_exact).astype(np.float32)
    large = max_exact + (np.log(nf / np.float32(max_exact)) / np.float32(math.log(REL_MAX_DIST / max_exact))
                         * np.float32(REL_BUCKETS - max_exact)).astype(np.int32)
    bucket = np.where(n < max_exact, n, np.minimum(large, REL_BUCKETS - 1))
    assert np.all(np.diff(bucket) >= 0) and bucket[-1] == REL_BUCKETS - 1
    return [int(np.argmax(bucket >= k)) for k in range(1, REL_BUCKETS)]


BUCKET_THR = _bucket_thresholds()
FAR_DIST = BUCKET_THR[-1]
assert FAR_DIST <= BLK - N_META + 1


def _dot(a, b):
    return jnp.dot(a, b, preferred_element_type=f32)


def _dot_nt(a, b):
    return lax.dot_general(a, b, (((1,), (1,)), ((), ())), preferred_element_type=f32)


def _rms(x, g):
    return (x * lax.rsqrt(jnp.mean(x * x, axis=-1, keepdims=True) + EPS)) * g


def _lane(shape):
    return lax.broadcasted_iota(jnp.int32, shape, len(shape) - 1)


def _row(shape):
    return lax.broadcasted_iota(jnp.int32, shape, 0)


def _bias_lookup(n, tab_ref, h):
    acc = jnp.full(n.shape, tab_ref[0, h], f32)
    for k, thr in enumerate(BUCKET_THR, start=1):
        acc = jnp.where(n >= thr, tab_ref[k, h], acc)
    return acc


def _bias_kernel(tab_ref, dd_diag, dd_prev, dd_meta, dd_self, ds_band, ds_meta, ds_self, *, td):
    rc = 32

    def rows_cols(r0, cols):
        a = r0 + lax.broadcasted_iota(jnp.int32, (rc, cols), 0)
        b = lax.broadcasted_iota(jnp.int32, (rc, cols), 1)
        return a, b

    def diff_body(i, carry, near_prev):
        r0 = pl.multiple_of(i * rc, rc)
        k, q = rows_cols(r0, td)
        for h in range(DIFF_HEADS):
            c = tab_ref[REL_BUCKETS - 1, h]
            dd_diag[h, pl.ds(r0, rc), :] = jnp.where(
                k <= q, (_bias_lookup(jnp.maximum(q - k, 0), tab_ref, h) - c) * LOG2E, NEG_INF)
            if near_prev:
                dd_prev[h, pl.ds(r0, rc), :] = (_bias_lookup(q + td - k, tab_ref, h) - c) * LOG2E
            else:
                dd_prev[h, pl.ds(r0, rc), :] = jnp.zeros((rc, td), f32)
        return carry

    n_far = (td - BLK) // rc
    lax.fori_loop(0, n_far, functools.partial(diff_body, near_prev=False), 0)
    lax.fori_loop(n_far, td // rc, functools.partial(diff_body, near_prev=True), 0)

    def blk_body(i, carry):
        r0 = pl.multiple_of(i * rc, rc)
        j, q = rows_cols(r0, td)
        j2, q2 = rows_cols(r0, BLK)
        self_ok = j2 <= jnp.minimum(q2, N_META - 1)
        for h in range(DIFF_HEADS):
            c = tab_ref[REL_BUCKETS - 1, h]
            dd_meta[0, h, pl.ds(r0, rc), :] = jnp.where(
                j < N_META, (_bias_lookup(N_META + q - j, tab_ref, h) - c) * LOG2E, NEG_INF)
            dd_meta[1, h, pl.ds(r0, rc), :] = jnp.where(j < N_META, 0.0, NEG_INF)
            dd_self[h, pl.ds(r0, rc), :] = jnp.where(
                self_ok, (_bias_lookup(jnp.maximum(q2 - j2, 0), tab_ref, h) - c) * LOG2E, NEG_INF)
        k, a = j2, q2
        for h in range(SWA_HEADS):
            hb = DIFF_HEADS + h
            near = _bias_lookup(jnp.maximum(a - k, 0), tab_ref, hb) * LOG2E
            cur = jnp.where(k <= a, near, NEG_INF)
            prev = jnp.where(k > a, _bias_lookup(a + BLK - k, tab_ref, hb) * LOG2E, NEG_INF)
            ds_band[0, h, pl.ds(r0, rc), :] = cur
            ds_band[0, h, pl.ds(BLK + r0, rc), :] = jnp.full((rc, BLK), NEG_INF, f32)
            ds_band[1, h, pl.ds(r0, rc), :] = prev
            ds_band[1, h, pl.ds(BLK + r0, rc), :] = cur
            ds_meta[0, h, pl.ds(r0, rc), :] = jnp.where(
                k < N_META, _bias_lookup(N_META + a - k, tab_ref, hb) * LOG2E, NEG_INF)
            far = jnp.full((rc, BLK), tab_ref[REL_BUCKETS - 1, hb], f32) * LOG2E
            ds_meta[1, h, pl.ds(r0, rc), :] = jnp.where(k < N_META, far, NEG_INF)
            ds_self[0, h, pl.ds(r0, rc), :] = jnp.where(self_ok, near, NEG_INF)
        return carry

    lax.fori_loop(0, BLK // rc, blk_body, 0)


def _bias_tiles(rel_bias):
    td = DIFF_TILE
    outs = (
        jax.ShapeDtypeStruct((DIFF_HEADS, td, td), f32),
        jax.ShapeDtypeStruct((DIFF_HEADS, td, td), f32),
        jax.ShapeDtypeStruct((2, DIFF_HEADS, BLK, td), f32),
        jax.ShapeDtypeStruct((DIFF_HEADS, BLK, BLK), f32),
        jax.ShapeDtypeStruct((2, SWA_HEADS, 2 * BLK, BLK), f32),
        jax.ShapeDtypeStruct((2, SWA_HEADS, BLK, LANES), f32),
        jax.ShapeDtypeStruct((1, SWA_HEADS, BLK, LANES), f32),
    )
    return pl.pallas_call(
        functools.partial(_bias_kernel, td=td),
        name="bias_tiles",
        out_shape=outs,
        in_specs=[pl.BlockSpec(memory_space=pltpu.SMEM)],
        compiler_params=pltpu.CompilerParams(vmem_limit_bytes=VMEM_LIMIT),
    )(rel_bias)


def _store_vt(out_ref, vt, heads):
    tm = vt.shape[1]
    ones_blk = jnp.where(_row((HEAD_V, tm)) == 0, 1.0, 0.0).astype(bf16)
    for hd in range(heads):
        out_ref[0, hd * LANES:hd * LANES + HEAD_V, :] = vt[hd * HEAD_V:(hd + 1) * HEAD_V].astype(bf16)
        out_ref[0, hd * LANES + HEAD_V:(hd + 1) * LANES, :] = ones_blk


def _proj_kernel(h_ref, g_ref, wnat_ref, wtr_ref, qn_ref, wqbt_ref, kvn_ref, wkb_ref, wvt_ref,
                 csqt_ref, cka_ref, ckb_ref,
                 mqt_ref, mk_ref, mvt_ref, dqt_ref, dk_ref, dvt_ref, sqt_ref, sk_ref, svt_ref):
    hn = _rms(h_ref[0], g_ref[...]).astype(bf16)
    proj = _dot(hn, wnat_ref[...])
    tr = _dot_nt(wtr_ref[...], hn)
    c_q = proj[:, NAT["c_q"]]
    c_kv = proj[:, NAT["c_kv"]]
    rope_a = proj[:, NAT["rope_a"]]
    rope_b = proj[:, NAT["rope_b"]]
    dk_ref[0] = proj[:, NAT["dk"]].astype(bf16)
    sk_ref[0] = proj[:, NAT["sk"]].astype(bf16)
    dqt_ref[0] = (tr[TR["dq"]] * (DIFF_SCALE * LOG2E)).astype(bf16)
    _store_vt(dvt_ref, tr[TR["dv"]], DIFF_HEADS)
    sqt_ref[0] = (tr[TR["sq"]] * (SWA_SCALE * LOG2E)).astype(bf16)
    svt_ref[0] = tr[TR["sv"]].astype(bf16)

    cqn = _rms(c_q, qn_ref[...]).astype(bf16)
    ckvn = _rms(c_kv, kvn_ref[...]).astype(bf16)
    qt = _dot_nt(wqbt_ref[...], cqn)
    kk = _dot(ckvn, wkb_ref[...])
    vt = _dot_nt(wvt_ref[...], ckvn)
    csqt = csqt_ref[...]
    k_rot = rope_a * cka_ref[...] + rope_b * ckb_ref[...]
    for hd in range(MLA_HEADS):
        sl = slice(hd * LANES, (hd + 1) * LANES)
        mqt_ref[0, sl, :] = (qt[sl] * csqt).astype(bf16)
        mk_ref[0, :, sl] = (kk[:, sl] + k_rot).astype(bf16)
    _store_vt(mvt_ref, vt, MLA_HEADS)


def _const_spec(shape):
    nd = len(shape)
    return pl.BlockSpec(shape, lambda *_: (0,) * nd)


def _layer_spec(shape, l, single=False):
    nd = len(shape)
    return pl.BlockSpec((None,) + tuple(shape), lambda *_: (l,) + (0,) * nd,
                        pipeline_mode=pl.Buffered(1) if single else None)


def _project(h, l, lw, tabs, tm):
    bsz, t, _ = h.shape
    width = {name: sp.stop - sp.start for name, sp in {**NAT, **TR}.items()}
    outs = ((MLA_QK_W, True), (MLA_QK_W, False), (MLA_HEADS * LANES, True), (width["dq"], True),
            (width["dk"], False), (DIFF_HEADS * LANES, True), (width["sq"], True), (width["sk"], False),
            (width["sv"], True))
    out_shape = tuple(jax.ShapeDtypeStruct((bsz, w, t) if tr else (bsz, t, w), bf16) for w, tr in outs)
    out_specs = tuple(pl.BlockSpec((1, w, tm), lambda b, i: (b, 0, i)) if tr
                      else pl.BlockSpec((1, tm, w), lambda b, i: (b, i, 0)) for w, tr in outs)
    tab_spec = pl.BlockSpec((tm, LANES), lambda b, i: (i, 0))
    in_specs = [
        pl.BlockSpec((1, tm, D_MODEL), lambda b, i: (b, i, 0)),
        _layer_spec((1, D_MODEL), l),
        _layer_spec((D_MODEL, D_NAT), l),
        _layer_spec((D_TR, D_MODEL), l),
        _layer_spec((1, MLA_Q_RANK), l),
        _layer_spec((MLA_QK_W, MLA_Q_RANK), l),
        _layer_spec((1, MLA_KV_RANK), l),
        _layer_spec((MLA_KV_RANK, MLA_QK_W), l),
        _layer_spec((MLA_V_W, MLA_KV_RANK), l),
        pl.BlockSpec((LANES, tm), lambda b, i: (0, i)),
        tab_spec, tab_spec,
    ]
    return pl.pallas_call(
        _proj_kernel,
        name="proj_real" if bsz > 1 else "proj_meta",
        out_shape=out_shape,
        grid=(bsz, t // tm),
        in_specs=in_specs,
        out_specs=out_specs,
        compiler_params=pltpu.CompilerParams(
            dimension_semantics=("parallel", "parallel"), vmem_limit_bytes=VMEM_LIMIT),
    )(h, lw["attn_norm"], lw["w_nat"], lw["w_tr"], lw["q_norm"], lw["w_qbt"], lw["kv_norm"], lw["w_kb"],
      lw["w_vt"], tabs["csqt"], tabs["cka"], tabs["ckb"])


def _flash_update(s, vt, m_ref, acc_ref, idx, cols=slice(None), n_rows=ACC_ROWS):
    m_prev = m_ref[idx, :, cols]
    m_new = jnp.maximum(m_prev, jnp.max(s, axis=0, keepdims=True))
    alpha = jnp.exp2(m_prev - m_new)
    p = jnp.exp2(s - m_new).astype(bf16)
    rows = slice(0, n_rows)
    acc_ref[idx, rows, cols] = acc_ref[idx, rows, cols] * alpha + _dot(vt[rows], p)
    m_ref[idx, :, cols] = m_new


def _init_stats(m_ref, acc_ref):
    m_ref[...] = jnp.full(m_ref.shape, NEG_INF, f32)
    acc_ref[...] = jnp.zeros(acc_ref.shape, f32)


def _normalized(acc_ref, idx, cols):
    return acc_ref[idx, 0:HEAD_V, cols] / acc_ref[idx, HEAD_V:HEAD_V + 1, cols]


def _mla_kernel(*refs, tq, tk, has_meta):
    if has_meta:
        qt_ref, k_ref, vt_ref, mk_ref, mvt_ref, o_ref, m_ref, acc_ref, s_ref = refs
    else:
        qt_ref, k_ref, vt_ref, o_ref, m_ref, acc_ref, s_ref = refs
    qi = pl.program_id(2)
    ratio = tq // tk
    _init_stats(m_ref, acc_ref)

    def head(hh):
        return slice(hh * LANES, (hh + 1) * LANES)

    if has_meta:
        for hh in range(2):
            s = _dot(mk_ref[0, 0:N_META, head(hh)], qt_ref[0, head(hh), :])
            _flash_update(s, mvt_ref[0, head(hh), 0:N_META], m_ref, acc_ref, hh, n_rows=LANES)

    def keys(j):
        return pl.ds(pl.multiple_of(j * tk, tk), tk)

    def stage_a(j, hh, cols):
        s_ref[hh, :, cols] = _dot(k_ref[0, keys(j), head(hh)], qt_ref[0, head(hh), cols])

    def stage_b(j, hh, cols, mask):
        s = s_ref[hh, :, cols]
        if mask is not None:
            s = jnp.where(mask, s, NEG_INF)
        _flash_update(s, vt_ref[0, head(hh), keys(j)], m_ref, acc_ref, hh, cols, n_rows=LANES)

    def blocks(cols):
        return [slice(c, c + tk) for c in range(cols.start, cols.stop, tk)]

    def tile_step(j, cols, mask, next_cols):
        chains = [(hh, cb) for cb in blocks(cols) for hh in range(2)]
        for idx, (hh, cb) in enumerate(chains):
            if idx + 1 < len(chains):
                stage_a(j, *chains[idx + 1])
            elif next_cols is not None:
                stage_a(j + 1, 0, blocks(next_cols)[0])
            chain_mask = None if mask is None else mask[:, cb.start - cols.start:cb.stop - cols.start]
            stage_b(j, hh, cb, chain_mask)

    full = slice(0, tq)
    stage_a(0, 0, blocks(full)[0])
    n_plain = qi * ratio

    def body(i, carry):
        tile_step(2 * i, full, None, full)
        tile_step(2 * i + 1, full, None, full)
        return carry

    lax.fori_loop(0, n_plain // 2, body, 0)
    if ratio % 2 == 1:
        @pl.when(n_plain % 2 == 1)
        def _():
            tile_step(n_plain - 1, full, None, full)

    for d in range(ratio):
        width = tq - d * tk
        key = _row((tk, width))
        qry = _lane((tk, width))
        if has_meta:
            mask = key <= qry
        else:
            mask = key <= jnp.minimum(qry, N_META - 1)
        next_cols = slice((d + 1) * tk, tq) if d + 1 < ratio else None
        tile_step(n_plain + d, slice(d * tk, tq), mask, next_cols)

    ot = jnp.concatenate([_normalized(acc_ref, 0, full), _normalized(acc_ref, 1, full)], axis=0)
    o_ref[0] = ot.T.astype(bf16)


def _mla_attention(mqt, mk, mvt, meta_kv, tq, tk):
    bsz, tlen, _ = mk.shape
    has_meta = meta_kv is not None
    npair = MLA_HEADS // 2
    in_specs = [
        pl.BlockSpec((1, 2 * LANES, tq), lambda b, p, i: (b, p, i)),
        pl.BlockSpec((1, tlen, 2 * LANES), lambda b, p, i: (b, 0, p)),
        pl.BlockSpec((1, 2 * LANES, tlen), lambda b, p, i: (b, p, 0)),
    ]
    args = [mqt, mk, mvt]
    if has_meta:
        in_specs += [
            pl.BlockSpec((1, BLK, 2 * LANES), lambda b, p, i: (0, 0, p)),
            pl.BlockSpec((1, 2 * LANES, BLK), lambda b, p, i: (0, p, 0)),
        ]
        args += list(meta_kv)
    return pl.pallas_call(
        functools.partial(_mla_kernel, tq=tq, tk=tk, has_meta=has_meta),
        name="mla_real" if has_meta else "mla_meta",
        out_shape=jax.ShapeDtypeStruct((bsz, tlen, npair * LANES), bf16),
        grid=(bsz, npair, tlen // tq),
        in_specs=in_specs,
        out_specs=pl.BlockSpec((1, tq, LANES), lambda b, p, i: (b, i, p)),
        scratch_shapes=[
            pltpu.VMEM((2, 1, tq), f32),
            pltpu.VMEM((2, LANES, tq), f32),
            pltpu.VMEM((2, tk, tq), f32),
        ],
        compiler_params=pltpu.CompilerParams(
            dimension_semantics=("parallel", "parallel", "arbitrary"), vmem_limit_bytes=VMEM_LIMIT),
    )(*args)


def _diff_kernel(*refs, tq, tk, has_meta, lam_init):
    if has_meta:
        (qt_ref, k_ref, vt_ref, mk_ref, mvt_ref, ddiag_ref, dprev_ref, dmeta_ref, lam_ref, subg_ref,
         o_ref, qs_ref, m_ref, acc_ref, s_ref) = refs
    else:
        (qt_ref, k_ref, vt_ref, ddiag_ref, lam_ref, subg_ref, o_ref, qs_ref, m_ref, acc_ref, s_ref) = refs
    qi = pl.program_id(2)
    ratio = tq // tk
    seg = 2 * tk
    _init_stats(m_ref, acc_ref)

    def head(hh):
        return slice(hh * LANES, (hh + 1) * LANES)

    def half_cols(h):
        return slice(h * seg, (h + 1) * seg)

    def map_cols(h, c):
        return slice(h * seg + c * tk, h * seg + (c + 1) * tk)

    group = _row((LANES, tk)) // DIFF_QK
    for h in range(ratio):
        qt = qt_ref[0, :, h * tk:(h + 1) * tk]
        for hh in range(2):
            for c in range(2):
                qs_ref[hh, :, map_cols(h, c)] = jnp.where(group == 2 * hh + c, qt, jnp.zeros_like(qt))

    def keys(j):
        return pl.ds(pl.multiple_of(j * tk, tk), tk)

    def stage_a(j, hh, h, c):
        s_ref[hh, :, map_cols(h, c)] = _dot(k_ref[0, keys(j), :], qs_ref[hh, :, map_cols(h, c)])

    def stage_b(j, hh, h, c, bias_ref):
        s = s_ref[hh, :, map_cols(h, c)]
        if bias_ref is not None:
            s = s + bias_ref[hh]
        _flash_update(s, vt_ref[0, head(hh), keys(j)], m_ref, acc_ref, hh, map_cols(h, c))

    def tile_step(j, first_half, biases, next_first_half):
        chains = [(hh, h, c) for h in range(first_half, ratio) for hh in range(2) for c in range(2)]
        for idx, (hh, h, c) in enumerate(chains):
            if idx + 1 < len(chains):
                stage_a(j, *chains[idx + 1])
            elif next_first_half is not None:
                stage_a(j + 1, 0, next_first_half, 0)
            stage_b(j, hh, h, c, biases.get(h))

    if has_meta:
        for h in range(ratio):
            for hh in range(2):
                b = dmeta_ref[jnp.minimum(qi, 1), hh, 0:N_META] if h == 0 else dmeta_ref[1, hh, 0:N_META]
                s = _dot(mk_ref[0, 0:N_META], qs_ref[hh, :, half_cols(h)]) + jnp.concatenate([b, b], axis=1)
                _flash_update(s, mvt_ref[0, head(hh), 0:N_META], m_ref, acc_ref, hh, half_cols(h))

    stage_a(0, 0, 0, 0)
    first_diag = qi * ratio
    if has_meta:
        def body(i, carry):
            tile_step(2 * i, 0, {}, 0)
            tile_step(2 * i + 1, 0, {}, 0)
            return carry

        lax.fori_loop(0, jnp.maximum(first_diag - 2, 0) // 2, body, 0)

        @pl.when(qi >= 1)
        def _():
            tile_step(first_diag - 2, 0, {}, 0)
            tile_step(first_diag - 1, 0, {0: dprev_ref}, 0)

    for d in range(ratio):
        biases = {d: ddiag_ref}
        if d + 1 < ratio:
            biases[d + 1] = dprev_ref
        tile_step(first_diag + d, d, biases, d + 1 if d + 1 < ratio else None)

    lp = lam_ref[...]
    lam = (jnp.exp(jnp.sum(lp[0:1] * lp[1:2], axis=-1, keepdims=True))
           - jnp.exp(jnp.sum(lp[2:3] * lp[3:4], axis=-1, keepdims=True)) + lam_init)
    outs = []
    for hh in range(2):
        halves = [_normalized(acc_ref, hh, map_cols(h, 0)) - lam * _normalized(acc_ref, hh, map_cols(h, 1))
                  for h in range(ratio)]
        o = halves[0] if ratio == 1 else jnp.concatenate(halves, axis=1)
        ms = jnp.mean(o * o, axis=0, keepdims=True)
        outs.append(o * lax.rsqrt(ms + EPS))
    o_nat = jnp.concatenate(outs, axis=0).T
    o_ref[0] = ((o_nat * subg_ref[...]) * (1.0 - lam_init)).astype(bf16)


def _diff_attention(dqt, dk, dvt, meta_kv, tiles, l, lam_p, sub_g, lam_init, tq, tk):
    bsz, tlen, _ = dk.shape
    has_meta = meta_kv is not None
    assert tq % tk == 0 and (not has_meta or (tq // tk) % 2 == 0)
    npair = DIFF_HEADS // 2
    in_specs = [
        pl.BlockSpec((1, LANES, tq), lambda b, p, i: (b, p, i)),
        pl.BlockSpec((1, tlen, LANES), lambda b, p, i: (b, 0, p)),
        pl.BlockSpec((1, 2 * LANES, tlen), lambda b, p, i: (b, p, 0)),
    ]
    args = [dqt, dk, dvt]
    if has_meta:
        in_specs += [
            pl.BlockSpec((1, BLK, LANES), lambda b, p, i: (0, 0, p)),
            pl.BlockSpec((1, 2 * LANES, BLK), lambda b, p, i: (0, p, 0)),
            pl.BlockSpec((2, tk, tk), lambda b, p, i: (p, 0, 0)),
            pl.BlockSpec((2, tk, tk), lambda b, p, i: (p, 0, 0)),
            pl.BlockSpec((2, 2, BLK, tk), lambda b, p, i: (0, p, 0, 0)),
        ]
        args += list(meta_kv) + [tiles["dd_diag"], tiles["dd_prev"], tiles["dd_meta"]]
    else:
        in_specs += [pl.BlockSpec((2, tk, tk), lambda b, p, i: (p, 0, 0))]
        args += [tiles["dd_self"]]
    in_specs += [_layer_spec((4, DIFF_QK), l), _layer_spec((1, LANES), l)]
    args += [lam_p, sub_g]
    return pl.pallas_call(
        functools.partial(_diff_kernel, tq=tq, tk=tk, has_meta=has_meta, lam_init=lam_init),
        name="diff_real" if has_meta else "diff_meta",
        out_shape=jax.ShapeDtypeStruct((bsz, tlen, npair * LANES), bf16),
        grid=(bsz, npair, tlen // tq),
        in_specs=in_specs,
        out_specs=pl.BlockSpec((1, tq, LANES), lambda b, p, i: (b, i, p)),
        scratch_shapes=[
            pltpu.VMEM((2, LANES, 2 * tq), bf16),
            pltpu.VMEM((2, 1, 2 * tq), f32),
            pltpu.VMEM((2, LANES, 2 * tq), f32),
            pltpu.VMEM((2, tk, 2 * tq), f32),
        ],
        compiler_params=pltpu.CompilerParams(
            dimension_semantics=("parallel", "parallel", "arbitrary"), vmem_limit_bytes=VMEM_LIMIT),
    )(*args)


def _swa_kernel(*refs, tq, has_band, layer):
    if has_band:
        qt_ref, k_ref, vt_ref, mk_ref, mvt_ref, dband_ref, dmeta_ref, sink_ref, o_ref, sm_ref, sb_ref = refs
    else:
        qt_ref, mk_ref, mvt_ref, dmeta_ref, sink_ref, o_ref, sm_ref = refs
    qi = pl.program_id(1)
    feat = _row((LANES, BLK))
    first_head = _lane((1, 2 * BLK)) < BLK
    n_meta_var = dmeta_ref.shape[0]

    def block(sub):
        blk = qi * (tq // BLK) + sub
        c0 = sub * BLK
        mvar = jnp.minimum(blk, n_meta_var - 1)
        if has_band:
            bvar = jnp.minimum(blk, 1)
            koff = pl.multiple_of(jnp.maximum(blk - 1, 0) * BLK, BLK)
            kband = k_ref[0, pl.ds(koff, 2 * BLK), :]
            vtband = vt_ref[0, :, pl.ds(koff, 2 * BLK)]

        def stage_a(r):
            q2t = qt_ref[0, r * LANES:(r + 1) * LANES, c0:c0 + BLK]
            zero = jnp.zeros_like(q2t)
            qs = jnp.concatenate([jnp.where(feat < SWA_HD, q2t, zero), jnp.where(feat >= SWA_HD, q2t, zero)],
                                 axis=1)
            sm_ref[r] = _dot(mk_ref[0], qs)
            if has_band:
                sb_ref[r] = _dot(kband, qs)

        def stage_b(r):
            h0, h1 = r, SWA_REP + r
            sink = jnp.where(first_head, sink_ref[layer, h0], sink_ref[layer, h1]) * LOG2E
            s_meta = sm_ref[r] + jnp.concatenate([dmeta_ref[mvar, h0], dmeta_ref[mvar, h1]], axis=1)
            m = jnp.maximum(jnp.max(s_meta, axis=0, keepdims=True), sink)
            if has_band:
                s_band = sb_ref[r] + jnp.concatenate([dband_ref[bvar, h0], dband_ref[bvar, h1]], axis=1)
                m = jnp.maximum(m, jnp.max(s_band, axis=0, keepdims=True))
            p_meta = jnp.exp2(s_meta - m)
            denom = jnp.sum(p_meta, axis=0, keepdims=True) + jnp.exp2(sink - m)
            acc = _dot(mvt_ref[0], p_meta.astype(bf16))
            if has_band:
                p_band = jnp.exp2(s_band - m)
                denom = denom + jnp.sum(p_band, axis=0, keepdims=True)
                acc = acc + _dot(vtband, p_band.astype(bf16))
            ot = acc / denom
            pair = jnp.where(feat < SWA_HD, ot[:, 0:BLK], ot[:, BLK:2 * BLK])
            o_ref[0, c0:c0 + BLK, r * LANES:(r + 1) * LANES] = pair.T.astype(bf16)

        return stage_a, stage_b

    stages = [block(sub) for sub in range(tq // BLK)]
    chains = [(sub, r) for sub in range(tq // BLK) for r in range(SWA_REP)]
    stages[0][0](0)
    stages[0][0](1)
    for i, (sub, r) in enumerate(chains):
        if i + 2 < len(chains):
            nsub, nr = chains[i + 2]
            stages[nsub][0](nr)
        stages[sub][1](r)


def _swa_attention(sqt, sk, svt, meta_kv, tiles, l, sinks, tq):
    bsz, width, tlen = sqt.shape
    has_band = meta_kv is not None
    smem = pl.BlockSpec(memory_space=pltpu.SMEM)
    if has_band:
        in_specs = [
            pl.BlockSpec((1, width, tq), lambda b, i: (b, 0, i)),
            pl.BlockSpec((1, tlen, LANES), lambda b, i: (b, 0, 0)),
            pl.BlockSpec((1, LANES, tlen), lambda b, i: (b, 0, 0)),
            _const_spec((1, BLK, LANES)),
            _const_spec((1, LANES, BLK)),
            _const_spec(tiles["ds_band"].shape),
            _const_spec(tiles["ds_meta"].shape),
            smem,
        ]
        args = [sqt, sk, svt, meta_kv[0], meta_kv[1], tiles["ds_band"], tiles["ds_meta"], sinks]
    else:
        in_specs = [
            pl.BlockSpec((1, width, tq), lambda b, i: (b, 0, i)),
            _const_spec((1, BLK, LANES)),
            _const_spec((1, LANES, BLK)),
            _const_spec(tiles["ds_self"].shape),
            smem,
        ]
        args = [sqt, sk, svt, tiles["ds_self"], sinks]
    return pl.pallas_call(
        functools.partial(_swa_kernel, tq=tq, has_band=has_band, layer=l),
        name="swa_real" if has_band else "swa_meta",
        out_shape=jax.ShapeDtypeStruct((bsz, tlen, width), bf16),
        grid=(bsz, tlen // tq),
        in_specs=in_specs,
        out_specs=pl.BlockSpec((1, tq, width), lambda b, i: (b, i, 0)),
        scratch_shapes=[pltpu.VMEM((SWA_REP, BLK, 2 * BLK), f32)]
        + ([pltpu.VMEM((SWA_REP, 2 * BLK, 2 * BLK), f32)] if has_band else []),
        compiler_params=pltpu.CompilerParams(
            dimension_semantics=("parallel", "parallel"), vmem_limit_bytes=VMEM_LIMIT),
    )(*args)


def _out_ffn_kernel(h_ref, ya_ref, yb_ref, yc_ref, wo_ref, g_ref, wg_ref, wu_ref, wd_ref, fg_ref, o_ref, *, final):
    y = jnp.concatenate([ya_ref[0], yb_ref[0], yc_ref[0]], axis=1)
    h1 = h_ref[0] + _dot(y, wo_ref[...])
    hn = _rms(h1, g_ref[...]).astype(bf16)
    gate = _dot(hn, wg_ref[...])
    up = _dot(hn, wu_ref[...])
    act = ((gate * jax.nn.sigmoid(gate)) * up).astype(bf16)
    h2 = h1 + _dot(act, wd_ref[...])
    if final:
        h2 = _rms(h2, fg_ref[...])
    o_ref[0] = h2


def _out_ffn(h, ya, yb, yc, l, lw, final_g, final, tm):
    bsz, t, _ = h.shape

    def row(w):
        return pl.BlockSpec((1, tm, w), lambda b, i: (b, i, 0))

    in_specs = [
        row(D_MODEL), row(MLA_HEADS * MLA_V), row(DIFF_HEADS * DIFF_V), row(SWA_HEADS * SWA_HD),
        _layer_spec((D_MODEL, D_MODEL), l, single=True),
        _layer_spec((1, D_MODEL), l),
        _layer_spec((D_MODEL, D_FF), l, single=True),
        _layer_spec((D_MODEL, D_FF), l, single=True),
        _layer_spec((D_FF, D_MODEL), l, single=True),
        _const_spec((1, D_MODEL)),
    ]
    return pl.pallas_call(
        functools.partial(_out_ffn_kernel, final=final),
        name="out_ffn_real" if bsz > 1 else "out_ffn_meta",
        out_shape=jax.ShapeDtypeStruct((bsz, t, D_MODEL), f32),
        grid=(bsz, t // tm),
        in_specs=in_specs,
        out_specs=row(D_MODEL),
        compiler_params=pltpu.CompilerParams(
            dimension_semantics=("parallel", "parallel"), vmem_limit_bytes=VMEM_LIMIT),
    )(h, ya, yb, yc, lw["w_o"], lw["ffn_norm"], lw["w_gate"], lw["w_up"], lw["w_down"], final_g)


def _rot_half(w):
    half = MLA_ROPE // 2
    return jnp.concatenate([-w[..., half:], w[..., :half]], axis=-1)


def _stacked_weights(attn_norm, w_in, mla_q_norm, mla_w_qb, mla_kv_norm, mla_w_kvb, diff_lambda, diff_subln, w_out,
                     ffn_norm, w_gate, w_up, w_down):
    depth = w_in.shape[0]
    parts, o = {}, 0
    for name, width in (("c_q", MLA_Q_RANK), ("c_kv", MLA_KV_RANK), ("k_rope", MLA_ROPE),
                        ("dq", DIFF_HEADS * 2 * DIFF_QK), ("dk", DIFF_HEADS * 2 * DIFF_QK),
                        ("dv", DIFF_HEADS * DIFF_V), ("sq", SWA_HEADS * SWA_HD), ("sk", SWA_KV_HEADS * SWA_HD),
                        ("sv", SWA_KV_HEADS * SWA_HD)):
        parts[name] = w_in[:, :, o:o + width]
        o += width
    kr = parts["k_rope"]
    ksw = _rot_half(kr)
    z64 = jnp.zeros((depth, D_MODEL, MLA_NOPE), f32)
    sq = parts["sq"].reshape(depth, D_MODEL, SWA_KV_HEADS, SWA_REP, SWA_HD)
    sq = sq.transpose(0, 1, 3, 2, 4).reshape(depth, D_MODEL, SWA_HEADS * SWA_HD)
    w_nat = jnp.concatenate([parts["c_q"], parts["c_kv"], z64, kr, kr, z64, ksw, ksw, parts["dk"], parts["sk"]],
                            axis=-1)
    w_tr = jnp.swapaxes(jnp.concatenate([parts["dq"], parts["dv"], sq, parts["sv"]], axis=-1), 1, 2)
    assert w_nat.shape == (depth, D_MODEL, D_NAT) and w_tr.shape == (depth, D_TR, D_MODEL)

    wq = mla_w_qb.reshape(depth, MLA_Q_RANK, MLA_HEADS, MLA_NOPE + MLA_ROPE)
    rope = wq[..., MLA_NOPE:]
    w_qbt = jnp.concatenate([wq[..., :MLA_NOPE], rope, _rot_half(rope)], axis=-1)
    w_qbt = jnp.swapaxes(w_qbt.reshape(depth, MLA_Q_RANK, MLA_HEADS * LANES), 1, 2)

    wkv = mla_w_kvb.reshape(depth, MLA_KV_RANK, MLA_HEADS, MLA_NOPE + MLA_V)
    w_kb = jnp.concatenate([wkv[..., :MLA_NOPE], jnp.zeros(wkv.shape[:3] + (LANES - MLA_NOPE,), f32)], axis=-1)
    w_kb = w_kb.reshape(depth, MLA_KV_RANK, MLA_HEADS * LANES)
    w_vt = jnp.swapaxes(wkv[..., MLA_NOPE:].reshape(depth, MLA_KV_RANK, MLA_HEADS * MLA_V), 1, 2)

    n_ab = MLA_HEADS * MLA_V + DIFF_HEADS * DIFF_V
    wo_c = w_out[:, n_ab:].reshape(depth, SWA_KV_HEADS, SWA_REP, SWA_HD, D_MODEL)
    wo_c = wo_c.transpose(0, 2, 1, 3, 4).reshape(depth, SWA_HEADS * SWA_HD, D_MODEL)
    w_o = jnp.concatenate([w_out[:, :n_ab], wo_c], axis=1)
    return {
        "attn_norm": attn_norm[:, None], "w_nat": w_nat.astype(bf16), "w_tr": w_tr.astype(bf16),
        "q_norm": mla_q_norm[:, None], "w_qbt": w_qbt.astype(bf16),
        "kv_norm": mla_kv_norm[:, None], "w_kb": w_kb.astype(bf16), "w_vt": w_vt.astype(bf16),
        "diff_lambda": diff_lambda, "sub_g": jnp.concatenate([diff_subln, diff_subln], axis=-1)[:, None],
        "w_o": w_o.astype(bf16), "ffn_norm": ffn_norm[:, None],
        "w_gate": w_gate.astype(bf16), "w_up": w_up.astype(bf16), "w_down": w_down.astype(bf16),
    }


def _rope_lane_selectors():
    half = MLA_ROPE // 2
    idx = np.arange(half)
    sel = np.zeros((3, 2 * half, LANES), np.float32)
    for rep in range(2):
        sel[0, idx, MLA_NOPE + rep * half + idx] = 1.0
        sel[0, half + idx, MLA_NOPE + MLA_ROPE + rep * half + idx] = 1.0
        for blk in range(2):
            lane0 = MLA_NOPE + (2 * blk + rep) * half
            sel[1, idx, lane0 + idx] = 1.0
            sel[2, half + idx, lane0 + idx] = 1.0
    return sel


def _rope_tables(pos):
    inv_freq = ROPE_THETA ** (-jnp.arange(0, MLA_ROPE, 2, dtype=f32) / MLA_ROPE)
    ang = pos.astype(f32)[:, None] * inv_freq[None, :]
    cos_sin = jnp.concatenate([jnp.cos(ang), jnp.sin(ang)], axis=1)
    placed = jnp.einsum("nk,tkl->tnl", cos_sin, jnp.asarray(_rope_lane_selectors()),
                        precision=lax.Precision.HIGHEST)
    nope_ones = (np.arange(LANES) < MLA_NOPE).astype(np.float32)[None, :]
    return {
        "csqt": ((placed[0] + nope_ones) * (MLA_SCALE * LOG2E)).T,
        "cka": placed[1],
        "ckb": placed[2],
    }


def _mixers(proj, meta_proj, tiles, l, lw, lam_init, sinks, real):
    mqt, mk, mvt, dqt, dk, dvt, sq, sk, sv = proj
    lam_p, sub_g = lw["diff_lambda"], lw["sub_g"]
    if real:
        m_mk, m_mvt, m_dk, m_dvt, m_sk, m_sv = (meta_proj[i] for i in (1, 2, 4, 5, 7, 8))
        ya = _mla_attention(mqt, mk, mvt, (m_mk, m_mvt), MLA_QTILE, MLA_KTILE)
        yb = _diff_attention(dqt, dk, dvt, (m_dk, m_dvt), tiles, l, lam_p, sub_g, lam_init, DIFF_QTILE, DIFF_TILE)
        yc = _swa_attention(sq, sk, sv, (m_sk, m_sv), tiles, l, sinks, SWA_QTILE)
    else:
        ya = _mla_attention(mqt, mk, mvt, None, BLK, BLK)
        yb = _diff_attention(dqt, dk, dvt, None, tiles, l, lam_p, sub_g, lam_init, BLK, BLK)
        yc = _swa_attention(sq, sk, sv, None, tiles, l, sinks, BLK)
    return ya, yb, yc


def kernel(x, meta_tokens, rel_bias, attn_norm, w_in, mla_q_norm, mla_w_qb, mla_kv_norm, mla_w_kvb, diff_lambda,
           diff_subln, swa_sinks, w_out, ffn_norm, w_gate, w_up, w_down, final_norm):
    bsz, seq, _ = x.shape
    assert seq % MLA_QTILE == 0 and MLA_QTILE % MLA_KTILE == 0 and seq % DIFF_QTILE == 0
    assert seq % SWA_QTILE == 0 and seq % PROJ_TILE == 0 and seq % ROW_TILE == 0

    h = x
    h_meta = jnp.concatenate([meta_tokens.astype(f32), jnp.zeros((BLK - N_META, D_MODEL), f32)], axis=0)[None]
    tabs = _rope_tables(N_META + jnp.arange(seq))
    tabs_meta = _rope_tables(jnp.minimum(jnp.arange(BLK), N_META - 1))
    tiles = dict(zip(("dd_diag", "dd_prev", "dd_meta", "dd_self", "ds_band", "ds_meta", "ds_self"),
                     _bias_tiles(rel_bias)))
    final_g = final_norm[None]
    lw = _stacked_weights(attn_norm, w_in, mla_q_norm, mla_w_qb, mla_kv_norm, mla_w_kvb, diff_lambda, diff_subln,
                          w_out, ffn_norm, w_gate, w_up, w_down)

    for l in range(DEPTH):
        lam_init = 0.8 - 0.6 * math.exp(-0.3 * l)
        last = l == DEPTH - 1

        meta_proj = _project(h_meta, l, lw, tabs_meta, BLK)
        proj = _project(h, l, lw, tabs, PROJ_TILE)
        if not last:
            y_meta = _mixers(meta_proj, None, tiles, l, lw, lam_init, swa_sinks, real=False)
            h_meta = _out_ffn(h_meta, *y_meta, l, lw, final_g, False, BLK)
        y = _mixers(proj, meta_proj, tiles, l, lw, lam_init, swa_sinks, real=True)
        h = _out_ffn(h, *y, l, lw, final_g, last, ROW_TILE)
    return h
```

```python
import functools
import math

import numpy as np
import jax
import jax.numpy as jnp
from jax import lax
from jax.experimental import pallas as pl
from jax.experimental.pallas import tpu as pltpu

D_MODEL = 1024
DEPTH = 2
N_META = 16
BLK = 128

MLA_HEADS = 6
MLA_Q_RANK = 256
MLA_KV_RANK = 128
MLA_NOPE = 64
MLA_ROPE = 32
MLA_V = 64
ROPE_THETA = 10000.0

DIFF_HEADS = 4
DIFF_QK = 32
DIFF_V = 64

SWA_HEADS = 6
SWA_KV_HEADS = 2
SWA_REP = SWA_HEADS // SWA_KV_HEADS
SWA_HD = 64
WINDOW = 128

REL_BUCKETS = 32
REL_MAX_DIST = 128
D_FF = 2816
NEG_INF = -1e30
EPS = 1e-6
LOG2E = math.log2(math.e)

MLA_SCALE = (MLA_NOPE + MLA_ROPE) ** -0.5
DIFF_SCALE = DIFF_QK ** -0.5
SWA_SCALE = SWA_HD ** -0.5

LANES = 128
HEAD_V = 64
BF16_ROWS = 16
ACC_ROWS = -(-(HEAD_V + 1) // BF16_ROWS) * BF16_ROWS


def _spans(widths):
    spans, o = {}, 0
    for name, w in widths:
        spans[name] = slice(o, o + w)
        o += w
    return spans, o


NAT, D_NAT = _spans((("c_q", MLA_Q_RANK), ("c_kv", MLA_KV_RANK), ("rope_a", LANES), ("rope_b", LANES),
                     ("dk", DIFF_HEADS * 2 * DIFF_QK), ("sk", SWA_KV_HEADS * SWA_HD)))
TR, D_TR = _spans((("dq", DIFF_HEADS * 2 * DIFF_QK), ("dv", DIFF_HEADS * DIFF_V), ("sq", SWA_HEADS * SWA_HD),
                   ("sv", SWA_KV_HEADS * SWA_HD)))
MLA_QK_W = MLA_HEADS * LANES
MLA_V_W = MLA_HEADS * MLA_V
VMEM_LIMIT = 56 * 1024 * 1024

MLA_QTILE = 2048
MLA_KTILE = 512
DIFF_QTILE = 1024
DIFF_TILE = 512
SWA_QTILE = 1024
PROJ_TILE = 1024
ROW_TILE = 512

f32 = jnp.float32
bf16 = jnp.bfloat16


def _bucket_thresholds():
    n = np.arange(0, 4 * REL_MAX_DIST)
    max_exact = REL_BUCKETS // 2
    nf = np.maximum(n, max_exact).astype(np.float32)
    large = max_exact + (np.log(nf / np.float32(max_exact)) / np.float32(math.log(REL_MAX_DIST / max_exact))
                         * np.float32(REL_BUCKETS - max_exact)).astype(np.int32)
    bucket = np.where(n < max_exact, n, np.minimum(large, REL_BUCKETS - 1))
    assert np.all(np.diff(bucket) >= 0) and bucket[-1] == REL_BUCKETS - 1
    return [int(np.argmax(bucket >= k)) for k in range(1, REL_BUCKETS)]


BUCKET_THR = _bucket_thresholds()
FAR_DIST = BUCKET_THR[-1]
assert FAR_DIST <= BLK - N_META + 1


def _dot(a, b):
    return jnp.dot(a, b, preferred_element_type=f32)


def _dot_nt(a, b):
    return lax.dot_general(a, b, (((1,), (1,)), ((), ())), preferred_element_type=f32)


def _rms(x, g):
    return (x * lax.rsqrt(jnp.mean(x * x, axis=-1, keepdims=True) + EPS)) * g


def _lane(shape):
    return lax.broadcasted_iota(jnp.int32, shape, len(shape) - 1)


def _row(shape):
    return lax.broadcasted_iota(jnp.int32, shape, 0)


def _bias_lookup(n, tab_ref, h):
    acc = jnp.full(n.shape, tab_ref[0, h], f32)
    for k, thr in enumerate(BUCKET_THR, start=1):
        acc = jnp.where(n >= thr, tab_ref[k, h], acc)
    return acc


def _bias_kernel(tab_ref, dd_diag, dd_prev, dd_meta, dd_self, ds_band, ds_meta, ds_self, *, td):
    rc = 32

    def rows_cols(r0, cols):
        a = r0 + lax.broadcasted_iota(jnp.int32, (rc, cols), 0)
        b = lax.broadcasted_iota(jnp.int32, (rc, cols), 1)
        return a, b

    def diff_body(i, carry, near_prev):
        r0 = pl.multiple_of(i * rc, rc)
        k, q = rows_cols(r0, td)
        for h in range(DIFF_HEADS):
            c = tab_ref[REL_BUCKETS - 1, h]
            dd_diag[h, pl.ds(r0, rc), :] = jnp.where(
                k <= q, (_bias_lookup(jnp.maximum(q - k, 0), tab_ref, h) - c) * LOG2E, NEG_INF)
            if near_prev:
                dd_prev[h, pl.ds(r0, rc), :] = (_bias_lookup(q + td - k, tab_ref, h) - c) * LOG2E
            else:
                dd_prev[h, pl.ds(r0, rc), :] = jnp.zeros((rc, td), f32)
        return carry

    n_far = (td - BLK) // rc
    lax.fori_loop(0, n_far, functools.partial(diff_body, near_prev=False), 0)
    lax.fori_loop(n_far, td // rc, functools.partial(diff_body, near_prev=True), 0)

    def blk_body(i, carry):
        r0 = pl.multiple_of(i * rc, rc)
        j, q = rows_cols(r0, td)
        j2, q2 = rows_cols(r0, BLK)
        self_ok = j2 <= jnp.minimum(q2, N_META - 1)
        for h in range(DIFF_HEADS):
            c = tab_ref[REL_BUCKETS - 1, h]
            dd_meta[0, h, pl.ds(r0, rc), :] = jnp.where(
                j < N_META, (_bias_lookup(N_META + q - j, tab_ref, h) - c) * LOG2E, NEG_INF)
            dd_meta[1, h, pl.ds(r0, rc), :] = jnp.where(j < N_META, 0.0, NEG_INF)
            dd_self[h, pl.ds(r0, rc), :] = jnp.where(
                self_ok, (_bias_lookup(jnp.maximum(q2 - j2, 0), tab_ref, h) - c) * LOG2E, NEG_INF)
        k, a = j2, q2
        for h in range(SWA_HEADS):
            hb = DIFF_HEADS + h
            near = _bias_lookup(jnp.maximum(a - k, 0), tab_ref, hb) * LOG2E
            cur = jnp.where(k <= a, near, NEG_INF)
            prev = jnp.where(k > a, _bias_lookup(a + BLK - k, tab_ref, hb) * LOG2E, NEG_INF)
            ds_band[0, h, pl.ds(r0, rc), :] = cur
            ds_band[0, h, pl.ds(BLK + r0, rc), :] = jnp.full((rc, BLK), NEG_INF, f32)
            ds_band[1, h, pl.ds(r0, rc), :] = prev
            ds_band[1, h, pl.ds(BLK + r0, rc), :] = cur
            ds_meta[0, h, pl.ds(r0, rc), :] = jnp.where(
                k < N_META, _bias_lookup(N_META + a - k, tab_ref, hb) * LOG2E, NEG_INF)
            far = jnp.full((rc, BLK), tab_ref[REL_BUCKETS - 1, hb], f32) * LOG2E
            ds_meta[1, h, pl.ds(r0, rc), :] = jnp.where(k < N_META, far, NEG_INF)
            ds_self[0, h, pl.ds(r0, rc), :] = jnp.where(self_ok, near, NEG_INF)
        return carry

    lax.fori_loop(0, BLK // rc, blk_body, 0)


def _bias_tiles(rel_bias):
    td = DIFF_TILE
    outs = (
        jax.ShapeDtypeStruct((DIFF_HEADS, td, td), f32),
        jax.ShapeDtypeStruct((DIFF_HEADS, td, td), f32),
        jax.ShapeDtypeStruct((2, DIFF_HEADS, BLK, td), f32),
        jax.ShapeDtypeStruct((DIFF_HEADS, BLK, BLK), f32),
        jax.ShapeDtypeStruct((2, SWA_HEADS, 2 * BLK, BLK), f32),
        jax.ShapeDtypeStruct((2, SWA_HEADS, BLK, LANES), f32),
        jax.ShapeDtypeStruct((1, SWA_HEADS, BLK, LANES), f32),
    )
    return pl.pallas_call(
        functools.partial(_bias_kernel, td=td),
        name="bias_tiles",
        out_shape=outs,
        in_specs=[pl.BlockSpec(memory_space=pltpu.SMEM)],
        compiler_params=pltpu.CompilerParams(vmem_limit_bytes=VMEM_LIMIT),
    )(rel_bias)


def _store_vt(out_ref, vt, heads):
    tm = vt.shape[1]
    ones_blk = jnp.where(_row((HEAD_V, tm)) == 0, 1.0, 0.0).astype(bf16)
    for hd in range(heads):
        out_ref[0, hd * LANES:hd * LANES + HEAD_V, :] = vt[hd * HEAD_V:(hd + 1) * HEAD_V].astype(bf16)
        out_ref[0, hd * LANES + HEAD_V:(hd + 1) * LANES, :] = ones_blk


def _proj_kernel(h_ref, g_ref, wnat_ref, wtr_ref, qn_ref, wqbt_ref, kvn_ref, wkb_ref, wvt_ref,
                 csqt_ref, cka_ref, ckb_ref,
                 mqt_ref, mk_ref, mvt_ref, dqt_ref, dk_ref, dvt_ref, sqt_ref, sk_ref, svt_ref):
    hn = _rms(h_ref[0], g_ref[...]).astype(bf16)
    proj = _dot(hn, wnat_ref[...])
    tr = _dot_nt(wtr_ref[...], hn)
    c_q = proj[:, NAT["c_q"]]
    c_kv = proj[:, NAT["c_kv"]]
    rope_a = proj[:, NAT["rope_a"]]
    rope_b = proj[:, NAT["rope_b"]]
    dk_ref[0] = proj[:, NAT["dk"]].astype(bf16)
    sk_ref[0] = proj[:, NAT["sk"]].astype(bf16)
    dqt_ref[0] = (tr[TR["dq"]] * (DIFF_SCALE * LOG2E)).astype(bf16)
    _store_vt(dvt_ref, tr[TR["dv"]], DIFF_HEADS)
    sqt_ref[0] = (tr[TR["sq"]] * (SWA_SCALE * LOG2E)).astype(bf16)
    svt_ref[0] = tr[TR["sv"]].astype(bf16)

    cqn = _rms(c_q, qn_ref[...]).astype(bf16)
    ckvn = _rms(c_kv, kvn_ref[...]).astype(bf16)
    qt = _dot_nt(wqbt_ref[...], cqn)
    kk = _dot(ckvn, wkb_ref[...])
    vt = _dot_nt(wvt_ref[...], ckvn)
    csqt = csqt_ref[...]
    k_rot = rope_a * cka_ref[...] + rope_b * ckb_ref[...]
    for hd in range(MLA_HEADS):
        sl = slice(hd * LANES, (hd + 1) * LANES)
        mqt_ref[0, sl, :] = (qt[sl] * csqt).astype(bf16)
        mk_ref[0, :, sl] = (kk[:, sl] + k_rot).astype(bf16)
    _store_vt(mvt_ref, vt, MLA_HEADS)


def _const_spec(shape):
    nd = len(shape)
    return pl.BlockSpec(shape, lambda *_: (0,) * nd)


def _layer_spec(shape, l, single=False):
    nd = len(shape)
    return pl.BlockSpec((None,) + tuple(shape), lambda *_: (l,) + (0,) * nd,
                        pipeline_mode=pl.Buffered(1) if single else None)


def _project(h, l, lw, tabs, tm):
    bsz, t, _ = h.shape
    width = {name: sp.stop - sp.start for name, sp in {**NAT, **TR}.items()}
    outs = ((MLA_QK_W, True), (MLA_QK_W, False), (MLA_HEADS * LANES, True), (width["dq"], True),
            (width["dk"], False), (DIFF_HEADS * LANES, True), (width["sq"], True), (width["sk"], False),
            (width["sv"], True))
    out_shape = tuple(jax.ShapeDtypeStruct((bsz, w, t) if tr else (bsz, t, w), bf16) for w, tr in outs)
    out_specs = tuple(pl.BlockSpec((1, w, tm), lambda b, i: (b, 0, i)) if tr
                      else pl.BlockSpec((1, tm, w), lambda b, i: (b, i, 0)) for w, tr in outs)
    tab_spec = pl.BlockSpec((tm, LANES), lambda b, i: (i, 0))
    in_specs = [
        pl.BlockSpec((1, tm, D_MODEL), lambda b, i: (b, i, 0)),
        _layer_spec((1, D_MODEL), l),
        _layer_spec((D_MODEL, D_NAT), l),
        _layer_spec((D_TR, D_MODEL), l),
        _layer_spec((1, MLA_Q_RANK), l),
        _layer_spec((MLA_QK_W, MLA_Q_RANK), l),
        _layer_spec((1, MLA_KV_RANK), l),
        _layer_spec((MLA_KV_RANK, MLA_QK_W), l),
        _layer_spec((MLA_V_W, MLA_KV_RANK), l),
        pl.BlockSpec((LANES, tm), lambda b, i: (0, i)),
        tab_spec, tab_spec,
    ]
    return pl.pallas_call(
        _proj_kernel,
        name="proj_real" if bsz > 1 else "proj_meta",
        out_shape=out_shape,
        grid=(bsz, t // tm),
        in_specs=in_specs,
        out_specs=out_specs,
        compiler_params=pltpu.CompilerParams(
            dimension_semantics=("parallel", "parallel"), vmem_limit_bytes=VMEM_LIMIT),
    )(h, lw["attn_norm"], lw["w_nat"], lw["w_tr"], lw["q_norm"], lw["w_qbt"], lw["kv_norm"], lw["w_kb"],
      lw["w_vt"], tabs["csqt"], tabs["cka"], tabs["ckb"])


def _flash_update(s, vt, m_ref, acc_ref, idx, cols=slice(None), n_rows=ACC_ROWS):
    m_prev = m_ref[idx, :, cols]
    m_new = jnp.maximum(m_prev, jnp.max(s, axis=0, keepdims=True))
    alpha = jnp.exp2(m_prev - m_new)
    p = jnp.exp2(s - m_new).astype(bf16)
    rows = slice(0, n_rows)
    acc_ref[idx, rows, cols] = acc_ref[idx, rows, cols] * alpha + _dot(vt[rows], p)
    m_ref[idx, :, cols] = m_new


def _init_stats(m_ref, acc_ref):
    m_ref[...] = jnp.full(m_ref.shape, NEG_INF, f32)
    acc_ref[...] = jnp.zeros(acc_ref.shape, f32)


def _normalized(acc_ref, idx, cols):
    return acc_ref[idx, 0:HEAD_V, cols] / acc_ref[idx, HEAD_V:HEAD_V + 1, cols]


def _mla_kernel(*refs, tq, tk, has_meta):
    if has_meta:
        qt_ref, k_ref, vt_ref, mk_ref, mvt_ref, o_ref, m_ref, acc_ref, s_ref = refs
    else:
        qt_ref, k_ref, vt_ref, o_ref, m_ref, acc_ref, s_ref = refs
    qi = pl.program_id(2)
    ratio = tq // tk
    _init_stats(m_ref, acc_ref)

    def head(hh):
        return slice(hh * LANES, (hh + 1) * LANES)

    if has_meta:
        for hh in range(2):
            s = _dot(mk_ref[0, 0:N_META, head(hh)], qt_ref[0, head(hh), :])
            _flash_update(s, mvt_ref[0, head(hh), 0:N_META], m_ref, acc_ref, hh, n_rows=LANES)

    def keys(j):
        return pl.ds(pl.multiple_of(j * tk, tk), tk)

    def stage_a(j, hh, cols):
        s_ref[hh, :, cols] = _dot(k_ref[0, keys(j), head(hh)], qt_ref[0, head(hh), cols])

    def stage_b(j, hh, cols, mask):
        s = s_ref[hh, :, cols]
        if mask is not None:
            s = jnp.where(mask, s, NEG_INF)
        _flash_update(s, vt_ref[0, head(hh), keys(j)], m_ref, acc_ref, hh, cols, n_rows=LANES)

    def blocks(cols):
        return [slice(c, c + tk) for c in range(cols.start, cols.stop, tk)]

    def tile_step(j, cols, mask, next_cols):
        chains = [(hh, cb) for cb in blocks(cols) for hh in range(2)]
        for idx, (hh, cb) in enumerate(chains):
            if idx + 1 < len(chains):
                stage_a(j, *chains[idx + 1])
            elif next_cols is not None:
                stage_a(j + 1, 0, blocks(next_cols)[0])
            chain_mask = None if mask is None else mask[:, cb.start - cols.start:cb.stop - cols.start]
            stage_b(j, hh, cb, chain_mask)

    full = slice(0, tq)
    stage_a(0, 0, blocks(full)[0])
    n_plain = qi * ratio

    def body(i, carry):
        tile_step(2 * i, full, None, full)
        tile_step(2 * i + 1, full, None, full)
        return carry

    lax.fori_loop(0, n_plain // 2, body, 0)
    if ratio % 2 == 1:
        @pl.when(n_plain % 2 == 1)
        def _():
            tile_step(n_plain - 1, full, None, full)

    for d in range(ratio):
        width = tq - d * tk
        key = _row((tk, width))
        qry = _lane((tk, width))
        if has_meta:
            mask = key <= qry
        else:
            mask = key <= jnp.minimum(qry, N_META - 1)
        next_cols = slice((d + 1) * tk, tq) if d + 1 < ratio else None
        tile_step(n_plain + d, slice(d * tk, tq), mask, next_cols)

    ot = jnp.concatenate([_normalized(acc_ref, 0, full), _normalized(acc_ref, 1, full)], axis=0)
    o_ref[0] = ot.T.astype(bf16)


def _mla_attention(mqt, mk, mvt, meta_kv, tq, tk):
    bsz, tlen, _ = mk.shape
    has_meta = meta_kv is not None
    npair = MLA_HEADS // 2
    in_specs = [
        pl.BlockSpec((1, 2 * LANES, tq), lambda b, p, i: (b, p, i)),
        pl.BlockSpec((1, tlen, 2 * LANES), lambda b, p, i: (b, 0, p)),
        pl.BlockSpec((1, 2 * LANES, tlen), lambda b, p, i: (b, p, 0)),
    ]
    args = [mqt, mk, mvt]
    if has_meta:
        in_specs += [
            pl.BlockSpec((1, BLK, 2 * LANES), lambda b, p, i: (0, 0, p)),
            pl.BlockSpec((1, 2 * LANES, BLK), lambda b, p, i: (0, p, 0)),
        ]
        args += list(meta_kv)
    return pl.pallas_call(
        functools.partial(_mla_kernel, tq=tq, tk=tk, has_meta=has_meta),
        name="mla_real" if has_meta else "mla_meta",
        out_shape=jax.ShapeDtypeStruct((bsz, tlen, npair * LANES), bf16),
        grid=(bsz, npair, tlen // tq),
        in_specs=in_specs,
        out_specs=pl.BlockSpec((1, tq, LANES), lambda b, p, i: (b, i, p)),
        scratch_shapes=[
            pltpu.VMEM((2, 1, tq), f32),
            pltpu.VMEM((2, LANES, tq), f32),
            pltpu.VMEM((2, tk, tq), f32),
        ],
        compiler_params=pltpu.CompilerParams(
            dimension_semantics=("parallel", "parallel", "arbitrary"), vmem_limit_bytes=VMEM_LIMIT),
    )(*args)


def _diff_kernel(*refs, tq, tk, has_meta, lam_init):
    if has_meta:
        (qt_ref, k_ref, vt_ref, mk_ref, mvt_ref, ddiag_ref, dprev_ref, dmeta_ref, lam_ref, subg_ref,
         o_ref, qs_ref, m_ref, acc_ref, s_ref) = refs
    else:
        (qt_ref, k_ref, vt_ref, ddiag_ref, lam_ref, subg_ref, o_ref, qs_ref, m_ref, acc_ref, s_ref) = refs
    qi = pl.program_id(2)
    ratio = tq // tk
    seg = 2 * tk
    _init_stats(m_ref, acc_ref)

    def head(hh):
        return slice(hh * LANES, (hh + 1) * LANES)

    def half_cols(h):
        return slice(h * seg, (h + 1) * seg)

    def map_cols(h, c):
        return slice(h * seg + c * tk, h * seg + (c + 1) * tk)

    group = _row((LANES, tk)) // DIFF_QK
    for h in range(ratio):
        qt = qt_ref[0, :, h * tk:(h + 1) * tk]
        for hh in range(2):
            for c in range(2):
                qs_ref[hh, :, map_cols(h, c)] = jnp.where(group == 2 * hh + c, qt, jnp.zeros_like(qt))

    def keys(j):
        return pl.ds(pl.multiple_of(j * tk, tk), tk)

    def stage_a(j, hh, h, c):
        s_ref[hh, :, map_cols(h, c)] = _dot(k_ref[0, keys(j), :], qs_ref[hh, :, map_cols(h, c)])

    def stage_b(j, hh, h, c, bias_ref):
        s = s_ref[hh, :, map_cols(h, c)]
        if bias_ref is not None:
            s = s + bias_ref[hh]
        _flash_update(s, vt_ref[0, head(hh), keys(j)], m_ref, acc_ref, hh, map_cols(h, c), n_rows=LANES)

    def tile_step(j, first_half, biases, next_first_half):
        chains = [(hh, h, c) for h in range(first_half, ratio) for hh in range(2) for c in range(2)]
        for idx, (hh, h, c) in enumerate(chains):
            if idx + 1 < len(chains):
                stage_a(j, *chains[idx + 1])
            elif next_first_half is not None:
                stage_a(j + 1, 0, next_first_half, 0)
            stage_b(j, hh, h, c, biases.get(h))

    if has_meta:
        for h in range(ratio):
            for hh in range(2):
                b = dmeta_ref[jnp.minimum(qi, 1), hh, 0:N_META] if h == 0 else dmeta_ref[1, hh, 0:N_META]
                s = _dot(mk_ref[0, 0:N_META], qs_ref[hh, :, half_cols(h)]) + jnp.concatenate([b, b], axis=1)
                _flash_update(s, mvt_ref[0, head(hh), 0:N_META], m_ref, acc_ref, hh, half_cols(h))

    stage_a(0, 0, 0, 0)
    first_diag = qi * ratio
    if has_meta:
        def body(i, carry):
            tile_step(2 * i, 0, {}, 0)
            tile_step(2 * i + 1, 0, {}, 0)
            return carry

        lax.fori_loop(0, jnp.maximum(first_diag - 2, 0) // 2, body, 0)

        @pl.when(qi >= 1)
        def _():
            tile_step(first_diag - 2, 0, {}, 0)
            tile_step(first_diag - 1, 0, {0: dprev_ref}, 0)

    for d in range(ratio):
        biases = {d: ddiag_ref}
        if d + 1 < ratio:
            biases[d + 1] = dprev_ref
        tile_step(first_diag + d, d, biases, d + 1 if d + 1 < ratio else None)

    lp = lam_ref[...]
    lam = (jnp.exp(jnp.sum(lp[0:1] * lp[1:2], axis=-1, keepdims=True))
           - jnp.exp(jnp.sum(lp[2:3] * lp[3:4], axis=-1, keepdims=True)) + lam_init)
    outs = []
    for hh in range(2):
        halves = [_normalized(acc_ref, hh, map_cols(h, 0)) - lam * _normalized(acc_ref, hh, map_cols(h, 1))
                  for h in range(ratio)]
        o = halves[0] if ratio == 1 else jnp.concatenate(halves, axis=1)
        ms = jnp.mean(o * o, axis=0, keepdims=True)
        outs.append(o * lax.rsqrt(ms + EPS))
    o_nat = jnp.concatenate(outs, axis=0).T
    o_ref[0] = ((o_nat * subg_ref[...]) * (1.0 - lam_init)).astype(bf16)


def _diff_attention(dqt, dk, dvt, meta_kv, tiles, l, lam_p, sub_g, lam_init, tq, tk):
    bsz, tlen, _ = dk.shape
    has_meta = meta_kv is not None
    assert tq % tk == 0 and (not has_meta or (tq // tk) % 2 == 0)
    npair = DIFF_HEADS // 2
    in_specs = [
        pl.BlockSpec((1, LANES, tq), lambda b, p, i: (b, p, i)),
        pl.BlockSpec((1, tlen, LANES), lambda b, p, i: (b, 0, p)),
        pl.BlockSpec((1, 2 * LANES, tlen), lambda b, p, i: (b, p, 0)),
    ]
    args = [dqt, dk, dvt]
    if has_meta:
        in_specs += [
            pl.BlockSpec((1, BLK, LANES), lambda b, p, i: (0, 0, p)),
            pl.BlockSpec((1, 2 * LANES, BLK), lambda b, p, i: (0, p, 0)),
            pl.BlockSpec((2, tk, tk), lambda b, p, i: (p, 0, 0)),
            pl.BlockSpec((2, tk, tk), lambda b, p, i: (p, 0, 0)),
            pl.BlockSpec((2, 2, BLK, tk), lambda b, p, i: (0, p, 0, 0)),
        ]
        args += list(meta_kv) + [tiles["dd_diag"], tiles["dd_prev"], tiles["dd_meta"]]
    else:
        in_specs += [pl.BlockSpec((2, tk, tk), lambda b, p, i: (p, 0, 0))]
        args += [tiles["dd_self"]]
    in_specs += [_layer_spec((4, DIFF_QK), l), _layer_spec((1, LANES), l)]
    args += [lam_p, sub_g]
    return pl.pallas_call(
        functools.partial(_diff_kernel, tq=tq, tk=tk, has_meta=has_meta, lam_init=lam_init),
        name="diff_real" if has_meta else "diff_meta",
        out_shape=jax.ShapeDtypeStruct((bsz, tlen, npair * LANES), bf16),
        grid=(bsz, npair, tlen // tq),
        in_specs=in_specs,
        out_specs=pl.BlockSpec((1, tq, LANES), lambda b, p, i: (b, i, p)),
        scratch_shapes=[
            pltpu.VMEM((2, LANES, 2 * tq), bf16),
            pltpu.VMEM((2, 1, 2 * tq), f32),
            pltpu.VMEM((2, LANES, 2 * tq), f32),
            pltpu.VMEM((2, tk, 2 * tq), f32),
        ],
        compiler_params=pltpu.CompilerParams(
            dimension_semantics=("parallel", "parallel", "arbitrary"), vmem_limit_bytes=VMEM_LIMIT),
    )(*args)


def _swa_kernel(*refs, tq, has_band, layer):
    if has_band:
        qt_ref, k_ref, vt_ref, mk_ref, mvt_ref, dband_ref, dmeta_ref, sink_ref, o_ref, sm_ref, sb_ref = refs
    else:
        qt_ref, mk_ref, mvt_ref, dmeta_ref, sink_ref, o_ref, sm_ref = refs
    qi = pl.program_id(1)
    feat = _row((LANES, BLK))
    first_head = _lane((1, 2 * BLK)) < BLK
    n_meta_var = dmeta_ref.shape[0]

    def block(sub):
        blk = qi * (tq // BLK) + sub
        c0 = sub * BLK
        mvar = jnp.minimum(blk, n_meta_var - 1)
        if has_band:
            bvar = jnp.minimum(blk, 1)
            koff = pl.multiple_of(jnp.maximum(blk - 1, 0) * BLK, BLK)
            kband = k_ref[0, pl.ds(koff, 2 * BLK), :]
            vtband = vt_ref[0, :, pl.ds(koff, 2 * BLK)]

        def stage_a(r):
            q2t = qt_ref[0, r * LANES:(r + 1) * LANES, c0:c0 + BLK]
            zero = jnp.zeros_like(q2t)
            qs = jnp.concatenate([jnp.where(feat < SWA_HD, q2t, zero), jnp.where(feat >= SWA_HD, q2t, zero)],
                                 axis=1)
            sm_ref[r] = _dot(mk_ref[0], qs)
            if has_band:
                sb_ref[r] = _dot(kband, qs)

        def stage_b(r):
            h0, h1 = r, SWA_REP + r
            sink = jnp.where(first_head, sink_ref[layer, h0], sink_ref[layer, h1]) * LOG2E
            s_meta = sm_ref[r] + jnp.concatenate([dmeta_ref[mvar, h0], dmeta_ref[mvar, h1]], axis=1)
            m = jnp.maximum(jnp.max(s_meta, axis=0, keepdims=True), sink)
            if has_band:
                s_band = sb_ref[r] + jnp.concatenate([dband_ref[bvar, h0], dband_ref[bvar, h1]], axis=1)
                m = jnp.maximum(m, jnp.max(s_band, axis=0, keepdims=True))
            p_meta = jnp.exp2(s_meta - m)
            denom = jnp.sum(p_meta, axis=0, keepdims=True) + jnp.exp2(sink - m)
            acc = _dot(mvt_ref[0], p_meta.astype(bf16))
            if has_band:
                p_band = jnp.exp2(s_band - m)
                denom = denom + jnp.sum(p_band, axis=0, keepdims=True)
                acc = acc + _dot(vtband, p_band.astype(bf16))
            ot = acc / denom
            pair = jnp.where(feat < SWA_HD, ot[:, 0:BLK], ot[:, BLK:2 * BLK])
            o_ref[0, c0:c0 + BLK, r * LANES:(r + 1) * LANES] = pair.T.astype(bf16)

        return stage_a, stage_b

    stages = [block(sub) for sub in range(tq // BLK)]
    chains = [(sub, r) for sub in range(tq // BLK) for r in range(SWA_REP)]
    stages[0][0](0)
    stages[0][0](1)
    for i, (sub, r) in enumerate(chains):
        if i + 2 < len(chains):
            nsub, nr = chains[i + 2]
            stages[nsub][0](nr)
        stages[sub][1](r)


def _swa_attention(sqt, sk, svt, meta_kv, tiles, l, sinks, tq):
    bsz, width, tlen = sqt.shape
    has_band = meta_kv is not None
    smem = pl.BlockSpec(memory_space=pltpu.SMEM)
    if has_band:
        in_specs = [
            pl.BlockSpec((1, width, tq), lambda b, i: (b, 0, i)),
            pl.BlockSpec((1, tlen, LANES), lambda b, i: (b, 0, 0)),
            pl.BlockSpec((1, LANES, tlen), lambda b, i: (b, 0, 0)),
            _const_spec((1, BLK, LANES)),
            _const_spec((1, LANES, BLK)),
            _const_spec(tiles["ds_band"].shape),
            _const_spec(tiles["ds_meta"].shape),
            smem,
        ]
        args = [sqt, sk, svt, meta_kv[0], meta_kv[1], tiles["ds_band"], tiles["ds_meta"], sinks]
    else:
        in_specs = [
            pl.BlockSpec((1, width, tq), lambda b, i: (b, 0, i)),
            _const_spec((1, BLK, LANES)),
            _const_spec((1, LANES, BLK)),
            _const_spec(tiles["ds_self"].shape),
            smem,
        ]
        args = [sqt, sk, svt, tiles["ds_self"], sinks]
    return pl.pallas_call(
        functools.partial(_swa_kernel, tq=tq, has_band=has_band, layer=l),
        name="swa_real" if has_band else "swa_meta",
        out_shape=jax.ShapeDtypeStruct((bsz, tlen, width), bf16),
        grid=(bsz, tlen // tq),
        in_specs=in_specs,
        out_specs=pl.BlockSpec((1, tq, width), lambda b, i: (b, i, 0)),
        scratch_shapes=[pltpu.VMEM((SWA_REP, BLK, 2 * BLK), f32)]
        + ([pltpu.VMEM((SWA_REP, 2 * BLK, 2 * BLK), f32)] if has_band else []),
        compiler_params=pltpu.CompilerParams(
            dimension_semantics=("parallel", "parallel"), vmem_limit_bytes=VMEM_LIMIT),
    )(*args)


def _out_ffn_kernel(h_ref, ya_ref, yb_ref, yc_ref, wo_ref, g_ref, wg_ref, wu_ref, wd_ref, fg_ref, o_ref, *, final):
    y = jnp.concatenate([ya_ref[0], yb_ref[0], yc_ref[0]], axis=1)
    h1 = h_ref[0] + _dot(y, wo_ref[...])
    hn = _rms(h1, g_ref[...]).astype(bf16)
    gate = _dot(hn, wg_ref[...])
    up = _dot(hn, wu_ref[...])
    act = ((gate * jax.nn.sigmoid(gate)) * up).astype(bf16)
    h2 = h1 + _dot(act, wd_ref[...])
    if final:
        h2 = _rms(h2, fg_ref[...])
    o_ref[0] = h2


def _out_ffn(h, ya, yb, yc, l, lw, final_g, final, tm):
    bsz, t, _ = h.shape

    def row(w):
        return pl.BlockSpec((1, tm, w), lambda b, i: (b, i, 0))

    in_specs = [
        row(D_MODEL), row(MLA_HEADS * MLA_V), row(DIFF_HEADS * DIFF_V), row(SWA_HEADS * SWA_HD),
        _layer_spec((D_MODEL, D_MODEL), l, single=True),
        _layer_spec((1, D_MODEL), l),
        _layer_spec((D_MODEL, D_FF), l, single=True),
        _layer_spec((D_MODEL, D_FF), l, single=True),
        _layer_spec((D_FF, D_MODEL), l, single=True),
        _const_spec((1, D_MODEL)),
    ]
    return pl.pallas_call(
        functools.partial(_out_ffn_kernel, final=final),
        name="out_ffn_real" if bsz > 1 else "out_ffn_meta",
        out_shape=jax.ShapeDtypeStruct((bsz, t, D_MODEL), f32),
        grid=(bsz, t // tm),
        in_specs=in_specs,
        out_specs=row(D_MODEL),
        compiler_params=pltpu.CompilerParams(
            dimension_semantics=("parallel", "parallel"), vmem_limit_bytes=VMEM_LIMIT),
    )(h, ya, yb, yc, lw["w_o"], lw["ffn_norm"], lw["w_gate"], lw["w_up"], lw["w_down"], final_g)


def _rot_half(w):
    half = MLA_ROPE // 2
    return jnp.concatenate([-w[..., half:], w[..., :half]], axis=-1)


def _stacked_weights(attn_norm, w_in, mla_q_norm, mla_w_qb, mla_kv_norm, mla_w_kvb, diff_lambda, diff_subln, w_out,
                     ffn_norm, w_gate, w_up, w_down):
    depth = w_in.shape[0]
    parts, o = {}, 0
    for name, width in (("c_q", MLA_Q_RANK), ("c_kv", MLA_KV_RANK), ("k_rope", MLA_ROPE),
                        ("dq", DIFF_HEADS * 2 * DIFF_QK), ("dk", DIFF_HEADS * 2 * DIFF_QK),
                        ("dv", DIFF_HEADS * DIFF_V), ("sq", SWA_HEADS * SWA_HD), ("sk", SWA_KV_HEADS * SWA_HD),
                        ("sv", SWA_KV_HEADS * SWA_HD)):
        parts[name] = w_in[:, :, o:o + width]
        o += width
    kr = parts["k_rope"]
    ksw = _rot_half(kr)
    z64 = jnp.zeros((depth, D_MODEL, MLA_NOPE), f32)
    sq = parts["sq"].reshape(depth, D_MODEL, SWA_KV_HEADS, SWA_REP, SWA_HD)
    sq = sq.transpose(0, 1, 3, 2, 4).reshape(depth, D_MODEL, SWA_HEADS * SWA_HD)
    w_nat = jnp.concatenate([parts["c_q"], parts["c_kv"], z64, kr, kr, z64, ksw, ksw, parts["dk"], parts["sk"]],
                            axis=-1)
    w_tr = jnp.swapaxes(jnp.concatenate([parts["dq"], parts["dv"], sq, parts["sv"]], axis=-1), 1, 2)
    assert w_nat.shape == (depth, D_MODEL, D_NAT) and w_tr.shape == (depth, D_TR, D_MODEL)

    wq = mla_w_qb.reshape(depth, MLA_Q_RANK, MLA_HEADS, MLA_NOPE + MLA_ROPE)
    rope = wq[..., MLA_NOPE:]
    w_qbt = jnp.concatenate([wq[..., :MLA_NOPE], rope, _rot_half(rope)], axis=-1)
    w_qbt = jnp.swapaxes(w_qbt.reshape(depth, MLA_Q_RANK, MLA_HEADS * LANES), 1, 2)

    wkv = mla_w_kvb.reshape(depth, MLA_KV_RANK, MLA_HEADS, MLA_NOPE + MLA_V)
    w_kb = jnp.concatenate([wkv[..., :MLA_NOPE], jnp.zeros(wkv.shape[:3] + (LANES - MLA_NOPE,), f32)], axis=-1)
    w_kb = w_kb.reshape(depth, MLA_KV_RANK, MLA_HEADS * LANES)
    w_vt = jnp.swapaxes(wkv[..., MLA_NOPE:].reshape(depth, MLA_KV_RANK, MLA_HEADS * MLA_V), 1, 2)

    n_ab = MLA_HEADS * MLA_V + DIFF_HEADS * DIFF_V
    wo_c = w_out[:, n_ab:].reshape(depth, SWA_KV_HEADS, SWA_REP, SWA_HD, D_MODEL)
    wo_c = wo_c.transpose(0, 2, 1, 3, 4).reshape(depth, SWA_HEADS * SWA_HD, D_MODEL)
    w_o = jnp.concatenate([w_out[:, :n_ab], wo_c], axis=1)
    return {
        "attn_norm": attn_norm[:, None], "w_nat": w_nat.astype(bf16), "w_tr": w_tr.astype(bf16),
        "q_norm": mla_q_norm[:, None], "w_qbt": w_qbt.astype(bf16),
        "kv_norm": mla_kv_norm[:, None], "w_kb": w_kb.astype(bf16), "w_vt": w_vt.astype(bf16),
        "diff_lambda": diff_lambda, "sub_g": jnp.concatenate([diff_subln, diff_subln], axis=-1)[:, None],
        "w_o": w_o.astype(bf16), "ffn_norm": ffn_norm[:, None],
        "w_gate": w_gate.astype(bf16), "w_up": w_up.astype(bf16), "w_down": w_down.astype(bf16),
    }


def _rope_lane_selectors():
    half = MLA_ROPE // 2
    idx = np.arange(half)
    sel = np.zeros((3, 2 * half, LANES), np.float32)
    for rep in range(2):
        sel[0, idx, MLA_NOPE + rep * half + idx] = 1.0
        sel[0, half + idx, MLA_NOPE + MLA_ROPE + rep * half + idx] = 1.0
        for blk in range(2):
            lane0 = MLA_NOPE + (2 * blk + rep) * half
            sel[1, idx, lane0 + idx] = 1.0
            sel[2, half + idx, lane0 + idx] = 1.0
    return sel


def _rope_tables(pos):
    inv_freq = ROPE_THETA ** (-jnp.arange(0, MLA_ROPE, 2, dtype=f32) / MLA_ROPE)
    ang = pos.astype(f32)[:, None] * inv_freq[None, :]
    cos_sin = jnp.concatenate([jnp.cos(ang), jnp.sin(ang)], axis=1)
    placed = jnp.einsum("nk,tkl->tnl", cos_sin, jnp.asarray(_rope_lane_selectors()),
                        precision=lax.Precision.HIGHEST)
    nope_ones = (np.arange(LANES) < MLA_NOPE).astype(np.float32)[None, :]
    return {
        "csqt": ((placed[0] + nope_ones) * (MLA_SCALE * LOG2E)).T,
        "cka": placed[1],
        "ckb": placed[2],
    }


def _mixers(proj, meta_proj, tiles, l, lw, lam_init, sinks, real):
    mqt, mk, mvt, dqt, dk, dvt, sq, sk, sv = proj
    lam_p, sub_g = lw["diff_lambda"], lw["sub_g"]
    if real:
        m_mk, m_mvt, m_dk, m_dvt, m_sk, m_sv = (meta_proj[i] for i in (1, 2, 4, 5, 7, 8))
        ya = _mla_attention(mqt, mk, mvt, (m_mk, m_mvt), MLA_QTILE, MLA_KTILE)
        yb = _diff_attention(dqt, dk, dvt, (m_dk, m_dvt), tiles, l, lam_p, sub_g, lam_init, DIFF_QTILE, DIFF_TILE)
        yc = _swa_attention(sq, sk, sv, (m_sk, m_sv), tiles, l, sinks, SWA_QTILE)
    else:
        ya = _mla_attention(mqt, mk, mvt, None, BLK, BLK)
        yb = _diff_attention(dqt, dk, dvt, None, tiles, l, lam_p, sub_g, lam_init, BLK, BLK)
        yc = _swa_attention(sq, sk, sv, None, tiles, l, sinks, BLK)
    return ya, yb, yc


def kernel(x, meta_tokens, rel_bias, attn_norm, w_in, mla_q_norm, mla_w_qb, mla_kv_norm, mla_w_kvb, diff_lambda,
           diff_subln, swa_sinks, w_out, ffn_norm, w_gate, w_up, w_down, final_norm):
    bsz, seq, _ = x.shape
    assert seq % MLA_QTILE == 0 and MLA_QTILE % MLA_KTILE == 0 and seq % DIFF_QTILE == 0
    assert seq % SWA_QTILE == 0 and seq % PROJ_TILE == 0 and seq % ROW_TILE == 0

    h = x
    h_meta = jnp.concatenate([meta_tokens.astype(f32), jnp.zeros((BLK - N_META, D_MODEL), f32)], axis=0)[None]
    tabs = _rope_tables(N_META + jnp.arange(seq))
    tabs_meta = _rope_tables(jnp.minimum(jnp.arange(BLK), N_META - 1))
    tiles = dict(zip(("dd_diag", "dd_prev", "dd_meta", "dd_self", "ds_band", "ds_meta", "ds_self"),
                     _bias_tiles(rel_bias)))
    final_g = final_norm[None]
    lw = _stacked_weights(attn_norm, w_in, mla_q_norm, mla_w_qb, mla_kv_norm, mla_w_kvb, diff_lambda, diff_subln,
                          w_out, ffn_norm, w_gate, w_up, w_down)

    for l in range(DEPTH):
        lam_init = 0.8 - 0.6 * math.exp(-0.3 * l)
        last = l == DEPTH - 1

        meta_proj = _project(h_meta, l, lw, tabs_meta, BLK)
        proj = _project(h, l, lw, tabs, PROJ_TILE)
        if not last:
            y_meta = _mixers(meta_proj, None, tiles, l, lw, lam_init, swa_sinks, real=False)
            h_meta = _out_ffn(h_meta, *y_meta, l, lw, final_g, False, BLK)
        y = _mixers(proj, meta_proj, tiles, l, lw, lam_init, swa_sinks, real=True)
        h = _out_ffn(h, *y, l, lw, final_g, last, ROW_TILE)
    return h
```
